```python
import math
import jax, jax.numpy as jnp
from jax import lax
import numpy as np

D_MODEL = 1024
BATCH = 8
SEQ = 2048
DEPTH = 1
DEC_BATCH = 128
DEC_SEQ = 1
PAST_LEN = 8192
PAGE_SIZE = 128

D_MIX = 2 * D_MODEL
D_ATTN = D_MIX // 2
D_SSM = D_MIX - D_ATTN
MLA_HEADS = 8
QK_NOPE = 128
QK_ROPE = 64
V_HEAD = D_ATTN // MLA_HEADS
Q_RANK = 384
KV_RANK = 256
ROPE_THETA = 10000.0
SOFTMAX_SCALE = (QK_NOPE + QK_ROPE) ** -0.5
SSD_HEADDIM = 64
SSD_HEADS = D_SSM // SSD_HEADDIM
SSD_GROUPS = 2
HEADS_PER_GROUP = SSD_HEADS // SSD_GROUPS
D_STATE = 128
CONV_W = 4
CONV_DIM = D_SSM + 2 * SSD_GROUPS * D_STATE
CHUNK = 128
Q_BLOCK = 128
EPS = 1e-6
NEG_BIG = -1e30
SPLIT_POINTS = (Q_RANK,
                Q_RANK + KV_RANK,
                Q_RANK + KV_RANK + QK_ROPE,
                Q_RANK + KV_RANK + QK_ROPE + D_ATTN,
                Q_RANK + KV_RANK + QK_ROPE + D_ATTN + D_SSM,
                Q_RANK + KV_RANK + QK_ROPE + D_ATTN + D_SSM + CONV_DIM)
D_IN_PROJ = SPLIT_POINTS[-1] + SSD_HEADS

kernel_name = 'hymba_mla_ssd_decode_step'


def _rmsnorm(x, w):
    xf = x.astype(jnp.float32)
    y = xf * lax.rsqrt(jnp.mean(xf * xf, axis=-1, keepdims=True) + EPS)
    return (y * w.astype(jnp.float32)).astype(x.dtype)


def _rope(x, pos):
    half = QK_ROPE // 2
    inv_freq = ROPE_THETA ** (-jnp.arange(half, dtype=jnp.float32) / half)
    ang = pos.astype(jnp.float32)[:, None] * inv_freq[None, :]
    cos = jnp.cos(ang)[None, :, None, :]
    sin = jnp.sin(ang)[None, :, None, :]
    xf = x.astype(jnp.float32)
    x1, x2 = xf[..., :half], xf[..., half:]
    return jnp.concatenate([x1 * cos - x2 * sin, x1 * sin + x2 * cos], axis=-1).astype(x.dtype)


def _in_projection(x, lw):
    h = _rmsnorm(x, lw['norm_pre'])
    return jnp.split(h @ lw['w_in'], SPLIT_POINTS, axis=-1)


def _mla_qk(q_a, c_raw, kr_raw, pos, lw):
    b, s, _ = q_a.shape
    q = (_rmsnorm(q_a, lw['q_a_norm']) @ lw['w_q_b']).reshape(b, s, MLA_HEADS, QK_NOPE + QK_ROPE)
    q_rope = _rope(q[..., QK_NOPE:], pos)
    q_lat = jnp.einsum('bshn,rhn->bshr', q[..., :QK_NOPE], lw['w_uk'])
    ckv = _rmsnorm(c_raw, lw['kv_a_norm'])
    k_rope = _rope(kr_raw[:, :, None, :], pos)[:, :, 0, :]
    return q_lat, q_rope, ckv, k_rope


def _latent_attention(q_lat, q_rope, ckv, k_rope, q_pos, k_pos):
    sc = (jnp.einsum('bqhr,bkr->bhqk', q_lat, ckv)
          + jnp.einsum('bqhe,bke->bhqk', q_rope, k_rope)).astype(jnp.float32) * SOFTMAX_SCALE
    sc = jnp.where(k_pos[None, None, None, :] <= q_pos[None, None, :, None], sc, NEG_BIG)
    p = jax.nn.softmax(sc, axis=-1).astype(ckv.dtype)
    return jnp.einsum('bhqk,bkr->bqhr', p, ckv)


def _mla_out(o_lat, z_attn, lw):
    b, s = o_lat.shape[:2]
    o = jnp.einsum('bshr,rhv->bshv', o_lat, lw['w_uv']).reshape(b, s, D_ATTN)
    return o * jax.nn.silu(z_attn)


def _causal_conv(xbc, prev, lw):
    s = xbc.shape[1]
    xp = jnp.concatenate([prev.astype(xbc.dtype), xbc], axis=1)
    w = lw['conv_w']
    acc = lw['conv_b'] + xp[:, 0:s, :] * w[0]
    for k in range(1, CONV_W):
        acc = acc + xp[:, k:k + s, :] * w[k]
    return jax.nn.silu(acc), xp[:, -(CONV_W - 1):, :]


def _ssd_pre(xbc_act, dt_raw, lw):
    b, s, _ = xbc_act.shape
    f = xbc_act.astype(jnp.float32)
    xs = f[..., :D_SSM].reshape(b, s, SSD_GROUPS, HEADS_PER_GROUP, SSD_HEADDIM)
    bm = f[..., D_SSM:D_SSM + SSD_GROUPS * D_STATE].reshape(b, s, SSD_GROUPS, D_STATE)
    cm = f[..., D_SSM + SSD_GROUPS * D_STATE:].reshape(b, s, SSD_GROUPS, D_STATE)
    dt = jax.nn.softplus(dt_raw.astype(jnp.float32) + lw['dt_bias'].astype(jnp.float32))
    dt = dt.reshape(b, s, SSD_GROUPS, HEADS_PER_GROUP)
    a = -jnp.exp(lw['a_log'].astype(jnp.float32)).reshape(SSD_GROUPS, HEADS_PER_GROUP)
    return xs, bm, cm, dt, dt * a


def _segsum(x):
    t = x.shape[-1]
    xr = jnp.broadcast_to(x[..., :, None], x.shape + (t,))
    idx = jnp.arange(t)
    cs = jnp.cumsum(jnp.where(idx[:, None] > idx[None, :], xr, 0.0), axis=-2)
    return jnp.where(idx[:, None] >= idx[None, :], cs, -jnp.inf)


def _ssd_chunked(xs, dt, da, bm, cm):
    b, s = xs.shape[:2]
    nc = s // CHUNK
    xdt = (xs * dt[..., None]).reshape(b, nc, CHUNK, SSD_GROUPS, HEADS_PER_GROUP, SSD_HEADDIM)
    a = da.reshape(b, nc, CHUNK, SSD_GROUPS, HEADS_PER_GROUP).transpose(0, 3, 4, 1, 2)
    bc = bm.reshape(b, nc, CHUNK, SSD_GROUPS, D_STATE)
    cc = cm.reshape(b, nc, CHUNK, SSD_GROUPS, D_STATE)
    a_cum = jnp.cumsum(a, axis=-1)
    decay_in = jnp.exp(_segsum(a))
    cb = jnp.einsum('bclgn,bcsgn->bcgls', cc, bc)
    y_diag = jnp.einsum('bcgls,bgjcls,bcsgjp->bclgjp', cb, decay_in, xdt)
    decay_to_end = jnp.exp(a_cum[..., -1:] - a_cum)
    chunk_states = jnp.einsum('bclgn,bgjcl,bclgjp->bcgjpn', bc, decay_to_end, xdt)
    chunk_states = jnp.concatenate([jnp.zeros_like(chunk_states[:, :1]), chunk_states], axis=1)
    decay_chunk = jnp.exp(_segsum(jnp.pad(a_cum[..., -1], ((0, 0), (0, 0), (0, 0), (1, 0)))))
    states = jnp.einsum('bgjzc,bcgjpn->bzgjpn', decay_chunk, chunk_states)
    y_off = jnp.einsum('bclgn,bcgjpn,bgjcl->bclgjp', cc, states[:, :-1], jnp.exp(a_cum))
    y = (y_diag + y_off).reshape(b, s, SSD_GROUPS, HEADS_PER_GROUP, SSD_HEADDIM)
    return y, states[:, -1]


def _ssd_recurrent(xs, dt, da, bm, cm, h0):
    def step(h, inp):
        x_t, dt_t, da_t, b_t, c_t = inp
        h = jnp.exp(da_t)[..., None, None] * h + jnp.einsum('bgjp,bgn->bgjpn', x_t * dt_t[..., None], b_t)
        return h, jnp.einsum('bgjpn,bgn->bgjp', h, c_t)
    seq_major = tuple(jnp.swapaxes(v, 0, 1) for v in (xs, dt, da, bm, cm))
    h, ys = lax.scan(step, h0, seq_major)
    return jnp.swapaxes(ys, 0, 1), h


def _ssd_out(y, xs, z_ssm, lw):
    b, s = y.shape[:2]
    d = lw['d_skip'].astype(jnp.float32).reshape(SSD_GROUPS, HEADS_PER_GROUP)[..., None]
    y = (y + d * xs).reshape(b, s, D_SSM)
    gated = (y * jax.nn.silu(z_ssm.astype(jnp.float32))).reshape(b, s, SSD_GROUPS, D_SSM // SSD_GROUPS)
    out = _rmsnorm(gated, lw['ssm_norm'].reshape(SSD_GROUPS, D_SSM // SSD_GROUPS))
    return out.reshape(b, s, D_SSM).astype(z_ssm.dtype)


def _merge(x, o_attn, y_ssm, lw):
    mix = jnp.concatenate([o_attn, y_ssm], axis=-1)
    return x + _rmsnorm(mix @ lw['w_out'], lw['norm_post'])


def _prompt_layer(x, lw):
    b, s, _ = x.shape
    pos = jnp.arange(s, dtype=jnp.int32)
    q_a, c_raw, kr_raw, z_attn, z_ssm, xbc, dt_raw = _in_projection(x, lw)
    q_lat, q_rope, ckv, k_rope = _mla_qk(q_a, c_raw, kr_raw, pos, lw)
    nb = s // Q_BLOCK
    blocks = (q_lat.reshape(b, nb, Q_BLOCK, MLA_HEADS, KV_RANK).swapaxes(0, 1),
              q_rope.reshape(b, nb, Q_BLOCK, MLA_HEADS, QK_ROPE).swapaxes(0, 1),
              pos.reshape(nb, Q_BLOCK))
    o_lat = lax.map(lambda blk: _latent_attention(blk[0], blk[1], ckv, k_rope, blk[2], pos), blocks)
    o_lat = o_lat.swapaxes(0, 1).reshape(b, s, MLA_HEADS, KV_RANK)
    o_attn = _mla_out(o_lat, z_attn, lw)
    conv_prev = jnp.zeros((b, CONV_W - 1, CONV_DIM), xbc.dtype)
    xbc_act, conv_state = _causal_conv(xbc, conv_prev, lw)
    xs, bm, cm, dt, da = _ssd_pre(xbc_act, dt_raw, lw)
    y, h_final = _ssd_chunked(xs, dt, da, bm, cm)
    y_ssm = _ssd_out(y, xs, z_ssm, lw)
    out = _merge(x, o_attn, y_ssm, lw)
    return out, ckv, k_rope, conv_state, h_final.reshape(b, SSD_HEADS, SSD_HEADDIM, D_STATE)


def _sample_layer(x, cache_ckv, cache_krope, conv_prev, h0, page_table, lw):
    b, s, _ = x.shape
    past = page_table.shape[1] * PAGE_SIZE
    pos = past + jnp.arange(s, dtype=jnp.int32)
    q_a, c_raw, kr_raw, z_attn, z_ssm, xbc, dt_raw = _in_projection(x, lw)
    q_lat, q_rope, ckv, k_rope = _mla_qk(q_a, c_raw, kr_raw, pos, lw)
    ckv_past = cache_ckv[page_table].reshape(b, past, KV_RANK)
    kr_past = cache_krope[page_table].reshape(b, past, QK_ROPE)
    keys_c = jnp.concatenate([ckv_past.astype(ckv.dtype), ckv], axis=1)
    keys_r = jnp.concatenate([kr_past.astype(k_rope.dtype), k_rope], axis=1)
    k_pos = jnp.arange(past + s, dtype=jnp.int32)
    o_lat = _latent_attention(q_lat, q_rope, keys_c, keys_r, pos, k_pos)
    o_attn = _mla_out(o_lat, z_attn, lw)
    xbc_act, conv_state = _causal_conv(xbc, conv_prev, lw)
    xs, bm, cm, dt, da = _ssd_pre(xbc_act, dt_raw, lw)
    h_init = h0.astype(jnp.float32).reshape(b, SSD_GROUPS, HEADS_PER_GROUP, SSD_HEADDIM, D_STATE)
    y, h = _ssd_recurrent(xs, dt, da, bm, cm, h_init)
    y_ssm = _ssd_out(y, xs, z_ssm, lw)
    out = _merge(x, o_attn, y_ssm, lw)
    return out, ckv, k_rope, conv_state, h.reshape(b, SSD_HEADS, SSD_HEADDIM, D_STATE)


def setup_inputs(seed: int = 0) -> dict:
    key = jax.random.key(seed)
    ks = jax.random.split(key, 24)
    f32 = jnp.float32
    n_pages = PAST_LEN // PAGE_SIZE
    n_used = DEC_BATCH * n_pages
    n_phys = n_used + n_used // 4

    def nrm(k, shape, scale):
        return jax.random.normal(k, shape, f32) * scale

    def gain(k, n):
        return 1.0 + 0.05 * jax.random.normal(k, (DEPTH, n), f32)

    page_table = jax.random.permutation(ks[0], n_phys)[:n_used].reshape(DEC_BATCH, n_pages).astype(jnp.int32)
    dt0 = jnp.exp(jax.random.uniform(ks[1], (DEPTH, SSD_HEADS), f32, math.log(1e-3), math.log(1e-1)))
    dt_bias = dt0 + jnp.log(-jnp.expm1(-dt0))
    a_log = jnp.log(jax.random.uniform(ks[2], (DEPTH, SSD_HEADS), f32, 1.0, 16.0))
    return {
        'x_prompt': nrm(ks[3], (BATCH, SEQ, D_MODEL), 1.0),
        'x_sample': nrm(ks[4], (DEC_BATCH, DEC_SEQ, D_MODEL), 1.0),
        'cache_ckv': nrm(ks[5], (DEPTH, n_phys, PAGE_SIZE, KV_RANK), 1.0),
        'cache_krope': nrm(ks[6], (DEPTH, n_phys, PAGE_SIZE, QK_ROPE), 1.0),
        'state_conv': nrm(ks[7], (DEPTH, DEC_BATCH, CONV_W - 1, CONV_DIM), 1.0),
        'state_ssm': nrm(ks[8], (DEPTH, DEC_BATCH, SSD_HEADS, SSD_HEADDIM, D_STATE), 0.1),
        'page_table': page_table,
        'norm_pre': gain(ks[9], D_MODEL),
        'w_in': nrm(ks[10], (DEPTH, D_MODEL, D_IN_PROJ), D_MODEL ** -0.5),
        'q_a_norm': gain(ks[11], Q_RANK),
        'w_q_b': nrm(ks[12], (DEPTH, Q_RANK, MLA_HEADS * (QK_NOPE + QK_ROPE)), Q_RANK ** -0.5),
        'kv_a_norm': gain(ks[13], KV_RANK),
        'w_uk': nrm(ks[14], (DEPTH, KV_RANK, MLA_HEADS, QK_NOPE), KV_RANK ** -0.5),
        'w_uv': nrm(ks[15], (DEPTH, KV_RANK, MLA_HEADS, V_HEAD), KV_RANK ** -0.5),
        'conv_w': nrm(ks[16], (DEPTH, CONV_W, CONV_DIM), CONV_W ** -0.5),
        'conv_b': nrm(ks[17], (DEPTH, CONV_DIM), 0.02),
        'dt_bias': dt_bias,
        'a_log': a_log,
        'd_skip': 1.0 + 0.1 * jax.random.normal(ks[18], (DEPTH, SSD_HEADS), f32),
        'ssm_norm': gain(ks[19], D_SSM),
        'w_out': nrm(ks[20], (DEPTH, D_MIX, D_MODEL), D_MIX ** -0.5),
        'norm_post': gain(ks[21], D_MODEL),
    }


def reference(x_prompt, x_sample, cache_ckv, cache_krope, state_conv, state_ssm, page_table,
              norm_pre, w_in, q_a_norm, w_q_b, kv_a_norm, w_uk, w_uv, conv_w, conv_b,
              dt_bias, a_log, d_skip, ssm_norm, w_out, norm_post):
    y_prompt, y_sample = x_prompt, x_sample
    ckv_p, kr_p, conv_p, ssm_p = [], [], [], []
    ckv_s, kr_s, conv_s, ssm_s = [], [], [], []
    for l in range(DEPTH):
        lw = {'norm_pre': norm_pre[l], 'w_in': w_in[l], 'q_a_norm': q_a_norm[l], 'w_q_b': w_q_b[l],
              'kv_a_norm': kv_a_norm[l], 'w_uk': w_uk[l], 'w_uv': w_uv[l], 'conv_w': conv_w[l],
              'conv_b': conv_b[l], 'dt_bias': dt_bias[l], 'a_log': a_log[l], 'd_skip': d_skip[l],
              'ssm_norm': ssm_norm[l], 'w_out': w_out[l], 'norm_post': norm_post[l]}
        y_prompt, c1, k1, v1, h1 = _prompt_layer(y_prompt, lw)
        y_sample, c2, k2, v2, h2 = _sample_layer(y_sample, cache_ckv[l], cache_krope[l], state_conv[l],
                                                 state_ssm[l], page_table, lw)
        ckv_p.append(c1); kr_p.append(k1); conv_p.append(v1); ssm_p.append(h1)
        ckv_s.append(c2); kr_s.append(k2); conv_s.append(v2); ssm_s.append(h2)
    ckv_prompt, krope_prompt = jnp.stack(ckv_p), jnp.stack(kr_p)
    conv_prompt, ssm_prompt = jnp.stack(conv_p), jnp.stack(ssm_p)
    ckv_sample, krope_sample = jnp.stack(ckv_s), jnp.stack(kr_s)
    conv_sample, ssm_sample = jnp.stack(conv_s), jnp.stack(ssm_s)
    return (y_prompt, y_sample, ckv_prompt, krope_prompt, conv_prompt, ssm_prompt,
            ckv_sample, krope_sample, conv_sample, ssm_sample)
```

```python
import functools
import math

import jax
import jax.numpy as jnp
from jax import lax
from jax.experimental import pallas as pl
from jax.experimental.pallas import tpu as pltpu

F32 = jnp.float32
BF16 = jnp.bfloat16

D_MODEL = 1024
PAGE_SIZE = 128
D_MIX = 2 * D_MODEL
D_ATTN = D_MIX // 2
D_SSM = D_MIX - D_ATTN
MLA_HEADS = 8
QK_NOPE = 128
QK_ROPE = 64
ROPE_HALF = QK_ROPE // 2
V_HEAD = D_ATTN // MLA_HEADS
Q_RANK = 384
KV_RANK = 256
ROPE_THETA = 10000.0
SOFTMAX_SCALE = (QK_NOPE + QK_ROPE) ** -0.5
SSD_HEADDIM = 64
SSD_HEADS = D_SSM // SSD_HEADDIM
SSD_GROUPS = 2
HEADS_PER_GROUP = SSD_HEADS // SSD_GROUPS
GROUP_WIDTH = D_SSM // SSD_GROUPS
D_STATE = 128
CONV_W = 4
CONV_DIM = D_SSM + 2 * SSD_GROUPS * D_STATE
CHUNK = 128
EPS = 1e-6
NEG_BIG = -1e30

LANES = 128
SUBLANES = 8
VMEM_LIMIT_BYTES = 56 * 1024 * 1024

COL_Q = 0
COL_C = COL_Q + Q_RANK
COL_ZA = COL_C + KV_RANK
COL_ZS = COL_ZA + D_ATTN
COL_XBC = COL_ZS + D_SSM
COL_MISC = COL_XBC + CONV_DIM
D_IN_PAD = COL_MISC + LANES
DT_LANE0 = QK_ROPE
QK_PAD = 2 * LANES

PROJ_ROWS = 256
ATTN_TQ = 256
MERGE_ROWS = 512
DEC_STATE_SEQS = 8

_NT = (((1,), (1,)), ((), ()))


def _rms(x, w):
    return x * lax.rsqrt(jnp.mean(x * x, axis=-1, keepdims=True) + EPS) * w


def _silu(x):
    return x / (1.0 + jnp.exp(-x))


def _softplus(x):
    return jnp.maximum(x, 0.0) + jnp.log1p(jnp.exp(-jnp.abs(x)))


def _dot(a, b):
    return jnp.dot(a, b, preferred_element_type=F32)


def _dot_exact(a, b):
    return jnp.dot(a, b, preferred_element_type=F32, precision=lax.Precision.HIGHEST)


def _rope_tile(x, cos, sin_signed):
    lane = lax.broadcasted_iota(jnp.int32, x.shape, 1)
    first_half = (lane % QK_ROPE) < ROPE_HALF
    partner = jnp.where(first_half,
                        pltpu.roll(x, LANES - ROPE_HALF, 1),
                        pltpu.roll(x, ROPE_HALF, 1))
    return x * cos + partner * sin_signed


def _front(x_ref, npre_ref, win_ref):
    h = _rms(x_ref[...], npre_ref[...]).astype(BF16)

    def seg(lo, hi):
        return _dot(h, win_ref[:, lo:hi])

    return seg


def _proj_prompt_kernel(x_ref, cos_ref, sin_ref, npre_ref, win_ref, qan_ref, wqb_ref, kvn_ref,
                        wuk_ref, wuv_ref,
                        q_ref, k_ref, v_ref, ckv_ref, kr_ref, za_ref, zs_ref, xbc_ref, misc_ref):
    seg = _front(x_ref, npre_ref, win_ref)
    za_ref[...] = seg(COL_ZA, COL_ZS)
    zs_ref[...] = seg(COL_ZS, COL_XBC)
    xbc_ref[...] = seg(COL_XBC, COL_MISC)
    misc = seg(COL_MISC, D_IN_PAD)
    misc_ref[...] = misc
    cos = cos_ref[...]
    sin = sin_ref[...]
    lane = lax.broadcasted_iota(jnp.int32, misc.shape, 1)
    low = lane < QK_ROPE

    kr_full = _rope_tile(misc, cos, sin)
    kr_ref[...] = kr_full[:, :QK_ROPE]
    kr_lo = jnp.where(low, kr_full, 0.0).astype(BF16)
    kr_hi = jnp.where(low, 0.0, pltpu.roll(kr_full, QK_ROPE, 1)).astype(BF16)

    ckv = _rms(seg(COL_C, COL_ZA), kvn_ref[...])
    ckv_ref[...] = ckv
    cb = ckv.astype(BF16)
    knope = _dot(cb, wuk_ref[...])
    v_ref[...] = _dot(cb, wuv_ref[...]).astype(BF16)

    qn = _rms(seg(COL_Q, COL_C), qan_ref[...]).astype(BF16)
    q = _dot(qn, wqb_ref[...])
    nope_w = MLA_HEADS * QK_NOPE
    for hh in range(MLA_HEADS):
        pair = hh // 2
        r = _rope_tile(q[:, nope_w + pair * LANES: nope_w + (pair + 1) * LANES], cos, sin)
        own = low if hh % 2 == 0 else jnp.logical_not(low)
        base = hh * QK_PAD
        q_ref[:, base:base + LANES] = (q[:, hh * QK_NOPE:(hh + 1) * QK_NOPE] * SOFTMAX_SCALE).astype(BF16)
        q_ref[:, base + LANES:base + QK_PAD] = (jnp.where(own, r, 0.0) * SOFTMAX_SCALE).astype(BF16)
        k_ref[:, base:base + LANES] = knope[:, hh * QK_NOPE:(hh + 1) * QK_NOPE].astype(BF16)
        k_ref[:, base + LANES:base + QK_PAD] = kr_lo if hh % 2 == 0 else kr_hi


def _proj_sample_kernel(x_ref, cos_ref, sin_ref, npre_ref, win_ref, qan_ref, wqb_ref, kvn_ref, wukt_ref,
                        qlat_ref, qrope_ref, ckv_ref, kr_ref, za_ref, zs_ref, xbc_ref, misc_ref):
    seg = _front(x_ref, npre_ref, win_ref)
    za_ref[...] = seg(COL_ZA, COL_ZS)
    zs_ref[...] = seg(COL_ZS, COL_XBC)
    xbc_ref[...] = seg(COL_XBC, COL_MISC)
    misc = seg(COL_MISC, D_IN_PAD)
    misc_ref[...] = misc
    cos = cos_ref[...]
    sin = sin_ref[...]
    kr_ref[...] = _rope_tile(misc, cos, sin)[:, :QK_ROPE]
    ckv_ref[...] = _rms(seg(COL_C, COL_ZA), kvn_ref[...])

    qn = _rms(seg(COL_Q, COL_C), qan_ref[...]).astype(BF16)
    q = _dot(qn, wqb_ref[...])
    nope_w = MLA_HEADS * QK_NOPE
    for pair in range(MLA_HEADS // 2):
        lo = nope_w + pair * LANES
        qrope_ref[:, pair * LANES:(pair + 1) * LANES] = _rope_tile(q[:, lo:lo + LANES], cos, sin) * SOFTMAX_SCALE
    for hh in range(MLA_HEADS):
        qh = q[:, hh * QK_NOPE:(hh + 1) * QK_NOPE].astype(BF16)
        qlat_ref[:, hh * KV_RANK:(hh + 1) * KV_RANK] = _dot(qh, wukt_ref[hh]) * SOFTMAX_SCALE


def _full(shape):
    return pl.BlockSpec(shape, lambda *_: (0,) * len(shape))


def _proj_prompt(x2, cos, sin, w):
    n = x2.shape[0]
    tm = PROJ_ROWS
    seq_tiles = cos.shape[0] // tm
    rows = lambda width: pl.BlockSpec((tm, width), lambda i: (i, 0))
    tab = pl.BlockSpec((tm, LANES), lambda i: (i % seq_tiles, 0))
    out_shape = (
        jax.ShapeDtypeStruct((n, MLA_HEADS * QK_PAD), BF16),
        jax.ShapeDtypeStruct((n, MLA_HEADS * QK_PAD), BF16),
        jax.ShapeDtypeStruct((n, D_ATTN), BF16),
        jax.ShapeDtypeStruct((n, KV_RANK), F32),
        jax.ShapeDtypeStruct((n, QK_ROPE), F32),
        jax.ShapeDtypeStruct((n, D_ATTN), F32),
        jax.ShapeDtypeStruct((n, D_SSM), F32),
        jax.ShapeDtypeStruct((n, CONV_DIM), F32),
        jax.ShapeDtypeStruct((n, LANES), F32),
    )
    return pl.pallas_call(
        _proj_prompt_kernel,
        grid=(n // tm,),
        in_specs=[rows(D_MODEL), tab, tab, _full((1, D_MODEL)), _full((D_MODEL, D_IN_PAD)),
                  _full((1, Q_RANK)), _full(w['w_qb'].shape), _full((1, KV_RANK)),
                  _full(w['w_uk2'].shape), _full(w['w_uv2'].shape)],
        out_specs=tuple(rows(s.shape[1]) for s in out_shape),
        out_shape=out_shape,
        compiler_params=pltpu.CompilerParams(dimension_semantics=("arbitrary",),
                                             vmem_limit_bytes=VMEM_LIMIT_BYTES),
        name="proj_prompt",
    )(x2, cos, sin, w['norm_pre'], w['w_in'], w['q_a_norm'], w['w_qb'], w['kv_a_norm'],
      w['w_uk2'], w['w_uv2'])


def _proj_sample(x2, cos, sin, w):
    n = x2.shape[0]
    rows = lambda width: pl.BlockSpec((n, width), lambda i: (0, 0))
    out_shape = (
        jax.ShapeDtypeStruct((n, MLA_HEADS * KV_RANK), F32),
        jax.ShapeDtypeStruct((n, MLA_HEADS * QK_ROPE), F32),
        jax.ShapeDtypeStruct((n, KV_RANK), F32),
        jax.ShapeDtypeStruct((n, QK_ROPE), F32),
        jax.ShapeDtypeStruct((n, D_ATTN), F32),
        jax.ShapeDtypeStruct((n, D_SSM), F32),
        jax.ShapeDtypeStruct((n, CONV_DIM), F32),
        jax.ShapeDtypeStruct((n, LANES), F32),
    )
    return pl.pallas_call(
        _proj_sample_kernel,
        grid=(1,),
        in_specs=[rows(D_MODEL), rows(LANES), rows(LANES), _full((1, D_MODEL)), _full((D_MODEL, D_IN_PAD)),
                  _full((1, Q_RANK)), _full(w['w_qb'].shape), _full((1, KV_RANK)),
                  _full(w['w_ukt'].shape)],
        out_specs=tuple(rows(s.shape[1]) for s in out_shape),
        out_shape=out_shape,
        compiler_params=pltpu.CompilerParams(dimension_semantics=("arbitrary",),
                                             vmem_limit_bytes=VMEM_LIMIT_BYTES),
        name="proj_sample",
    )(x2, cos, sin, w['norm_pre'], w['w_in'], w['q_a_norm'], w['w_qb'], w['kv_a_norm'], w['w_ukt'])


def _attn_prompt_kernel(q_ref, k_ref, v_ref, z_ref, o_ref):
    seq = q_ref.shape[0]
    tq = ATTN_TQ
    row = lax.broadcasted_iota(jnp.int32, (tq, tq), 0)
    col = lax.broadcasted_iota(jnp.int32, (tq, tq), 1)
    causal = col <= row
    for qi in range(seq // tq):
        lim = (qi + 1) * tq
        q = q_ref[qi * tq:lim, :]
        s = lax.dot_general(q, k_ref[0:lim, :], _NT, preferred_element_type=F32)
        diag = jnp.where(causal, s[:, lim - tq:], NEG_BIG)
        s = diag if qi == 0 else jnp.concatenate([s[:, :lim - tq], diag], axis=1)
        m = jnp.max(s, axis=-1, keepdims=True)
        p = jnp.exp(s - m)
        l = jnp.sum(p, axis=-1, keepdims=True)
        o = _dot(p.astype(BF16), v_ref[0:lim, :]) / l
        o_ref[qi * tq:lim, :] = (o * _silu(z_ref[qi * tq:lim, :])).astype(BF16)


def _attn_prompt(q3, k3, v3, z3):
    b, s, _ = q3.shape
    qk = pl.BlockSpec((None, s, QK_PAD), lambda i, j: (i, 0, j))
    hv = pl.BlockSpec((None, s, V_HEAD), lambda i, j: (i, 0, j))
    return pl.pallas_call(
        _attn_prompt_kernel,
        grid=(b, MLA_HEADS),
        in_specs=[qk, qk, hv, hv],
        out_specs=hv,
        out_shape=jax.ShapeDtypeStruct((b, s, D_ATTN), BF16),
        compiler_params=pltpu.CompilerParams(dimension_semantics=("arbitrary", "arbitrary"),
                                             vmem_limit_bytes=VMEM_LIMIT_BYTES),
        name="attn_prompt",
    )(q3, k3, v3, z3)


def _dt_and_da(misc, dtb, alog):
    lane = lax.broadcasted_iota(jnp.int32, misc.shape, 1)
    is_dt = jnp.logical_and(lane >= DT_LANE0, lane < DT_LANE0 + SSD_HEADS)
    dt = jnp.where(is_dt, _softplus(misc + dtb), 0.0)
    return dt, dt * (-jnp.exp(alog))


def _ssd_finish(y, xs, z, dskip, normw):
    y = y + dskip * xs
    gated = y * _silu(z)
    outs = []
    for g in range(SSD_GROUPS):
        sl = slice(g * GROUP_WIDTH, (g + 1) * GROUP_WIDTH)
        outs.append(_rms(gated[:, sl], normw[:, sl]))
    return jnp.concatenate(outs, axis=1).astype(BF16)


def _out_proj(mix_a, mix_s, wout_ref, npost, x):
    o = _dot(mix_a, wout_ref[0:D_ATTN, :]) + _dot(mix_s, wout_ref[D_ATTN:D_MIX, :])
    return x + _rms(o, npost)


CONV_PAD = SUBLANES


def _ssd_prompt_kernel(xbc_ref, misc_ref, z_ref, cw_ref, cb_ref, dtb_ref, alog_ref, expand_ref,
                       dskip_ref, normw_ref, mix_ref, state_ref, xp_ref):
    c = pl.program_id(1)

    @pl.when(c == 0)
    def _():
        xp_ref[0:CONV_PAD, :] = jnp.zeros((CONV_PAD, CONV_DIM), F32)
        state_ref[...] = jnp.zeros(state_ref.shape, F32)

    xb = xbc_ref[...]
    xp_ref[CONV_PAD:CONV_PAD + CHUNK, :] = xb
    acc = cb_ref[...] + xp_ref[pl.ds(CONV_PAD - 3, CHUNK), :] * cw_ref[0:1, :]
    acc = acc + xp_ref[pl.ds(CONV_PAD - 2, CHUNK), :] * cw_ref[1:2, :]
    acc = acc + xp_ref[pl.ds(CONV_PAD - 1, CHUNK), :] * cw_ref[2:3, :]
    acc = acc + xb * cw_ref[3:4, :]
    act = _silu(acc)
    xp_ref[CONV_PAD - 3:CONV_PAD, :] = xp_ref[CONV_PAD + CHUNK - 3:CONV_PAD + CHUNK, :]

    xs = act[:, :D_SSM]
    dt, da = _dt_and_da(misc_ref[...], dtb_ref[...], alog_ref[...])
    row = lax.broadcasted_iota(jnp.int32, (CHUNK, CHUNK), 0)
    col = lax.broadcasted_iota(jnp.int32, (CHUNK, CHUNK), 1)
    lower = row >= col
    a_cum = _dot_exact(lower.astype(F32), da)
    a_cum_t = a_cum.T
    expand = expand_ref[...]
    dt_w = _dot_exact(dt, expand)
    a_cum_w = _dot_exact(a_cum, expand)
    xdt = xs * dt_w
    decay_out = jnp.exp(a_cum_w)
    a_last_w = a_cum_w[CHUNK - 1:CHUNK, :]
    xdt_end = (xdt * jnp.exp(a_last_w - a_cum_w)).astype(BF16)
    xdt_b = xdt.astype(BF16)
    lane_w = lax.broadcasted_iota(jnp.int32, (CHUNK, LANES), 1)
    first_head = lane_w < SSD_HEADDIM

    ys = []
    for g in range(SSD_GROUPS):
        bm = act[:, D_SSM + g * D_STATE:D_SSM + (g + 1) * D_STATE]
        cm = act[:, D_SSM + (SSD_GROUPS + g) * D_STATE:D_SSM + (SSD_GROUPS + g + 1) * D_STATE]
        bm_b = bm.astype(BF16)
        cm_b = cm.astype(BF16)
        cb = lax.dot_general(cm_b, bm_b, _NT, preferred_element_type=F32)
        gsl = slice(g * GROUP_WIDTH, (g + 1) * GROUP_WIDTH)
        state = state_ref[g]
        y_off = _dot(cm_b, state.astype(BF16)) * decay_out[:, gsl]
        y_diag = []
        for pair in range(HEADS_PER_GROUP // 2):
            halves = []
            x_pair = xdt_b[:, g * GROUP_WIDTH + pair * LANES: g * GROUP_WIDTH + (pair + 1) * LANES]
            for k in range(2):
                lane_h = DT_LANE0 + g * HEADS_PER_GROUP + 2 * pair + k
                seg = a_cum[:, lane_h:lane_h + 1] - a_cum_t[lane_h:lane_h + 1, :]
                decay = jnp.exp(jnp.where(lower, seg, NEG_BIG))
                halves.append(_dot((cb * decay).astype(BF16), x_pair))
            y_diag.append(jnp.where(first_head, halves[0], halves[1]))
        ys.append(jnp.concatenate(y_diag, axis=1) + y_off)
        state_ref[g] = state * jnp.exp(a_last_w[:, gsl]) + _dot(bm.T.astype(BF16), xdt_end[:, gsl])

    y = jnp.concatenate(ys, axis=1)
    mix_ref[...] = _ssd_finish(y, xs, z_ref[...], dskip_ref[...], normw_ref[...])


def _ssd_prompt(xbc3, misc3, z3, w):
    b, s, _ = xbc3.shape
    nc = s // CHUNK
    blk = lambda width: pl.BlockSpec((None, CHUNK, width), lambda i, j: (i, j, 0))
    return pl.pallas_call(
        _ssd_prompt_kernel,
        grid=(b, nc),
        in_specs=[blk(CONV_DIM), blk(LANES), blk(D_SSM), _full((CONV_W, CONV_DIM)), _full((1, CONV_DIM)),
                  _full((1, LANES)), _full((1, LANES)), _full((LANES, D_SSM)),
                  _full((1, D_SSM)), _full((1, D_SSM))],
        out_specs=(blk(D_SSM),
                   pl.BlockSpec((None, SSD_GROUPS, D_STATE, GROUP_WIDTH), lambda i, j: (i, 0, 0, 0))),
        out_shape=(jax.ShapeDtypeStruct((b, s, D_SSM), BF16),
                   jax.ShapeDtypeStruct((b, SSD_GROUPS, D_STATE, GROUP_WIDTH), F32)),
        scratch_shapes=[pltpu.VMEM((CONV_PAD + CHUNK, CONV_DIM), F32)],
        compiler_params=pltpu.CompilerParams(dimension_semantics=("arbitrary", "arbitrary"),
                                             vmem_limit_bytes=VMEM_LIMIT_BYTES),
        name="ssd_prompt",
    )(xbc3, misc3, z3, w['conv_w'], w['conv_b'], w['dt_bias_t'], w['a_log_t'], w['expand'],
      w['d_skip_w'], w['ssm_norm'])


def _merge_kernel(ma_ref, ms_ref, x_ref, wout_ref, npost_ref, y_ref):
    y_ref[...] = _out_proj(ma_ref[...], ms_ref[...], wout_ref, npost_ref[...], x_ref[...])


def _merge(mix_a, mix_s, x2, w):
    n = x2.shape[0]
    tm = min(MERGE_ROWS, n)
    rows = lambda width: pl.BlockSpec((tm, width), lambda i: (i, 0))
    return pl.pallas_call(
        _merge_kernel,
        grid=(n // tm,),
        in_specs=[rows(D_ATTN), rows(D_SSM), rows(D_MODEL), _full((D_MIX, D_MODEL)), _full((1, D_MODEL))],
        out_specs=rows(D_MODEL),
        out_shape=jax.ShapeDtypeStruct((n, D_MODEL), F32),
        compiler_params=pltpu.CompilerParams(dimension_semantics=("arbitrary",),
                                             vmem_limit_bytes=VMEM_LIMIT_BYTES),
        name="merge",
    )(mix_a, mix_s, x2, w['w_out'], w['norm_post'])


def _attn_sample_kernel(layer, pt_ref, qlat_ref, qrope_ref, cnew_ref, rnew_ref, cache_c_ref, cache_r_ref,
                        o_ref, cbuf, rbuf, sem):
    i = pl.program_id(0)
    n = pl.num_programs(0)
    n_pages = pt_ref.shape[1]
    past = n_pages * PAGE_SIZE
    slot = i % 2

    def page_copies(seq, slot_):
        copies = []
        for p in range(n_pages):
            page = pt_ref[seq, p]
            dst = pl.ds(p * PAGE_SIZE, PAGE_SIZE)
            copies.append(pltpu.make_async_copy(cache_c_ref.at[layer, page], cbuf.at[slot_, dst], sem.at[0, slot_]))
            copies.append(pltpu.make_async_copy(cache_r_ref.at[layer, page], rbuf.at[slot_, dst], sem.at[1, slot_]))
        return copies

    @pl.when(i == 0)
    def _():
        for cp in page_copies(0, 0):
            cp.start()

    @pl.when(i + 1 < n)
    def _():
        for cp in page_copies(i + 1, 1 - slot):
            cp.start()

    for cp in page_copies(i, slot):
        cp.wait()

    qlat = qlat_ref[...]
    qrope = qrope_ref[...]
    kc = cbuf[slot].astype(BF16)
    kr = rbuf[slot].astype(BF16)
    s = (lax.dot_general(qlat.astype(BF16), kc, _NT, preferred_element_type=F32)
         + lax.dot_general(qrope.astype(BF16), kr, _NT, preferred_element_type=F32))
    q_pos = past
    k_pos = lax.broadcasted_iota(jnp.int32, s.shape, 1)
    s = jnp.where(k_pos <= q_pos, s, NEG_BIG)
    cnew = cnew_ref[...]
    rnew = rnew_ref[...]
    s_new = (jnp.sum(qlat * cnew, axis=-1, keepdims=True)
             + jnp.sum(qrope * rnew, axis=-1, keepdims=True))
    m = jnp.maximum(jnp.max(s, axis=-1, keepdims=True), s_new)
    p = jnp.exp(s - m)
    p_new = jnp.exp(s_new - m)
    l = jnp.sum(p, axis=-1, keepdims=True) + p_new
    o_ref[...] = (_dot(p.astype(BF16), kc) + p_new * cnew) / l


def _attn_sample(layer, page_table, qlat3, qrope3, cnew3, rnew3, cache_c, cache_r):
    b, n_pages = page_table.shape
    past = n_pages * PAGE_SIZE
    per_seq = lambda d1, d2: pl.BlockSpec((None, d1, d2), lambda i, pt: (i, 0, 0))
    grid_spec = pltpu.PrefetchScalarGridSpec(
        num_scalar_prefetch=1,
        grid=(b,),
        in_specs=[per_seq(MLA_HEADS, KV_RANK), per_seq(MLA_HEADS, QK_ROPE), per_seq(1, KV_RANK),
                  per_seq(1, QK_ROPE), pl.BlockSpec(memory_space=pl.ANY), pl.BlockSpec(memory_space=pl.ANY)],
        out_specs=per_seq(MLA_HEADS, KV_RANK),
        scratch_shapes=[pltpu.VMEM((2, past, KV_RANK), F32), pltpu.VMEM((2, past, QK_ROPE), F32),
                        pltpu.SemaphoreType.DMA((2, 2))],
    )
    return pl.pallas_call(
        functools.partial(_attn_sample_kernel, layer),
        grid_spec=grid_spec,
        out_shape=jax.ShapeDtypeStruct((b, MLA_HEADS, KV_RANK), F32),
        compiler_params=pltpu.CompilerParams(dimension_semantics=("arbitrary",),
                                             vmem_limit_bytes=VMEM_LIMIT_BYTES),
        name="attn_sample",
    )(page_table, qlat3, qrope3, cnew3, rnew3, cache_c, cache_r)


def _post_sample_kernel(olat_ref, wuv_ref, za_ref, xbc_ref, misc_ref, cprev_ref, cw_ref, cb_ref,
                        dtb_ref, alog_ref, expand_ref,
                        mixa_ref, cnew_ref, act_ref, xdt_t_ref, dec_t_ref):
    for hh in range(MLA_HEADS):
        o = _dot(olat_ref[:, hh * KV_RANK:(hh + 1) * KV_RANK].astype(BF16), wuv_ref[hh])
        vs = slice(hh * V_HEAD, (hh + 1) * V_HEAD)
        mixa_ref[:, vs] = (o * _silu(za_ref[:, vs])).astype(BF16)

    xb = xbc_ref[...]
    acc = cb_ref[...]
    for k in range(CONV_W - 1):
        acc = acc + cprev_ref[:, k * CONV_DIM:(k + 1) * CONV_DIM] * cw_ref[k:k + 1, :]
    acc = acc + xb * cw_ref[CONV_W - 1:CONV_W, :]
    act = _silu(acc)
    act_ref[...] = act
    cnew_ref[:, 0:(CONV_W - 2) * CONV_DIM] = cprev_ref[:, CONV_DIM:(CONV_W - 1) * CONV_DIM]
    cnew_ref[:, (CONV_W - 2) * CONV_DIM:(CONV_W - 1) * CONV_DIM] = xb

    dt, da = _dt_and_da(misc_ref[...], dtb_ref[...], alog_ref[...])
    expand = expand_ref[...]
    xdt = act[:, :D_SSM] * _dot_exact(dt, expand)
    decay = _dot_exact(jnp.exp(da), expand)
    xdt_t_ref[...] = xdt.T
    dec_t_ref[...] = decay.T


def _post_sample(olat2, za, xbc, misc, cprev2, w):
    n = olat2.shape[0]
    full = lambda a: _full(a.shape)
    args = (olat2, w['w_uvh'], za, xbc, misc, cprev2, w['conv_w'], w['conv_b'], w['dt_bias_t'], w['a_log_t'],
            w['expand'])
    out_shape = (
        jax.ShapeDtypeStruct((n, D_ATTN), BF16),
        jax.ShapeDtypeStruct((n, (CONV_W - 1) * CONV_DIM), F32),
        jax.ShapeDtypeStruct((n, CONV_DIM), F32),
        jax.ShapeDtypeStruct((D_SSM, n), F32),
        jax.ShapeDtypeStruct((D_SSM, n), F32),
    )
    return pl.pallas_call(
        _post_sample_kernel,
        grid=(1,),
        in_specs=[full(a) for a in args],
        out_specs=tuple(_full(s.shape) for s in out_shape),
        out_shape=out_shape,
        compiler_params=pltpu.CompilerParams(dimension_semantics=("arbitrary",),
                                             vmem_limit_bytes=VMEM_LIMIT_BYTES),
        name="post_sample",
    )(*args)


def _state_sample_kernel(h0_ref, xdt_t_ref, dec_t_ref, bm_ref, cm_ref, h_ref, y_t_ref):
    t = pl.program_id(0)
    tb = h0_ref.shape[0]
    rows, n_seq = xdt_t_ref.shape

    @pl.when(t == 0)
    def _():
        y_t_ref[...] = jnp.zeros(y_t_ref.shape, F32)

    lane = lax.broadcasted_iota(jnp.int32, (rows, n_seq), 1)
    xdt_t = xdt_t_ref[...]
    dec_t = dec_t_ref[...]
    for j in range(tb):
        own = lane == t * tb + j
        x_col = jnp.sum(jnp.where(own, xdt_t, 0.0), axis=1, keepdims=True)
        d_col = jnp.sum(jnp.where(own, dec_t, 0.0), axis=1, keepdims=True)
        b_rows = jnp.concatenate(
            [jnp.broadcast_to(bm_ref[j:j + 1, g * D_STATE:(g + 1) * D_STATE], (GROUP_WIDTH, D_STATE))
             for g in range(SSD_GROUPS)], axis=0)
        c_rows = jnp.concatenate(
            [jnp.broadcast_to(cm_ref[j:j + 1, g * D_STATE:(g + 1) * D_STATE], (GROUP_WIDTH, D_STATE))
             for g in range(SSD_GROUPS)], axis=0)
        h = d_col * h0_ref[j] + x_col * b_rows
        h_ref[j] = h
        y_col = jnp.sum(h * c_rows, axis=1, keepdims=True)
        y_t_ref[...] = jnp.where(own, y_col, y_t_ref[...])


def _state_sample(h0, xdt_t, dec_t, act):
    n = h0.shape[0]
    tb = DEC_STATE_SEQS
    bc_w = SSD_GROUPS * D_STATE
    st = pl.BlockSpec((tb, D_SSM, D_STATE), lambda i: (i, 0, 0))
    return pl.pallas_call(
        _state_sample_kernel,
        grid=(n // tb,),
        in_specs=[st, _full((D_SSM, n)), _full((D_SSM, n)),
                  pl.BlockSpec((tb, bc_w), lambda i: (i, D_SSM // bc_w)),
                  pl.BlockSpec((tb, bc_w), lambda i: (i, D_SSM // bc_w + 1))],
        out_specs=(st, _full((D_SSM, n))),
        out_shape=(jax.ShapeDtypeStruct(h0.shape, F32), jax.ShapeDtypeStruct((D_SSM, n), F32)),
        compiler_params=pltpu.CompilerParams(dimension_semantics=("arbitrary",),
                                             vmem_limit_bytes=VMEM_LIMIT_BYTES),
        name="state_sample",
    )(h0, xdt_t, dec_t, act, act)


def _finish_sample_kernel(y_t_ref, act_ref, z_ref, dskip_ref, normw_ref, ma_ref, x_ref, wout_ref, npost_ref,
                          y_ref):
    mix_s = _ssd_finish(y_t_ref[...].T, act_ref[...], z_ref[...], dskip_ref[...], normw_ref[...])
    y_ref[...] = _out_proj(ma_ref[...], mix_s, wout_ref, npost_ref[...], x_ref[...])


def _finish_sample(y_t, act, zs, mix_a, x2, w):
    n = x2.shape[0]
    return pl.pallas_call(
        _finish_sample_kernel,
        grid=(1,),
        in_specs=[_full(y_t.shape), pl.BlockSpec((n, D_SSM), lambda i: (0, 0)), _full(zs.shape),
                  _full((1, D_SSM)), _full((1, D_SSM)), _full(mix_a.shape), _full(x2.shape),
                  _full((D_MIX, D_MODEL)), _full((1, D_MODEL))],
        out_specs=_full((n, D_MODEL)),
        out_shape=jax.ShapeDtypeStruct((n, D_MODEL), F32),
        compiler_params=pltpu.CompilerParams(dimension_semantics=("arbitrary",),
                                             vmem_limit_bytes=VMEM_LIMIT_BYTES),
        name="finish_sample",
    )(y_t, act, zs, w['d_skip_w'], w['ssm_norm'], mix_a, x2, w['w_out'], w['norm_post'])


def _rope_tables(pos):
    inv_freq = ROPE_THETA ** (-jnp.arange(ROPE_HALF, dtype=F32) / ROPE_HALF)
    ang = pos.astype(F32)[:, None] * inv_freq[None, :]
    cos, sin = jnp.cos(ang), jnp.sin(ang)
    reps = LANES // QK_ROPE
    return jnp.tile(cos, (1, 2 * reps)), jnp.tile(jnp.concatenate([-sin, sin], axis=1), (1, reps))


def _prep_weights(lw):
    row = lambda v: v.reshape(1, -1).astype(F32)
    w_in = lw['w_in']
    q_a, c_raw, kr, za, zs, xbc, dt = jnp.split(
        w_in, (Q_RANK, Q_RANK + KV_RANK, Q_RANK + KV_RANK + QK_ROPE,
               Q_RANK + KV_RANK + QK_ROPE + D_ATTN, Q_RANK + KV_RANK + QK_ROPE + D_ATTN + D_SSM,
               Q_RANK + KV_RANK + QK_ROPE + D_ATTN + D_SSM + CONV_DIM), axis=1)
    pad = jnp.zeros((D_MODEL, LANES - QK_ROPE - SSD_HEADS), w_in.dtype)
    w_qb = lw['w_q_b'].reshape(Q_RANK, MLA_HEADS, QK_NOPE + QK_ROPE)
    lane_pad = lambda v: jnp.pad(v.reshape(1, -1).astype(F32), ((0, 0), (DT_LANE0, LANES - DT_LANE0 - SSD_HEADS)))
    head_of_col = jnp.arange(D_SSM) // SSD_HEADDIM
    return {
        'norm_pre': row(lw['norm_pre']),
        'w_in': jnp.concatenate([q_a, c_raw, za, zs, xbc, kr, dt, pad], axis=1).astype(BF16),
        'q_a_norm': row(lw['q_a_norm']),
        'w_qb': jnp.concatenate([w_qb[:, :, :QK_NOPE].reshape(Q_RANK, -1),
                                 w_qb[:, :, QK_NOPE:].reshape(Q_RANK, -1)], axis=1).astype(BF16),
        'kv_a_norm': row(lw['kv_a_norm']),
        'w_uk2': lw['w_uk'].reshape(KV_RANK, MLA_HEADS * QK_NOPE).astype(BF16),
        'w_uv2': lw['w_uv'].reshape(KV_RANK, MLA_HEADS * V_HEAD).astype(BF16),
        'w_ukt': jnp.transpose(lw['w_uk'], (1, 2, 0)).astype(BF16),
        'w_uvh': jnp.transpose(lw['w_uv'], (1, 0, 2)).astype(BF16),
        'conv_w': lw['conv_w'].astype(F32),
        'conv_b': row(lw['conv_b']),
        'dt_bias_t': lane_pad(lw['dt_bias']),
        'a_log_t': lane_pad(lw['a_log']),
        'expand': (jnp.arange(LANES)[:, None] == DT_LANE0 + head_of_col[None, :]).astype(F32),
        'd_skip_w': jnp.repeat(lw['d_skip'].astype(F32), SSD_HEADDIM).reshape(1, D_SSM),
        'ssm_norm': row(lw['ssm_norm']),
        'w_out': lw['w_out'].astype(BF16),
        'norm_post': row(lw['norm_post']),
    }


def _prompt_layer(x, w):
    b, s, _ = x.shape
    x2 = x.reshape(b * s, D_MODEL)
    cos, sin = _rope_tables(jnp.arange(s, dtype=jnp.int32))
    q, k, v, ckv, kr, za, zs, xbc, misc = _proj_prompt(x2, cos, sin, w)
    r3 = lambda a: a.reshape(b, s, a.shape[-1])
    mix_a = _attn_prompt(r3(q), r3(k), r3(v), r3(za))
    xbc3 = r3(xbc)
    mix_s, state = _ssd_prompt(xbc3, r3(misc), r3(zs), w)
    y = _merge(mix_a.reshape(b * s, D_ATTN), mix_s.reshape(b * s, D_SSM), x2, w)
    h = state.reshape(b, SSD_GROUPS, D_STATE, HEADS_PER_GROUP, SSD_HEADDIM)
    h = jnp.transpose(h, (0, 1, 3, 4, 2)).reshape(b, SSD_HEADS, SSD_HEADDIM, D_STATE)
    return (y.reshape(b, s, D_MODEL), r3(ckv), r3(kr), xbc3[:, s - (CONV_W - 1):, :], h)


def _sample_layer(layer, x, cache_c, cache_r, conv_prev, h0, page_table, w):
    b, s, _ = x.shape
    n = b * s
    past = page_table.shape[1] * PAGE_SIZE
    x2 = x.reshape(n, D_MODEL)
    pos = past + jnp.arange(s, dtype=jnp.int32)
    cos, sin = _rope_tables(jnp.tile(pos, b))
    qlat, qrope, ckv, kr, za, zs, xbc, misc = _proj_sample(x2, cos, sin, w)
    olat = _attn_sample(layer, page_table, qlat.reshape(n, MLA_HEADS, KV_RANK), qrope.reshape(n, MLA_HEADS, QK_ROPE),
                        ckv.reshape(n, 1, KV_RANK), kr.reshape(n, 1, QK_ROPE), cache_c, cache_r)
    mix_a, conv_new, act, xdt_t, dec_t = _post_sample(
        olat.reshape(n, MLA_HEADS * KV_RANK), za, xbc, misc, conv_prev.reshape(n, (CONV_W - 1) * CONV_DIM), w)
    h, y_t = _state_sample(h0.astype(F32).reshape(n, D_SSM, D_STATE), xdt_t, dec_t, act)
    y = _finish_sample(y_t, act, zs, mix_a, x2, w)
    return (y.reshape(b, s, D_MODEL), ckv.reshape(b, s, KV_RANK), kr.reshape(b, s, QK_ROPE),
            conv_new.reshape(b, CONV_W - 1, CONV_DIM), h.reshape(b, SSD_HEADS, SSD_HEADDIM, D_STATE))


def kernel(x_prompt, x_sample, cache_ckv, cache_krope, state_conv, state_ssm, page_table, norm_pre, w_in,
           q_a_norm, w_q_b, kv_a_norm, w_uk, w_uv, conv_w, conv_b, dt_bias, a_log, d_skip, ssm_norm, w_out,
           norm_post):
    assert x_sample.shape[1] == 1, "the sample path handles one new token per sequence"
    depth = w_in.shape[0]
    y_prompt, y_sample = x_prompt, x_sample
    outs = [[] for _ in range(8)]
    for l in range(depth):
        w = _prep_weights({'norm_pre': norm_pre[l], 'w_in': w_in[l], 'q_a_norm': q_a_norm[l],
                           'w_q_b': w_q_b[l], 'kv_a_norm': kv_a_norm[l], 'w_uk': w_uk[l], 'w_uv': w_uv[l],
                           'conv_w': conv_w[l], 'conv_b': conv_b[l], 'dt_bias': dt_bias[l], 'a_log': a_log[l],
                           'd_skip': d_skip[l], 'ssm_norm': ssm_norm[l], 'w_out': w_out[l],
                           'norm_post': norm_post[l]})
        y_prompt, c1, k1, v1, h1 = _prompt_layer(y_prompt, w)
        y_sample, c2, k2, v2, h2 = _sample_layer(l, y_sample, cache_ckv, cache_krope, state_conv[l],
                                                 state_ssm[l], page_table, w)
        for lst, val in zip(outs, (c1, k1, v1, h1, c2, k2, v2, h2)):
            lst.append(val)
    return (y_prompt, y_sample) + tuple(jnp.stack(o) for o in outs)
```

```python
import functools
import math

import jax
import jax.numpy as jnp
from jax import lax
from jax.experimental import pallas as pl
from jax.experimental.pallas import tpu as pltpu

F32 = jnp.float32
BF16 = jnp.bfloat16

D_MODEL = 1024
PAGE_SIZE = 128
D_MIX = 2 * D_MODEL
D_ATTN = D_MIX // 2
D_SSM = D_MIX - D_ATTN
MLA_HEADS = 8
QK_NOPE = 128
QK_ROPE = 64
ROPE_HALF = QK_ROPE // 2
V_HEAD = D_ATTN // MLA_HEADS
Q_RANK = 384
KV_RANK = 256
ROPE_THETA = 10000.0
SOFTMAX_SCALE = (QK_NOPE + QK_ROPE) ** -0.5
SSD_HEADDIM = 64
SSD_HEADS = D_SSM // SSD_HEADDIM
SSD_GROUPS = 2
HEADS_PER_GROUP = SSD_HEADS // SSD_GROUPS
GROUP_WIDTH = D_SSM // SSD_GROUPS
D_STATE = 128
CONV_W = 4
CONV_DIM = D_SSM + 2 * SSD_GROUPS * D_STATE
CHUNK = 128
EPS = 1e-6
SPLIT_PIECES = 3
NEG_BIG = -1e30

LANES = 128
SUBLANES = 8
VMEM_LIMIT_BYTES = 56 * 1024 * 1024

COL_Q = 0
COL_C = COL_Q + Q_RANK
COL_ZA = COL_C + KV_RANK
COL_ZS = COL_ZA + D_ATTN
COL_XBC = COL_ZS + D_SSM
COL_MISC = COL_XBC + CONV_DIM
D_IN_PAD = COL_MISC + LANES
DT_LANE0 = QK_ROPE
QK_PAD = 2 * LANES

PROJ_ROWS = 256
ATTN_TQ = 256
MERGE_ROWS = 512
DEC_STATE_SEQS = 8
DEC_KV_CHUNK = 2048
DEC_ATTN_SEQS = 2

_NT = (((1,), (1,)), ((), ()))


def _rms(x, w):
    return x * lax.rsqrt(jnp.mean(x * x, axis=-1, keepdims=True) + EPS) * w


def _silu(x):
    return x / (1.0 + jnp.exp(-x))


def _softplus(x):
    return jnp.maximum(x, 0.0) + jnp.log1p(jnp.exp(-jnp.abs(x)))


def _dot(a, b):
    return jnp.dot(a, b, preferred_element_type=F32)


def _rope_tile(x, cos, sin_signed):
    lane = lax.broadcasted_iota(jnp.int32, x.shape, 1)
    first_half = (lane % QK_ROPE) < ROPE_HALF
    partner = jnp.where(first_half,
                        pltpu.roll(x, LANES - ROPE_HALF, 1),
                        pltpu.roll(x, ROPE_HALF, 1))
    return x * cos + partner * sin_signed


def _front(x_ref, npre_ref, win_ref):
    h = _rms(x_ref[...], npre_ref[...]).astype(BF16)

    def seg(lo, hi):
        return _dot(h, win_ref[:, lo:hi])

    return seg


def _proj_prompt_kernel(x_ref, cos_ref, sin_ref, npre_ref, win_ref, qan_ref, wqb_ref, kvn_ref,
                        wuk_ref, wuv_ref,
                        q_ref, k_ref, v_ref, ckv_ref, kr_ref, za_ref, zs_ref, xbc_ref, misc_ref):
    seg = _front(x_ref, npre_ref, win_ref)
    za_ref[...] = seg(COL_ZA, COL_ZS)
    zs_ref[...] = seg(COL_ZS, COL_XBC)
    xbc_ref[...] = seg(COL_XBC, COL_MISC)
    misc = seg(COL_MISC, D_IN_PAD)
    misc_ref[...] = misc
    cos = cos_ref[...]
    sin = sin_ref[...]
    lane = lax.broadcasted_iota(jnp.int32, misc.shape, 1)
    low = lane < QK_ROPE

    kr_full = _rope_tile(misc, cos, sin)
    kr_ref[...] = kr_full[:, :QK_ROPE]
    kr_lo = jnp.where(low, kr_full, 0.0).astype(BF16)
    kr_hi = jnp.where(low, 0.0, pltpu.roll(kr_full, QK_ROPE, 1)).astype(BF16)

    ckv = _rms(seg(COL_C, COL_ZA), kvn_ref[...])
    ckv_ref[...] = ckv
    cb = ckv.astype(BF16)
    knope = _dot(cb, wuk_ref[...])
    v_ref[...] = _dot(cb, wuv_ref[...]).astype(BF16)

    qn = _rms(seg(COL_Q, COL_C), qan_ref[...]).astype(BF16)
    q = _dot(qn, wqb_ref[...])
    nope_w = MLA_HEADS * QK_NOPE
    for hh in range(MLA_HEADS):
        pair = hh // 2
        r = _rope_tile(q[:, nope_w + pair * LANES: nope_w + (pair + 1) * LANES], cos, sin)
        own = low if hh % 2 == 0 else jnp.logical_not(low)
        base = hh * QK_PAD
        q_ref[:, base:base + LANES] = (q[:, hh * QK_NOPE:(hh + 1) * QK_NOPE] * SOFTMAX_SCALE).astype(BF16)
        q_ref[:, base + LANES:base + QK_PAD] = (jnp.where(own, r, 0.0) * SOFTMAX_SCALE).astype(BF16)
        k_ref[:, base:base + LANES] = knope[:, hh * QK_NOPE:(hh + 1) * QK_NOPE].astype(BF16)
        k_ref[:, base + LANES:base + QK_PAD] = kr_lo if hh % 2 == 0 else kr_hi


def _proj_sample_kernel(x_ref, cos_ref, sin_ref, npre_ref, win_ref, qan_ref, wqb_ref, kvn_ref, wukt_ref,
                        qlat_ref, qrope_ref, ckv_ref, kr_ref, za_ref, zs_ref, xbc_ref, misc_ref):
    seg = _front(x_ref, npre_ref, win_ref)
    za_ref[...] = seg(COL_ZA, COL_ZS)
    zs_ref[...] = seg(COL_ZS, COL_XBC)
    xbc_ref[...] = seg(COL_XBC, COL_MISC)
    misc = seg(COL_MISC, D_IN_PAD)
    misc_ref[...] = misc
    cos = cos_ref[...]
    sin = sin_ref[...]
    kr_ref[...] = _rope_tile(misc, cos, sin)[:, :QK_ROPE]
    ckv_ref[...] = _rms(seg(COL_C, COL_ZA), kvn_ref[...])

    qn = _rms(seg(COL_Q, COL_C), qan_ref[...]).astype(BF16)
    q = _dot(qn, wqb_ref[...])
    nope_w = MLA_HEADS * QK_NOPE
    for pair in range(MLA_HEADS // 2):
        lo = nope_w + pair * LANES
        qrope_ref[:, pair * LANES:(pair + 1) * LANES] = _rope_tile(q[:, lo:lo + LANES], cos, sin) * SOFTMAX_SCALE
    for hh in range(MLA_HEADS):
        qh = q[:, hh * QK_NOPE:(hh + 1) * QK_NOPE].astype(BF16)
        qlat_ref[:, hh * KV_RANK:(hh + 1) * KV_RANK] = _dot(qh, wukt_ref[hh]) * SOFTMAX_SCALE


def _full(shape):
    return pl.BlockSpec(shape, lambda *_: (0,) * len(shape))


def _proj_prompt(x2, cos, sin, w):
    n = x2.shape[0]
    tm = PROJ_ROWS
    seq_tiles = cos.shape[0] // tm
    rows = lambda width: pl.BlockSpec((tm, width), lambda i: (i, 0))
    tab = pl.BlockSpec((tm, LANES), lambda i: (i % seq_tiles, 0))
    out_shape = (
        jax.ShapeDtypeStruct((n, MLA_HEADS * QK_PAD), BF16),
        jax.ShapeDtypeStruct((n, MLA_HEADS * QK_PAD), BF16),
        jax.ShapeDtypeStruct((n, D_ATTN), BF16),
        jax.ShapeDtypeStruct((n, KV_RANK), F32),
        jax.ShapeDtypeStruct((n, QK_ROPE), F32),
        jax.ShapeDtypeStruct((n, D_ATTN), F32),
        jax.ShapeDtypeStruct((n, D_SSM), F32),
        jax.ShapeDtypeStruct((n, CONV_DIM), F32),
        jax.ShapeDtypeStruct((n, LANES), F32),
    )
    return pl.pallas_call(
        _proj_prompt_kernel,
        grid=(n // tm,),
        in_specs=[rows(D_MODEL), tab, tab, _full((1, D_MODEL)), _full((D_MODEL, D_IN_PAD)),
                  _full((1, Q_RANK)), _full(w['w_qb'].shape), _full((1, KV_RANK)),
                  _full(w['w_uk2'].shape), _full(w['w_uv2'].shape)],
        out_specs=tuple(rows(s.shape[1]) for s in out_shape),
        out_shape=out_shape,
        compiler_params=pltpu.CompilerParams(dimension_semantics=("arbitrary",),
                                             vmem_limit_bytes=VMEM_LIMIT_BYTES),
        name="proj_prompt",
    )(x2, cos, sin, w['norm_pre'], w['w_in'], w['q_a_norm'], w['w_qb'], w['kv_a_norm'],
      w['w_uk2'], w['w_uv2'])


def _proj_sample(x2, cos, sin, w):
    n = x2.shape[0]
    rows = lambda width: pl.BlockSpec((n, width), lambda i: (0, 0))
    out_shape = (
        jax.ShapeDtypeStruct((n, MLA_HEADS * KV_RANK), F32),
        jax.ShapeDtypeStruct((n, MLA_HEADS * QK_ROPE), F32),
        jax.ShapeDtypeStruct((n, KV_RANK), F32),
        jax.ShapeDtypeStruct((n, QK_ROPE), F32),
        jax.ShapeDtypeStruct((n, D_ATTN), F32),
        jax.ShapeDtypeStruct((n, D_SSM), F32),
        jax.ShapeDtypeStruct((n, CONV_DIM), F32),
        jax.ShapeDtypeStruct((n, LANES), F32),
    )
    return pl.pallas_call(
        _proj_sample_kernel,
        grid=(1,),
        in_specs=[rows(D_MODEL), rows(LANES), rows(LANES), _full((1, D_MODEL)), _full((D_MODEL, D_IN_PAD)),
                  _full((1, Q_RANK)), _full(w['w_qb'].shape), _full((1, KV_RANK)),
                  _full(w['w_ukt'].shape)],
        out_specs=tuple(rows(s.shape[1]) for s in out_shape),
        out_shape=out_shape,
        compiler_params=pltpu.CompilerParams(dimension_semantics=("arbitrary",),
                                             vmem_limit_bytes=VMEM_LIMIT_BYTES),
        name="proj_sample",
    )(x2, cos, sin, w['norm_pre'], w['w_in'], w['q_a_norm'], w['w_qb'], w['kv_a_norm'], w['w_ukt'])


def _attn_prompt_kernel(q_ref, k_ref, v_ref, z_ref, o_ref):
    seq = q_ref.shape[0]
    tq = ATTN_TQ
    row = lax.broadcasted_iota(jnp.int32, (tq, tq), 0)
    col = lax.broadcasted_iota(jnp.int32, (tq, tq), 1)
    causal = col <= row
    for qi in range(seq // tq):
        lim = (qi + 1) * tq
        q = q_ref[qi * tq:lim, :]
        s = lax.dot_general(q, k_ref[0:lim, :], _NT, preferred_element_type=F32)
        diag = jnp.where(causal, s[:, lim - tq:], NEG_BIG)
        s = diag if qi == 0 else jnp.concatenate([s[:, :lim - tq], diag], axis=1)
        m = jnp.max(s, axis=-1, keepdims=True)
        p = jnp.exp(s - m)
        l = jnp.sum(p, axis=-1, keepdims=True)
        o = _dot(p.astype(BF16), v_ref[0:lim, :]) / l
        o_ref[qi * tq:lim, :] = (o * _silu(z_ref[qi * tq:lim, :])).astype(BF16)


def _attn_prompt(q3, k3, v3, z3):
    b, s, _ = q3.shape
    qk = pl.BlockSpec((None, s, QK_PAD), lambda i, j: (i, 0, j))
    hv = pl.BlockSpec((None, s, V_HEAD), lambda i, j: (i, 0, j))
    return pl.pallas_call(
        _attn_prompt_kernel,
        grid=(b, MLA_HEADS),
        in_specs=[qk, qk, hv, hv],
        out_specs=hv,
        out_shape=jax.ShapeDtypeStruct((b, s, D_ATTN), BF16),
        compiler_params=pltpu.CompilerParams(dimension_semantics=("arbitrary", "arbitrary"),
                                             vmem_limit_bytes=VMEM_LIMIT_BYTES),
        name="attn_prompt",
    )(q3, k3, v3, z3)


def _dt_lanes(shape):
    lane = lax.broadcasted_iota(jnp.int32, shape, 1)
    return jnp.logical_and(lane >= DT_LANE0, lane < DT_LANE0 + SSD_HEADS)


def _dt_and_da(misc, dtb, alog):
    dt = jnp.where(_dt_lanes(misc.shape), _softplus(misc + dtb), 0.0)
    return dt, dt * (-jnp.exp(alog))


def _split_bf16(x):
    pieces = []
    for _ in range(SPLIT_PIECES):
        piece = x.astype(BF16).astype(F32)
        pieces.append(piece)
        x = x - piece
    return pieces


def _cumsum_rows(lower_b, x):
    return sum(_dot(lower_b, piece.astype(BF16)) for piece in _split_bf16(x))


def _head_expand(x, expand_ref):
    pieces = _split_bf16(jnp.where(_dt_lanes(x.shape), x, 0.0))
    packed = pieces[0]
    for k in range(1, SPLIT_PIECES):
        packed = packed + pltpu.roll(pieces[k], k * SSD_HEADS, 1)
    return _dot(packed.astype(BF16), expand_ref[...])


def _ssd_finish(y, xs, z, dskip, normw):
    y = y + dskip * xs
    gated = y * _silu(z)
    outs = []
    for g in range(SSD_GROUPS):
        sl = slice(g * GROUP_WIDTH, (g + 1) * GROUP_WIDTH)
        outs.append(_rms(gated[:, sl], normw[:, sl]))
    return jnp.concatenate(outs, axis=1).astype(BF16)


def _out_proj(mix_a, mix_s, wout_ref, npost, x):
    o = _dot(mix_a, wout_ref[0:D_ATTN, :]) + _dot(mix_s, wout_ref[D_ATTN:D_MIX, :])
    return x + _rms(o, npost)


CONV_PAD = SUBLANES


def _ssd_prompt_kernel(xbc_ref, misc_ref, z_ref, cw_ref, cb_ref, dtb_ref, alog_ref, expand_ref,
                       dskip_ref, normw_ref, mix_ref, state_ref, xp_ref):
    c = pl.program_id(1)

    @pl.when(c == 0)
    def _():
        xp_ref[0:CONV_PAD, :] = jnp.zeros((CONV_PAD, CONV_DIM), F32)
        state_ref[...] = jnp.zeros(state_ref.shape, F32)

    xb = xbc_ref[...]
    xp_ref[CONV_PAD:CONV_PAD + CHUNK, :] = xb
    acc = cb_ref[...] + xp_ref[pl.ds(CONV_PAD - 3, CHUNK), :] * cw_ref[0:1, :]
    acc = acc + xp_ref[pl.ds(CONV_PAD - 2, CHUNK), :] * cw_ref[1:2, :]
    acc = acc + xp_ref[pl.ds(CONV_PAD - 1, CHUNK), :] * cw_ref[2:3, :]
    acc = acc + xb * cw_ref[3:4, :]
    act = _silu(acc)
    xp_ref[CONV_PAD - 3:CONV_PAD, :] = xp_ref[CONV_PAD + CHUNK - 3:CONV_PAD + CHUNK, :]

    xs = act[:, :D_SSM]
    dt, da = _dt_and_da(misc_ref[...], dtb_ref[...], alog_ref[...])
    row = lax.broadcasted_iota(jnp.int32, (CHUNK, CHUNK), 0)
    col = lax.broadcasted_iota(jnp.int32, (CHUNK, CHUNK), 1)
    lower = row >= col
    a_cum = _cumsum_rows(lower.astype(BF16), da)
    a_cum_t = a_cum.T
    a_last = a_cum[CHUNK - 1:CHUNK, :]
    xdt = xs * _head_expand(dt, expand_ref)
    decay_out = _head_expand(jnp.exp(a_cum), expand_ref)
    state_decay = decay_out[CHUNK - 1:CHUNK, :]
    xdt_end = (xdt * _head_expand(jnp.exp(a_last - a_cum), expand_ref)).astype(BF16)
    xdt_b = xdt.astype(BF16)
    lane_w = lax.broadcasted_iota(jnp.int32, (CHUNK, LANES), 1)
    first_head = lane_w < SSD_HEADDIM

    ys = []
    for g in range(SSD_GROUPS):
        bm = act[:, D_SSM + g * D_STATE:D_SSM + (g + 1) * D_STATE]
        cm = act[:, D_SSM + (SSD_GROUPS + g) * D_STATE:D_SSM + (SSD_GROUPS + g + 1) * D_STATE]
        bm_b = bm.astype(BF16)
        cm_b = cm.astype(BF16)
        cb = lax.dot_general(cm_b, bm_b, _NT, preferred_element_type=F32)
        gsl = slice(g * GROUP_WIDTH, (g + 1) * GROUP_WIDTH)
        state = state_ref[g]
        y_off = _dot(cm_b, state.astype(BF16)) * decay_out[:, gsl]
        y_diag = []
        for pair in range(HEADS_PER_GROUP // 2):
            halves = []
            x_pair = xdt_b[:, g * GROUP_WIDTH + pair * LANES: g * GROUP_WIDTH + (pair + 1) * LANES]
            for k in range(2):
                lane_h = DT_LANE0 + g * HEADS_PER_GROUP + 2 * pair + k
                seg = a_cum[:, lane_h:lane_h + 1] - a_cum_t[lane_h:lane_h + 1, :]
                decay = jnp.exp(jnp.where(lower, seg, NEG_BIG))
                halves.append(_dot((cb * decay).astype(BF16), x_pair))
            y_diag.append(jnp.where(first_head, halves[0], halves[1]))
        ys.append(jnp.concatenate(y_diag, axis=1) + y_off)
        state_ref[g] = state * state_decay[:, gsl] + _dot(bm.T.astype(BF16), xdt_end[:, gsl])

    y = jnp.concatenate(ys, axis=1)
    mix_ref[...] = _ssd_finish(y, xs, z_ref[...], dskip_ref[...], normw_ref[...])


def _ssd_prompt(xbc3, misc3, z3, w):
    b, s, _ = xbc3.shape
    nc = s // CHUNK
    blk = lambda width: pl.BlockSpec((None, CHUNK, width), lambda i, j: (i, j, 0))
    return pl.pallas_call(
        _ssd_prompt_kernel,
        grid=(b, nc),
        in_specs=[blk(CONV_DIM), blk(LANES), blk(D_SSM), _full((CONV_W, CONV_DIM)), _full((1, CONV_DIM)),
                  _full((1, LANES)), _full((1, LANES)), _full((LANES, D_SSM)),
                  _full((1, D_SSM)), _full((1, D_SSM))],
        out_specs=(blk(D_SSM),
                   pl.BlockSpec((None, SSD_GROUPS, D_STATE, GROUP_WIDTH), lambda i, j: (i, 0, 0, 0))),
        out_shape=(jax.ShapeDtypeStruct((b, s, D_SSM), BF16),
                   jax.ShapeDtypeStruct((b, SSD_GROUPS, D_STATE, GROUP_WIDTH), F32)),
        scratch_shapes=[pltpu.VMEM((CONV_PAD + CHUNK, CONV_DIM), F32)],
        compiler_params=pltpu.CompilerParams(dimension_semantics=("arbitrary", "arbitrary"),
                                             vmem_limit_bytes=VMEM_LIMIT_BYTES),
        name="ssd_prompt",
    )(xbc3, misc3, z3, w['conv_w'], w['conv_b'], w['dt_bias_t'], w['a_log_t'], w['expand'],
      w['d_skip_w'], w['ssm_norm'])


def _merge_kernel(ma_ref, ms_ref, x_ref, wout_ref, npost_ref, y_ref):
    y_ref[...] = _out_proj(ma_ref[...], ms_ref[...], wout_ref, npost_ref[...], x_ref[...])


def _merge(mix_a, mix_s, x2, w):
    n = x2.shape[0]
    tm = min(MERGE_ROWS, n)
    rows = lambda width: pl.BlockSpec((tm, width), lambda i: (i, 0))
    return pl.pallas_call(
        _merge_kernel,
        grid=(n // tm,),
        in_specs=[rows(D_ATTN), rows(D_SSM), rows(D_MODEL), _full((D_MIX, D_MODEL)), _full((1, D_MODEL))],
        out_specs=rows(D_MODEL),
        out_shape=jax.ShapeDtypeStruct((n, D_MODEL), F32),
        compiler_params=pltpu.CompilerParams(dimension_semantics=("arbitrary",),
                                             vmem_limit_bytes=VMEM_LIMIT_BYTES),
        name="merge",
    )(mix_a, mix_s, x2, w['w_out'], w['norm_post'])


def _attn_sample_kernel(layer, pt_ref, qlat_ref, qrope_ref, cnew_ref, rnew_ref, cache_c_ref, cache_rt_ref,
                        o_ref, cbuf, rbuf, sem):
    i = pl.program_id(0)
    n = pl.num_programs(0)
    n_pages = pt_ref.shape[1]
    seqs = qlat_ref.shape[0]
    past = n_pages * PAGE_SIZE
    slot = i % 2

    def page_copies(step, slot_):
        copies = []
        for j in range(seqs):
            buf = slot_ * seqs + j
            for p in range(n_pages):
                page = pt_ref[step * seqs + j, p]
                dst = pl.ds(p * PAGE_SIZE, PAGE_SIZE)
                copies.append(pltpu.make_async_copy(cache_c_ref.at[layer, page], cbuf.at[buf, dst],
                                                    sem.at[0, slot_]))
                copies.append(pltpu.make_async_copy(cache_rt_ref.at[layer, page], rbuf.at[buf, :, dst],
                                                    sem.at[1, slot_]))
        return copies

    @pl.when(i == 0)
    def _():
        for cp in page_copies(0, 0):
            cp.start()

    @pl.when(i + 1 < n)
    def _():
        for cp in page_copies(i + 1, 1 - slot):
            cp.start()

    for cp in page_copies(i, slot):
        cp.wait()

    q_pos = past
    chunks = [pl.ds(c * DEC_KV_CHUNK, DEC_KV_CHUNK) for c in range(past // DEC_KV_CHUNK)]
    for j in range(seqs):
        buf = slot * seqs + j
        qlat = qlat_ref[j]
        qrope = qrope_ref[j]
        cnew = cnew_ref[j]
        rnew = rnew_ref[j]
        s_new = (jnp.sum(qlat * cnew, axis=-1, keepdims=True)
                 + jnp.sum(qrope * rnew, axis=-1, keepdims=True))
        s = jnp.concatenate(
            [lax.dot_general(qlat, cbuf[buf, keys, :], _NT, preferred_element_type=F32)
             + _dot(qrope, rbuf[buf, :, keys]) for keys in chunks], axis=1)
        k_pos = lax.broadcasted_iota(jnp.int32, s.shape, 1)
        s = jnp.where(k_pos <= q_pos, s, NEG_BIG)
        m = jnp.maximum(jnp.max(s, axis=-1, keepdims=True), s_new)
        p = jnp.exp(s - m)
        p_new = jnp.exp(s_new - m)
        l = jnp.sum(p, axis=-1, keepdims=True) + p_new
        acc = p_new * cnew
        for c, keys in enumerate(chunks):
            acc = acc + _dot(p[:, c * DEC_KV_CHUNK:(c + 1) * DEC_KV_CHUNK], cbuf[buf, keys, :])
        o_ref[j] = acc / l


def _attn_sample(layer, page_table, qlat3, qrope3, cnew3, rnew3, cache_c, cache_rt):
    b, n_pages = page_table.shape
    past = n_pages * PAGE_SIZE
    seqs = DEC_ATTN_SEQS
    assert past % DEC_KV_CHUNK == 0 and b % seqs == 0
    per_step = lambda d1, d2: pl.BlockSpec((seqs, d1, d2), lambda i, pt: (i, 0, 0))
    grid_spec = pltpu.PrefetchScalarGridSpec(
        num_scalar_prefetch=1,
        grid=(b // seqs,),
        in_specs=[per_step(MLA_HEADS, KV_RANK), per_step(MLA_HEADS, QK_ROPE), per_step(1, KV_RANK),
                  per_step(1, QK_ROPE), pl.BlockSpec(memory_space=pl.ANY), pl.BlockSpec(memory_space=pl.ANY)],
        out_specs=per_step(MLA_HEADS, KV_RANK),
        scratch_shapes=[pltpu.VMEM((2 * seqs, past, KV_RANK), F32), pltpu.VMEM((2 * seqs, QK_ROPE, past), F32),
                        pltpu.SemaphoreType.DMA((2, 2))],
    )
    return pl.pallas_call(
        functools.partial(_attn_sample_kernel, layer),
        grid_spec=grid_spec,
        out_shape=jax.ShapeDtypeStruct((b, MLA_HEADS, KV_RANK), F32),
        compiler_params=pltpu.CompilerParams(dimension_semantics=("arbitrary",),
                                             vmem_limit_bytes=VMEM_LIMIT_BYTES),
        name="attn_sample",
    )(page_table, qlat3, qrope3, cnew3, rnew3, cache_c, cache_rt)


def _post_sample_kernel(olat_ref, wuv_ref, za_ref, xbc_ref, misc_ref, cprev_ref, cw_ref, cb_ref,
                        dtb_ref, alog_ref, expand_ref,
                        mixa_ref, cnew_ref, act_ref, xdt_t_ref, dec_t_ref):
    for hh in range(MLA_HEADS):
        o = _dot(olat_ref[:, hh * KV_RANK:(hh + 1) * KV_RANK].astype(BF16), wuv_ref[hh])
        vs = slice(hh * V_HEAD, (hh + 1) * V_HEAD)
        mixa_ref[:, vs] = (o * _silu(za_ref[:, vs])).astype(BF16)

    xb = xbc_ref[...]
    acc = cb_ref[...]
    for k in range(CONV_W - 1):
        acc = acc + cprev_ref[:, k * CONV_DIM:(k + 1) * CONV_DIM] * cw_ref[k:k + 1, :]
    acc = acc + xb * cw_ref[CONV_W - 1:CONV_W, :]
    act = _silu(acc)
    act_ref[...] = act
    cnew_ref[:, 0:(CONV_W - 2) * CONV_DIM] = cprev_ref[:, CONV_DIM:(CONV_W - 1) * CONV_DIM]
    cnew_ref[:, (CONV_W - 2) * CONV_DIM:(CONV_W - 1) * CONV_DIM] = xb

    dt, da = _dt_and_da(misc_ref[...], dtb_ref[...], alog_ref[...])
    xdt = act[:, :D_SSM] * _head_expand(dt, expand_ref)
    decay = _head_expand(jnp.exp(da), expand_ref)
    xdt_t_ref[...] = xdt.T
    dec_t_ref[...] = decay.T


def _post_sample(olat2, za, xbc, misc, cprev2, w):
    n = olat2.shape[0]
    full = lambda a: _full(a.shape)
    args = (olat2, w['w_uvh'], za, xbc, misc, cprev2, w['conv_w'], w['conv_b'], w['dt_bias_t'], w['a_log_t'],
            w['expand'])
    out_shape = (
        jax.ShapeDtypeStruct((n, D_ATTN), BF16),
        jax.ShapeDtypeStruct((n, (CONV_W - 1) * CONV_DIM), F32),
        jax.ShapeDtypeStruct((n, CONV_DIM), F32),
        jax.ShapeDtypeStruct((D_SSM, n), F32),
        jax.ShapeDtypeStruct((D_SSM, n), F32),
    )
    return pl.pallas_call(
        _post_sample_kernel,
        grid=(1,),
        in_specs=[full(a) for a in args],
        out_specs=tuple(_full(s.shape) for s in out_shape),
        out_shape=out_shape,
        compiler_params=pltpu.CompilerParams(dimension_semantics=("arbitrary",),
                                             vmem_limit_bytes=VMEM_LIMIT_BYTES),
        name="post_sample",
    )(*args)


def _state_sample_kernel(h0_ref, xdt_t_ref, dec_t_ref, bm_ref, cm_ref, h_ref, y_t_ref):
    t = pl.program_id(0)
    tb = h0_ref.shape[0]
    rows, n_seq = xdt_t_ref.shape

    @pl.when(t == 0)
    def _():
        y_t_ref[...] = jnp.zeros(y_t_ref.shape, F32)

    lane = lax.broadcasted_iota(jnp.int32, (rows, n_seq), 1)
    xdt_t = xdt_t_ref[...]
    dec_t = dec_t_ref[...]
    for j in range(tb):
        own = lane == t * tb + j
        x_col = jnp.sum(jnp.where(own, xdt_t, 0.0), axis=1, keepdims=True)
        d_col = jnp.sum(jnp.where(own, dec_t, 0.0), axis=1, keepdims=True)
        b_rows = jnp.concatenate(
            [jnp.broadcast_to(bm_ref[j:j + 1, g * D_STATE:(g + 1) * D_STATE], (GROUP_WIDTH, D_STATE))
             for g in range(SSD_GROUPS)], axis=0)
        c_rows = jnp.concatenate(
            [jnp.broadcast_to(cm_ref[j:j + 1, g * D_STATE:(g + 1) * D_STATE], (GROUP_WIDTH, D_STATE))
             for g in range(SSD_GROUPS)], axis=0)
        h = d_col * h0_ref[j] + x_col * b_rows
        h_ref[j] = h
        y_col = jnp.sum(h * c_rows, axis=1, keepdims=True)
        y_t_ref[...] = jnp.where(own, y_col, y_t_ref[...])


def _state_sample(h0, xdt_t, dec_t, act):
    n = h0.shape[0]
    tb = DEC_STATE_SEQS
    bc_w = SSD_GROUPS * D_STATE
    st = pl.BlockSpec((tb, D_SSM, D_STATE), lambda i: (i, 0, 0))
    return pl.pallas_call(
        _state_sample_kernel,
        grid=(n // tb,),
        in_specs=[st, _full((D_SSM, n)), _full((D_SSM, n)),
                  pl.BlockSpec((tb, bc_w), lambda i: (i, D_SSM // bc_w)),
                  pl.BlockSpec((tb, bc_w), lambda i: (i, D_SSM // bc_w + 1))],
        out_specs=(st, _full((D_SSM, n))),
        out_shape=(jax.ShapeDtypeStruct(h0.shape, F32), jax.ShapeDtypeStruct((D_SSM, n), F32)),
        compiler_params=pltpu.CompilerParams(dimension_semantics=("arbitrary",),
                                             vmem_limit_bytes=VMEM_LIMIT_BYTES),
        name="state_sample",
    )(h0, xdt_t, dec_t, act, act)


def _finish_sample_kernel(y_t_ref, act_ref, z_ref, dskip_ref, normw_ref, ma_ref, x_ref, wout_ref, npost_ref,
                          y_ref):
    mix_s = _ssd_finish(y_t_ref[...].T, act_ref[...], z_ref[...], dskip_ref[...], normw_ref[...])
    y_ref[...] = _out_proj(ma_ref[...], mix_s, wout_ref, npost_ref[...], x_ref[...])


def _finish_sample(y_t, act, zs, mix_a, x2, w):
    n = x2.shape[0]
    return pl.pallas_call(
        _finish_sample_kernel,
        grid=(1,),
        in_specs=[_full(y_t.shape), pl.BlockSpec((n, D_SSM), lambda i: (0, 0)), _full(zs.shape),
                  _full((1, D_SSM)), _full((1, D_SSM)), _full(mix_a.shape), _full(x2.shape),
                  _full((D_MIX, D_MODEL)), _full((1, D_MODEL))],
        out_specs=_full((n, D_MODEL)),
        out_shape=jax.ShapeDtypeStruct((n, D_MODEL), F32),
        compiler_params=pltpu.CompilerParams(dimension_semantics=("arbitrary",),
                                             vmem_limit_bytes=VMEM_LIMIT_BYTES),
        name="finish_sample",
    )(y_t, act, zs, w['d_skip_w'], w['ssm_norm'], mix_a, x2, w['w_out'], w['norm_post'])


def _rope_tables(pos):
    inv_freq = ROPE_THETA ** (-jnp.arange(ROPE_HALF, dtype=F32) / ROPE_HALF)
    ang = pos.astype(F32)[:, None] * inv_freq[None, :]
    cos, sin = jnp.cos(ang), jnp.sin(ang)
    reps = LANES // QK_ROPE
    return jnp.tile(cos, (1, 2 * reps)), jnp.tile(jnp.concatenate([-sin, sin], axis=1), (1, reps))


def _prep_weights(lw):
    row = lambda v: v.reshape(1, -1).astype(F32)
    w_in = lw['w_in']
    q_a, c_raw, kr, za, zs, xbc, dt = jnp.split(
        w_in, (Q_RANK, Q_RANK + KV_RANK, Q_RANK + KV_RANK + QK_ROPE,
               Q_RANK + KV_RANK + QK_ROPE + D_ATTN, Q_RANK + KV_RANK + QK_ROPE + D_ATTN + D_SSM,
               Q_RANK + KV_RANK + QK_ROPE + D_ATTN + D_SSM + CONV_DIM), axis=1)
    pad = jnp.zeros((D_MODEL, LANES - QK_ROPE - SSD_HEADS), w_in.dtype)
    w_qb = lw['w_q_b'].reshape(Q_RANK, MLA_HEADS, QK_NOPE + QK_ROPE)
    lane_pad = lambda v: jnp.pad(v.reshape(1, -1).astype(F32), ((0, 0), (DT_LANE0, LANES - DT_LANE0 - SSD_HEADS)))
    head_of_col = jnp.arange(D_SSM) // SSD_HEADDIM
    return {
        'norm_pre': row(lw['norm_pre']),
        'w_in': jnp.concatenate([q_a, c_raw, za, zs, xbc, kr, dt, pad], axis=1).astype(BF16),
        'q_a_norm': row(lw['q_a_norm']),
        'w_qb': jnp.concatenate([w_qb[:, :, :QK_NOPE].reshape(Q_RANK, -1),
                                 w_qb[:, :, QK_NOPE:].reshape(Q_RANK, -1)], axis=1).astype(BF16),
        'kv_a_norm': row(lw['kv_a_norm']),
        'w_uk2': lw['w_uk'].reshape(KV_RANK, MLA_HEADS * QK_NOPE).astype(BF16),
        'w_uv2': lw['w_uv'].reshape(KV_RANK, MLA_HEADS * V_HEAD).astype(BF16),
        'w_ukt': jnp.transpose(lw['w_uk'], (1, 2, 0)).astype(BF16),
        'w_uvh': jnp.transpose(lw['w_uv'], (1, 0, 2)).astype(BF16),
        'conv_w': lw['conv_w'].astype(F32),
        'conv_b': row(lw['conv_b']),
        'dt_bias_t': lane_pad(lw['dt_bias']),
        'a_log_t': lane_pad(lw['a_log']),
        'expand': sum((jnp.arange(LANES)[:, None] == DT_LANE0 + k * SSD_HEADS + head_of_col[None, :])
                      for k in range(SPLIT_PIECES)).astype(BF16),
        'd_skip_w': jnp.repeat(lw['d_skip'].astype(F32), SSD_HEADDIM).reshape(1, D_SSM),
        'ssm_norm': row(lw['ssm_norm']),
        'w_out': lw['w_out'].astype(BF16),
        'norm_post': row(lw['norm_post']),
    }


def _prompt_layer(x, w):
    b, s, _ = x.shape
    x2 = x.reshape(b * s, D_MODEL)
    cos, sin = _rope_tables(jnp.arange(s, dtype=jnp.int32))
    q, k, v, ckv, kr, za, zs, xbc, misc = _proj_prompt(x2, cos, sin, w)
    r3 = lambda a: a.reshape(b, s, a.shape[-1])
    mix_a = _attn_prompt(r3(q), r3(k), r3(v), r3(za))
    xbc3 = r3(xbc)
    mix_s, state = _ssd_prompt(xbc3, r3(misc), r3(zs), w)
    y = _merge(mix_a.reshape(b * s, D_ATTN), mix_s.reshape(b * s, D_SSM), x2, w)
    h = state.reshape(b, SSD_GROUPS, D_STATE, HEADS_PER_GROUP, SSD_HEADDIM)
    h = jnp.transpose(h, (0, 1, 3, 4, 2)).reshape(b, SSD_HEADS, SSD_HEADDIM, D_STATE)
    return (y.reshape(b, s, D_MODEL), r3(ckv), r3(kr), xbc3[:, s - (CONV_W - 1):, :], h)


def _sample_layer(layer, x, cache_c, cache_r, conv_prev, h0, page_table, w):
    b, s, _ = x.shape
    n = b * s
    past = page_table.shape[1] * PAGE_SIZE
    x2 = x.reshape(n, D_MODEL)
    pos = past + jnp.arange(s, dtype=jnp.int32)
    cos, sin = _rope_tables(jnp.tile(pos, b))
    qlat, qrope, ckv, kr, za, zs, xbc, misc = _proj_sample(x2, cos, sin, w)
    olat = _attn_sample(layer, page_table, qlat.reshape(n, MLA_HEADS, KV_RANK), qrope.reshape(n, MLA_HEADS, QK_ROPE),
                        ckv.reshape(n, 1, KV_RANK), kr.reshape(n, 1, QK_ROPE), cache_c,
                        jnp.swapaxes(cache_r, 2, 3))
    mix_a, conv_new, act, xdt_t, dec_t = _post_sample(
        olat.reshape(n, MLA_HEADS * KV_RANK), za, xbc, misc, conv_prev.reshape(n, (CONV_W - 1) * CONV_DIM), w)
    h, y_t = _state_sample(h0.astype(F32).reshape(n, D_SSM, D_STATE), xdt_t, dec_t, act)
    y = _finish_sample(y_t, act, zs, mix_a, x2, w)
    return (y.reshape(b, s, D_MODEL), ckv.reshape(b, s, KV_RANK), kr.reshape(b, s, QK_ROPE),
            conv_new.reshape(b, CONV_W - 1, CONV_DIM), h.reshape(b, SSD_HEADS, SSD_HEADDIM, D_STATE))


def kernel(x_prompt, x_sample, cache_ckv, cache_krope, state_conv, state_ssm, page_table, norm_pre, w_in,
           q_a_norm, w_q_b, kv_a_norm, w_uk, w_uv, conv_w, conv_b, dt_bias, a_log, d_skip, ssm_norm, w_out,
           norm_post):
    assert x_sample.shape[1] == 1, "the sample path handles one new token per sequence"
    depth = w_in.shape[0]
    y_prompt, y_sample = x_prompt, x_sample
    outs = [[] for _ in range(8)]
    for l in range(depth):
        w = _prep_weights({'norm_pre': norm_pre[l], 'w_in': w_in[l], 'q_a_norm': q_a_norm[l],
                           'w_q_b': w_q_b[l], 'kv_a_norm': kv_a_norm[l], 'w_uk': w_uk[l], 'w_uv': w_uv[l],
                           'conv_w': conv_w[l], 'conv_b': conv_b[l], 'dt_bias': dt_bias[l], 'a_log': a_log[l],
                           'd_skip': d_skip[l], 'ssm_norm': ssm_norm[l], 'w_out': w_out[l],
                           'norm_post': norm_post[l]})
        y_prompt, c1, k1, v1, h1 = _prompt_layer(y_prompt, w)
        y_sample, c2, k2, v2, h2 = _sample_layer(l, y_sample, cache_ckv, cache_krope, state_conv[l],
                                                 state_ssm[l], page_table, w)
        for lst, val in zip(outs, (c1, k1, v1, h1, c2, k2, v2, h2)):
            lst.append(val)
    return (y_prompt, y_sample) + tuple(jnp.stack(o) for o in outs)
```

```python
import functools
import math

import jax
import jax.numpy as jnp
from jax import lax
from jax.experimental import pallas as pl
from jax.experimental.pallas import tpu as pltpu

F32 = jnp.float32
BF16 = jnp.bfloat16

D_MODEL = 1024
PAGE_SIZE = 128
D_MIX = 2 * D_MODEL
D_ATTN = D_MIX // 2
D_SSM = D_MIX - D_ATTN
MLA_HEADS = 8
QK_NOPE = 128
QK_ROPE = 64
ROPE_HALF = QK_ROPE // 2
V_HEAD = D_ATTN // MLA_HEADS
Q_RANK = 384
KV_RANK = 256
ROPE_THETA = 10000.0
SOFTMAX_SCALE = (QK_NOPE + QK_ROPE) ** -0.5
SSD_HEADDIM = 64
SSD_HEADS = D_SSM // SSD_HEADDIM
SSD_GROUPS = 2
HEADS_PER_GROUP = SSD_HEADS // SSD_GROUPS
GROUP_WIDTH = D_SSM // SSD_GROUPS
D_STATE = 128
CONV_W = 4
CONV_DIM = D_SSM + 2 * SSD_GROUPS * D_STATE
CHUNK = 128
EPS = 1e-6
SPLIT_PIECES = 3
NEG_BIG = -1e30

LANES = 128
SUBLANES = 8
VMEM_LIMIT_BYTES = 56 * 1024 * 1024

COL_Q = 0
COL_C = COL_Q + Q_RANK
COL_ZA = COL_C + KV_RANK
COL_ZS = COL_ZA + D_ATTN
COL_XBC = COL_ZS + D_SSM
COL_MISC = COL_XBC + CONV_DIM
D_IN_PAD = COL_MISC + LANES
DT_LANE0 = QK_ROPE
QK_PAD = 2 * LANES

PROJ_ROWS = 256
ATTN_TQ = 256
ATTN_HEADS = 2
SSD_STEP_CHUNKS = 2
DEC_STATE_SEQS = 8
DEC_KV_CHUNK = 2048
DEC_ATTN_SEQS = 2

_NT = (((1,), (1,)), ((), ()))


def _rms(x, w):
    return x * lax.rsqrt(jnp.mean(x * x, axis=-1, keepdims=True) + EPS) * w


def _silu(x):
    return x / (1.0 + jnp.exp(-x))


def _softplus(x):
    return jnp.maximum(x, 0.0) + jnp.log1p(jnp.exp(-jnp.abs(x)))


def _dot(a, b):
    return jnp.dot(a, b, preferred_element_type=F32)


def _rope_tile(x, cos, sin_signed):
    lane = lax.broadcasted_iota(jnp.int32, x.shape, 1)
    first_half = (lane % QK_ROPE) < ROPE_HALF
    partner = jnp.where(first_half,
                        pltpu.roll(x, LANES - ROPE_HALF, 1),
                        pltpu.roll(x, ROPE_HALF, 1))
    return x * cos + partner * sin_signed


def _front(x_ref, npre_ref, win_ref):
    h = _rms(x_ref[...], npre_ref[...]).astype(BF16)

    def seg(lo, hi):
        return _dot(h, win_ref[:, lo:hi])

    return seg


def _proj_prompt_kernel(x_ref, cos_ref, sin_ref, npre_ref, win_ref, qan_ref, wqb_ref, kvn_ref,
                        wukt_ref, wuv_ref,
                        q_ref, kt_ref, v_ref, ckv_ref, kr_ref, za_ref, zs_ref, xbc_ref, misc_ref):
    seg = _front(x_ref, npre_ref, win_ref)
    za_ref[...] = seg(COL_ZA, COL_ZS)
    zs_ref[...] = seg(COL_ZS, COL_XBC)
    xbc_ref[...] = seg(COL_XBC, COL_MISC)
    misc = seg(COL_MISC, D_IN_PAD)
    misc_ref[...] = misc
    cos = cos_ref[...]
    sin = sin_ref[...]
    lane = lax.broadcasted_iota(jnp.int32, misc.shape, 1)
    low = lane < QK_ROPE

    kr_full = _rope_tile(misc, cos, sin)
    kr_ref[...] = kr_full[:, :QK_ROPE]
    kr_lo_t = jnp.where(low, kr_full, 0.0).T.astype(BF16)
    kr_hi_t = jnp.where(low, 0.0, pltpu.roll(kr_full, QK_ROPE, 1)).T.astype(BF16)

    ckv = _rms(seg(COL_C, COL_ZA), kvn_ref[...])
    ckv_ref[...] = ckv
    cb = ckv.astype(BF16)
    knope_t = lax.dot_general(wukt_ref[...], cb, _NT, preferred_element_type=F32)
    v_ref[...] = _dot(cb, wuv_ref[...]).astype(BF16)

    qn = _rms(seg(COL_Q, COL_C), qan_ref[...]).astype(BF16)
    q = _dot(qn, wqb_ref[...])
    nope_w = MLA_HEADS * QK_NOPE
    for hh in range(MLA_HEADS):
        pair = hh // 2
        r = _rope_tile(q[:, nope_w + pair * LANES: nope_w + (pair + 1) * LANES], cos, sin)
        own = low if hh % 2 == 0 else jnp.logical_not(low)
        base = hh * QK_PAD
        q_ref[:, base:base + LANES] = (q[:, hh * QK_NOPE:(hh + 1) * QK_NOPE] * SOFTMAX_SCALE).astype(BF16)
        q_ref[:, base + LANES:base + QK_PAD] = (jnp.where(own, r, 0.0) * SOFTMAX_SCALE).astype(BF16)
        kt_ref[base:base + LANES, :] = knope_t[hh * QK_NOPE:(hh + 1) * QK_NOPE, :].astype(BF16)
        kt_ref[base + LANES:base + QK_PAD, :] = kr_lo_t if hh % 2 == 0 else kr_hi_t


def _proj_sample_kernel(x_ref, cos_ref, sin_ref, npre_ref, win_ref, qan_ref, wqb_ref, kvn_ref, wukt_ref,
                        qlat_ref, qrope_ref, ckv_ref, kr_ref, za_ref, zs_ref, xbc_ref, misc_ref):
    seg = _front(x_ref, npre_ref, win_ref)
    za_ref[...] = seg(COL_ZA, COL_ZS)
    zs_ref[...] = seg(COL_ZS, COL_XBC)
    xbc_ref[...] = seg(COL_XBC, COL_MISC)
    misc = seg(COL_MISC, D_IN_PAD)
    misc_ref[...] = misc
    cos = cos_ref[...]
    sin = sin_ref[...]
    kr_ref[...] = _rope_tile(misc, cos, sin)[:, :QK_ROPE]
    ckv_ref[...] = _rms(seg(COL_C, COL_ZA), kvn_ref[...])

    qn = _rms(seg(COL_Q, COL_C), qan_ref[...]).astype(BF16)
    q = _dot(qn, wqb_ref[...])
    nope_w = MLA_HEADS * QK_NOPE
    for pair in range(MLA_HEADS // 2):
        lo = nope_w + pair * LANES
        qrope_ref[:, pair * LANES:(pair + 1) * LANES] = _rope_tile(q[:, lo:lo + LANES], cos, sin) * SOFTMAX_SCALE
    for hh in range(MLA_HEADS):
        qh = q[:, hh * QK_NOPE:(hh + 1) * QK_NOPE].astype(BF16)
        qlat_ref[:, hh * KV_RANK:(hh + 1) * KV_RANK] = _dot(qh, wukt_ref[hh]) * SOFTMAX_SCALE


def _full(shape):
    return pl.BlockSpec(shape, lambda *_: (0,) * len(shape))


def _proj_prompt(x2, cos, sin, w):
    n = x2.shape[0]
    tm = PROJ_ROWS
    seq_tiles = cos.shape[0] // tm
    rows = lambda width: pl.BlockSpec((tm, width), lambda i: (i, 0))
    tab = pl.BlockSpec((tm, LANES), lambda i: (i % seq_tiles, 0))
    out_shape = (
        jax.ShapeDtypeStruct((n, MLA_HEADS * QK_PAD), BF16),
        jax.ShapeDtypeStruct((MLA_HEADS * QK_PAD, n), BF16),
        jax.ShapeDtypeStruct((n, D_ATTN), BF16),
        jax.ShapeDtypeStruct((n, KV_RANK), F32),
        jax.ShapeDtypeStruct((n, QK_ROPE), F32),
        jax.ShapeDtypeStruct((n, D_ATTN), F32),
        jax.ShapeDtypeStruct((n, D_SSM), F32),
        jax.ShapeDtypeStruct((n, CONV_DIM), F32),
        jax.ShapeDtypeStruct((n, LANES), F32),
    )
    return pl.pallas_call(
        _proj_prompt_kernel,
        grid=(n // tm,),
        in_specs=[rows(D_MODEL), tab, tab, _full((1, D_MODEL)), _full((D_MODEL, D_IN_PAD)),
                  _full((1, Q_RANK)), _full(w['w_qb'].shape), _full((1, KV_RANK)),
                  _full(w['w_uk2t'].shape), _full(w['w_uv2'].shape)],
        out_specs=tuple(pl.BlockSpec((s.shape[0], tm), lambda i: (0, i)) if k == 1 else rows(s.shape[1])
                        for k, s in enumerate(out_shape)),
        out_shape=out_shape,
        compiler_params=pltpu.CompilerParams(dimension_semantics=("arbitrary",),
                                             vmem_limit_bytes=VMEM_LIMIT_BYTES),
        name="proj_prompt",
    )(x2, cos, sin, w['norm_pre'], w['w_in'], w['q_a_norm'], w['w_qb'], w['kv_a_norm'],
      w['w_uk2t'], w['w_uv2'])


def _proj_sample(x2, cos, sin, w):
    n = x2.shape[0]
    rows = lambda width: pl.BlockSpec((n, width), lambda i: (0, 0))
    out_shape = (
        jax.ShapeDtypeStruct((n, MLA_HEADS * KV_RANK), F32),
        jax.ShapeDtypeStruct((n, MLA_HEADS * QK_ROPE), F32),
        jax.ShapeDtypeStruct((n, KV_RANK), F32),
        jax.ShapeDtypeStruct((n, QK_ROPE), F32),
        jax.ShapeDtypeStruct((n, D_ATTN), F32),
        jax.ShapeDtypeStruct((n, D_SSM), F32),
        jax.ShapeDtypeStruct((n, CONV_DIM), F32),
        jax.ShapeDtypeStruct((n, LANES), F32),
    )
    return pl.pallas_call(
        _proj_sample_kernel,
        grid=(1,),
        in_specs=[rows(D_MODEL), rows(LANES), rows(LANES), _full((1, D_MODEL)), _full((D_MODEL, D_IN_PAD)),
                  _full((1, Q_RANK)), _full(w['w_qb'].shape), _full((1, KV_RANK)),
                  _full(w['w_ukt'].shape)],
        out_specs=tuple(rows(s.shape[1]) for s in out_shape),
        out_shape=out_shape,
        compiler_params=pltpu.CompilerParams(dimension_semantics=("arbitrary",),
                                             vmem_limit_bytes=VMEM_LIMIT_BYTES),
        name="proj_sample",
    )(x2, cos, sin, w['norm_pre'], w['w_in'], w['q_a_norm'], w['w_qb'], w['kv_a_norm'], w['w_ukt'])


def _attn_prompt_kernel(q_ref, kt_ref, v_ref, z_ref, o_ref):
    seq = q_ref.shape[0]
    tq = ATTN_TQ
    row = lax.broadcasted_iota(jnp.int32, (tq, tq), 0)
    col = lax.broadcasted_iota(jnp.int32, (tq, tq), 1)
    causal = col <= row
    for qi in reversed(range(seq // tq)):
        lim = (qi + 1) * tq
        ones_col = (lax.broadcasted_iota(jnp.int32, (lim, LANES), 1) == 0).astype(BF16)
        for hd in range(ATTN_HEADS):
            qk = slice(hd * QK_PAD, (hd + 1) * QK_PAD)
            hv = slice(hd * V_HEAD, (hd + 1) * V_HEAD)
            s = _dot(q_ref[qi * tq:lim, qk], kt_ref[qk, 0:lim])
            diag = jnp.where(causal, s[:, lim - tq:], NEG_BIG)
            s = diag if qi == 0 else jnp.concatenate([s[:, :lim - tq], diag], axis=1)
            m = jnp.max(s, axis=-1, keepdims=True)
            p = jnp.exp(s - m).astype(BF16)
            ol = _dot(p, jnp.concatenate([v_ref[0:lim, hv], ones_col], axis=1))
            o = ol[:, :V_HEAD] / jnp.sum(ol[:, V_HEAD:], axis=-1, keepdims=True)
            o_ref[qi * tq:lim, hv] = (o * _silu(z_ref[qi * tq:lim, hv])).astype(BF16)


def _attn_prompt(q3, kt, v3, z3):
    b, s, _ = q3.shape
    nh = ATTN_HEADS
    hv = pl.BlockSpec((None, s, nh * V_HEAD), lambda i, j: (i, 0, j))
    return pl.pallas_call(
        _attn_prompt_kernel,
        grid=(b, MLA_HEADS // nh),
        in_specs=[pl.BlockSpec((None, s, nh * QK_PAD), lambda i, j: (i, 0, j)),
                  pl.BlockSpec((nh * QK_PAD, s), lambda i, j: (j, i)), hv, hv],
        out_specs=hv,
        out_shape=jax.ShapeDtypeStruct((b, s, D_ATTN), BF16),
        compiler_params=pltpu.CompilerParams(dimension_semantics=("arbitrary", "arbitrary"),
                                             vmem_limit_bytes=VMEM_LIMIT_BYTES),
        name="attn_prompt",
    )(q3, kt, v3, z3)


def _dt_lanes(shape):
    lane = lax.broadcasted_iota(jnp.int32, shape, 1)
    return jnp.logical_and(lane >= DT_LANE0, lane < DT_LANE0 + SSD_HEADS)


def _dt_and_da(misc, dtb, alog):
    dt = jnp.where(_dt_lanes(misc.shape), _softplus(misc + dtb), 0.0)
    return dt, dt * (-jnp.exp(alog))


def _split_bf16(x):
    pieces = []
    for _ in range(SPLIT_PIECES):
        piece = x.astype(BF16).astype(F32)
        pieces.append(piece)
        x = x - piece
    return pieces


def _cumsum_rows(lower_b, x):
    return sum(_dot(lower_b, piece.astype(BF16)) for piece in _split_bf16(x))


def _head_expand(x, expand_ref):
    pieces = _split_bf16(jnp.where(_dt_lanes(x.shape), x, 0.0))
    packed = pieces[0]
    for k in range(1, SPLIT_PIECES):
        packed = packed + pltpu.roll(pieces[k], k * SSD_HEADS, 1)
    return _dot(packed.astype(BF16), expand_ref[...])


def _ssd_finish(y, xs, z, dskip, normw):
    y = y + dskip * xs
    gated = y * _silu(z)
    outs = []
    for g in range(SSD_GROUPS):
        sl = slice(g * GROUP_WIDTH, (g + 1) * GROUP_WIDTH)
        outs.append(_rms(gated[:, sl], normw[:, sl]))
    return jnp.concatenate(outs, axis=1).astype(BF16)


def _out_proj(mix_a, mix_s, wout_ref, npost, x):
    o = _dot(mix_a, wout_ref[0:D_ATTN, :]) + _dot(mix_s, wout_ref[D_ATTN:D_MIX, :])
    return x + _rms(o, npost)


CONV_PAD = SUBLANES


def _ssd_chunk(act, misc, dtb_ref, alog_ref, expand_ref, state_ref):
    xs = act[:, :D_SSM]
    dt, da = _dt_and_da(misc, dtb_ref[...], alog_ref[...])
    row = lax.broadcasted_iota(jnp.int32, (CHUNK, CHUNK), 0)
    col = lax.broadcasted_iota(jnp.int32, (CHUNK, CHUNK), 1)
    lower = row >= col
    a_cum = _cumsum_rows(lower.astype(BF16), da)
    a_cum_t = a_cum.T
    a_last = a_cum[CHUNK - 1:CHUNK, :]
    xdt = xs * _head_expand(dt, expand_ref)
    decay_out = _head_expand(jnp.exp(a_cum), expand_ref)
    state_decay = decay_out[CHUNK - 1:CHUNK, :]
    xdt_end = (xdt * _head_expand(jnp.exp(a_last - a_cum), expand_ref)).astype(BF16)
    xdt_b = xdt.astype(BF16)
    lane_w = lax.broadcasted_iota(jnp.int32, (CHUNK, LANES), 1)
    first_head = lane_w < SSD_HEADDIM

    ys = []
    for g in range(SSD_GROUPS):
        bm = act[:, D_SSM + g * D_STATE:D_SSM + (g + 1) * D_STATE]
        cm = act[:, D_SSM + (SSD_GROUPS + g) * D_STATE:D_SSM + (SSD_GROUPS + g + 1) * D_STATE]
        bm_b = bm.astype(BF16)
        cm_b = cm.astype(BF16)
        cb = lax.dot_general(cm_b, bm_b, _NT, preferred_element_type=F32)
        gsl = slice(g * GROUP_WIDTH, (g + 1) * GROUP_WIDTH)
        state = state_ref[g]
        y_off = _dot(cm_b, state.astype(BF16)) * decay_out[:, gsl]
        y_diag = []
        for pair in range(HEADS_PER_GROUP // 2):
            halves = []
            x_pair = xdt_b[:, g * GROUP_WIDTH + pair * LANES: g * GROUP_WIDTH + (pair + 1) * LANES]
            for k in range(2):
                lane_h = DT_LANE0 + g * HEADS_PER_GROUP + 2 * pair + k
                seg = a_cum[:, lane_h:lane_h + 1] - a_cum_t[lane_h:lane_h + 1, :]
                decay = jnp.exp(jnp.where(lower, seg, NEG_BIG))
                halves.append(_dot((cb * decay).astype(BF16), x_pair))
            y_diag.append(jnp.where(first_head, halves[0], halves[1]))
        ys.append(jnp.concatenate(y_diag, axis=1) + y_off)
        state_ref[g] = state * state_decay[:, gsl] + _dot(bm.T.astype(BF16), xdt_end[:, gsl])
    return jnp.concatenate(ys, axis=1)


def _ssd_prompt_kernel(xbc_ref, misc_ref, z_ref, mixa_ref, x_ref, cw_ref, cb_ref, dtb_ref, alog_ref, expand_ref,
                       dskip_ref, normw_ref, wout_ref, npost_ref, y_ref, state_ref, xp_ref):
    c = pl.program_id(1)
    rows = xbc_ref.shape[0]

    @pl.when(c == 0)
    def _():
        xp_ref[0:CONV_PAD, :] = jnp.zeros((CONV_PAD, CONV_DIM), F32)
        state_ref[...] = jnp.zeros(state_ref.shape, F32)

    xp_ref[CONV_PAD:CONV_PAD + rows, :] = xbc_ref[...]
    mixes = []
    for k in range(rows // CHUNK):
        r0 = k * CHUNK
        acc = cb_ref[...]
        for tap in range(CONV_W):
            shifted = xp_ref[pl.ds(CONV_PAD + r0 - (CONV_W - 1) + tap, CHUNK), :]
            acc = acc + shifted * cw_ref[tap:tap + 1, :]
        act = _silu(acc)
        y = _ssd_chunk(act, misc_ref[r0:r0 + CHUNK, :], dtb_ref, alog_ref, expand_ref, state_ref)
        mixes.append(_ssd_finish(y, act[:, :D_SSM], z_ref[r0:r0 + CHUNK, :], dskip_ref[...], normw_ref[...]))
    xp_ref[CONV_PAD - 3:CONV_PAD, :] = xp_ref[CONV_PAD + rows - 3:CONV_PAD + rows, :]
    y_ref[...] = _out_proj(mixa_ref[...], jnp.concatenate(mixes, axis=0), wout_ref, npost_ref[...], x_ref[...])


def _ssd_prompt(xbc3, misc3, z3, mix_a3, x3, w):
    b, s, _ = xbc3.shape
    rows = SSD_STEP_CHUNKS * CHUNK
    blk = lambda width: pl.BlockSpec((None, rows, width), lambda i, j: (i, j, 0))
    return pl.pallas_call(
        _ssd_prompt_kernel,
        grid=(b, s // rows),
        in_specs=[blk(CONV_DIM), blk(LANES), blk(D_SSM), blk(D_ATTN), blk(D_MODEL),
                  _full((CONV_W, CONV_DIM)), _full((1, CONV_DIM)),
                  _full((1, LANES)), _full((1, LANES)), _full((LANES, D_SSM)),
                  _full((1, D_SSM)), _full((1, D_SSM)), _full((D_MIX, D_MODEL)), _full((1, D_MODEL))],
        out_specs=(blk(D_MODEL),
                   pl.BlockSpec((None, SSD_GROUPS, D_STATE, GROUP_WIDTH), lambda i, j: (i, 0, 0, 0))),
        out_shape=(jax.ShapeDtypeStruct((b, s, D_MODEL), F32),
                   jax.ShapeDtypeStruct((b, SSD_GROUPS, D_STATE, GROUP_WIDTH), F32)),
        scratch_shapes=[pltpu.VMEM((CONV_PAD + rows, CONV_DIM), F32)],
        compiler_params=pltpu.CompilerParams(dimension_semantics=("arbitrary", "arbitrary"),
                                             vmem_limit_bytes=VMEM_LIMIT_BYTES),
        name="ssd_prompt",
    )(xbc3, misc3, z3, mix_a3, x3, w['conv_w'], w['conv_b'], w['dt_bias_t'], w['a_log_t'], w['expand'],
      w['d_skip_w'], w['ssm_norm'], w['w_out'], w['norm_post'])


def _attn_sample_kernel(layer, pt_ref, qlat_ref, qrope_ref, cnew_ref, rnew_ref, cache_c_ref, cache_rt_ref,
                        o_ref, cbuf, rbuf, sem):
    i = pl.program_id(0)
    n = pl.num_programs(0)
    n_pages = pt_ref.shape[1]
    seqs = qlat_ref.shape[0]
    past = n_pages * PAGE_SIZE
    slot = i % 2

    def page_copies(step, slot_):
        copies = []
        for j in range(seqs):
            buf = slot_ * seqs + j
            for p in range(n_pages):
                page = pt_ref[step * seqs + j, p]
                dst = pl.ds(p * PAGE_SIZE, PAGE_SIZE)
                copies.append(pltpu.make_async_copy(cache_c_ref.at[layer, page], cbuf.at[buf, dst],
                                                    sem.at[0, slot_]))
                copies.append(pltpu.make_async_copy(cache_rt_ref.at[layer, page], rbuf.at[buf, :, dst],
                                                    sem.at[1, slot_]))
        return copies

    @pl.when(i == 0)
    def _():
        for cp in page_copies(0, 0):
            cp.start()

    @pl.when(i + 1 < n)
    def _():
        for cp in page_copies(i + 1, 1 - slot):
            cp.start()

    for cp in page_copies(i, slot):
        cp.wait()

    q_pos = past
    chunks = [pl.ds(c * DEC_KV_CHUNK, DEC_KV_CHUNK) for c in range(past // DEC_KV_CHUNK)]
    for j in range(seqs):
        buf = slot * seqs + j
        qlat = qlat_ref[j]
        qrope = qrope_ref[j]
        cnew = cnew_ref[j]
        rnew = rnew_ref[j]
        s_new = (jnp.sum(qlat * cnew, axis=-1, keepdims=True)
                 + jnp.sum(qrope * rnew, axis=-1, keepdims=True))
        s = jnp.concatenate(
            [lax.dot_general(qlat, cbuf[buf, keys, :], _NT, preferred_element_type=F32)
             + _dot(qrope, rbuf[buf, :, keys]) for keys in chunks], axis=1)
        k_pos = lax.broadcasted_iota(jnp.int32, s.shape, 1)
        s = jnp.where(k_pos <= q_pos, s, NEG_BIG)
        m = jnp.maximum(jnp.max(s, axis=-1, keepdims=True), s_new)
        p = jnp.exp(s - m)
        p_new = jnp.exp(s_new - m)
        l = jnp.sum(p, axis=-1, keepdims=True) + p_new
        acc = p_new * cnew
        for c, keys in enumerate(chunks):
            acc = acc + _dot(p[:, c * DEC_KV_CHUNK:(c + 1) * DEC_KV_CHUNK], cbuf[buf, keys, :])
        o_ref[j] = acc / l


def _attn_sample(layer, page_table, qlat3, qrope3, cnew3, rnew3, cache_c, cache_rt):
    b, n_pages = page_table.shape
    past = n_pages * PAGE_SIZE
    seqs = DEC_ATTN_SEQS
    assert past % DEC_KV_CHUNK == 0 and b % seqs == 0
    per_step = lambda d1, d2: pl.BlockSpec((seqs, d1, d2), lambda i, pt: (i, 0, 0))
    grid_spec = pltpu.PrefetchScalarGridSpec(
        num_scalar_prefetch=1,
        grid=(b // seqs,),
        in_specs=[per_step(MLA_HEADS, KV_RANK), per_step(MLA_HEADS, QK_ROPE), per_step(1, KV_RANK),
                  per_step(1, QK_ROPE), pl.BlockSpec(memory_space=pl.ANY), pl.BlockSpec(memory_space=pl.ANY)],
        out_specs=per_step(MLA_HEADS, KV_RANK),
        scratch_shapes=[pltpu.VMEM((2 * seqs, past, KV_RANK), F32), pltpu.VMEM((2 * seqs, QK_ROPE, past), F32),
                        pltpu.SemaphoreType.DMA((2, 2))],
    )
    return pl.pallas_call(
        functools.partial(_attn_sample_kernel, layer),
        grid_spec=grid_spec,
        out_shape=jax.ShapeDtypeStruct((b, MLA_HEADS, KV_RANK), F32),
        compiler_params=pltpu.CompilerParams(dimension_semantics=("arbitrary",),
                                             vmem_limit_bytes=VMEM_LIMIT_BYTES),
        name="attn_sample",
    )(page_table, qlat3, qrope3, cnew3, rnew3, cache_c, cache_rt)


def _post_sample_kernel(olat_ref, wuv_ref, za_ref, xbc_ref, misc_ref, cprev_ref, cw_ref, cb_ref,
                        dtb_ref, alog_ref, expand_ref,
                        mixa_ref, cnew_ref, act_ref, xdt_t_ref, dec_t_ref):
    for hh in range(MLA_HEADS):
        o = _dot(olat_ref[:, hh * KV_RANK:(hh + 1) * KV_RANK].astype(BF16), wuv_ref[hh])
        vs = slice(hh * V_HEAD, (hh + 1) * V_HEAD)
        mixa_ref[:, vs] = (o * _silu(za_ref[:, vs])).astype(BF16)

    xb = xbc_ref[...]
    acc = cb_ref[...]
    for k in range(CONV_W - 1):
        acc = acc + cprev_ref[:, k * CONV_DIM:(k + 1) * CONV_DIM] * cw_ref[k:k + 1, :]
    acc = acc + xb * cw_ref[CONV_W - 1:CONV_W, :]
    act = _silu(acc)
    act_ref[...] = act
    cnew_ref[:, 0:(CONV_W - 2) * CONV_DIM] = cprev_ref[:, CONV_DIM:(CONV_W - 1) * CONV_DIM]
    cnew_ref[:, (CONV_W - 2) * CONV_DIM:(CONV_W - 1) * CONV_DIM] = xb

    dt, da = _dt_and_da(misc_ref[...], dtb_ref[...], alog_ref[...])
    xdt = act[:, :D_SSM] * _head_expand(dt, expand_ref)
    decay = _head_expand(jnp.exp(da), expand_ref)
    xdt_t_ref[...] = xdt.T
    dec_t_ref[...] = decay.T


def _post_sample(olat2, za, xbc, misc, cprev2, w):
    n = olat2.shape[0]
    full = lambda a: _full(a.shape)
    args = (olat2, w['w_uvh'], za, xbc, misc, cprev2, w['conv_w'], w['conv_b'], w['dt_bias_t'], w['a_log_t'],
            w['expand'])
    out_shape = (
        jax.ShapeDtypeStruct((n, D_ATTN), BF16),
        jax.ShapeDtypeStruct((n, (CONV_W - 1) * CONV_DIM), F32),
        jax.ShapeDtypeStruct((n, CONV_DIM), F32),
        jax.ShapeDtypeStruct((D_SSM, n), F32),
        jax.ShapeDtypeStruct((D_SSM, n), F32),
    )
    return pl.pallas_call(
        _post_sample_kernel,
        grid=(1,),
        in_specs=[full(a) for a in args],
        out_specs=tuple(_full(s.shape) for s in out_shape),
        out_shape=out_shape,
        compiler_params=pltpu.CompilerParams(dimension_semantics=("arbitrary",),
                                             vmem_limit_bytes=VMEM_LIMIT_BYTES),
        name="post_sample",
    )(*args)


def _state_sample_kernel(h0_ref, xdt_t_ref, dec_t_ref, bm_ref, cm_ref, h_ref, y_t_ref):
    t = pl.program_id(0)
    tb = h0_ref.shape[0]
    rows, n_seq = xdt_t_ref.shape

    @pl.when(t == 0)
    def _():
        y_t_ref[...] = jnp.zeros(y_t_ref.shape, F32)

    lane = lax.broadcasted_iota(jnp.int32, (rows, n_seq), 1)
    xdt_t = xdt_t_ref[...]
    dec_t = dec_t_ref[...]
    for j in range(tb):
        own = lane == t * tb + j
        x_col = jnp.sum(jnp.where(own, xdt_t, 0.0), axis=1, keepdims=True)
        d_col = jnp.sum(jnp.where(own, dec_t, 0.0), axis=1, keepdims=True)
        b_rows = jnp.concatenate(
            [jnp.broadcast_to(bm_ref[j:j + 1, g * D_STATE:(g + 1) * D_STATE], (GROUP_WIDTH, D_STATE))
             for g in range(SSD_GROUPS)], axis=0)
        c_rows = jnp.concatenate(
            [jnp.broadcast_to(cm_ref[j:j + 1, g * D_STATE:(g + 1) * D_STATE], (GROUP_WIDTH, D_STATE))
             for g in range(SSD_GROUPS)], axis=0)
        h = d_col * h0_ref[j] + x_col * b_rows
        h_ref[j] = h
        y_col = jnp.sum(h * c_rows, axis=1, keepdims=True)
        y_t_ref[...] = jnp.where(own, y_col, y_t_ref[...])


def _state_sample(h0, xdt_t, dec_t, act):
    n = h0.shape[0]
    tb = DEC_STATE_SEQS
    bc_w = SSD_GROUPS * D_STATE
    st = pl.BlockSpec((tb, D_SSM, D_STATE), lambda i: (i, 0, 0))
    return pl.pallas_call(
        _state_sample_kernel,
        grid=(n // tb,),
        in_specs=[st, _full((D_SSM, n)), _full((D_SSM, n)),
                  pl.BlockSpec((tb, bc_w), lambda i: (i, D_SSM // bc_w)),
                  pl.BlockSpec((tb, bc_w), lambda i: (i, D_SSM // bc_w + 1))],
        out_specs=(st, _full((D_SSM, n))),
        out_shape=(jax.ShapeDtypeStruct(h0.shape, F32), jax.ShapeDtypeStruct((D_SSM, n), F32)),
        compiler_params=pltpu.CompilerParams(dimension_semantics=("arbitrary",),
                                             vmem_limit_bytes=VMEM_LIMIT_BYTES),
        name="state_sample",
    )(h0, xdt_t, dec_t, act, act)


def _finish_sample_kernel(y_t_ref, act_ref, z_ref, dskip_ref, normw_ref, ma_ref, x_ref, wout_ref, npost_ref,
                          y_ref):
    mix_s = _ssd_finish(y_t_ref[...].T, act_ref[...], z_ref[...], dskip_ref[...], normw_ref[...])
    y_ref[...] = _out_proj(ma_ref[...], mix_s, wout_ref, npost_ref[...], x_ref[...])


def _finish_sample(y_t, act, zs, mix_a, x2, w):
    n = x2.shape[0]
    return pl.pallas_call(
        _finish_sample_kernel,
        grid=(1,),
        in_specs=[_full(y_t.shape), pl.BlockSpec((n, D_SSM), lambda i: (0, 0)), _full(zs.shape),
                  _full((1, D_SSM)), _full((1, D_SSM)), _full(mix_a.shape), _full(x2.shape),
                  _full((D_MIX, D_MODEL)), _full((1, D_MODEL))],
        out_specs=_full((n, D_MODEL)),
        out_shape=jax.ShapeDtypeStruct((n, D_MODEL), F32),
        compiler_params=pltpu.CompilerParams(dimension_semantics=("arbitrary",),
                                             vmem_limit_bytes=VMEM_LIMIT_BYTES),
        name="finish_sample",
    )(y_t, act, zs, w['d_skip_w'], w['ssm_norm'], mix_a, x2, w['w_out'], w['norm_post'])


def _rope_tables(pos):
    inv_freq = ROPE_THETA ** (-jnp.arange(ROPE_HALF, dtype=F32) / ROPE_HALF)
    ang = pos.astype(F32)[:, None] * inv_freq[None, :]
    cos, sin = jnp.cos(ang), jnp.sin(ang)
    reps = LANES // QK_ROPE
    return jnp.tile(cos, (1, 2 * reps)), jnp.tile(jnp.concatenate([-sin, sin], axis=1), (1, reps))


def _prep_weights(lw):
    row = lambda v: v.reshape(1, -1).astype(F32)
    w_in = lw['w_in']
    q_a, c_raw, kr, za, zs, xbc, dt = jnp.split(
        w_in, (Q_RANK, Q_RANK + KV_RANK, Q_RANK + KV_RANK + QK_ROPE,
               Q_RANK + KV_RANK + QK_ROPE + D_ATTN, Q_RANK + KV_RANK + QK_ROPE + D_ATTN + D_SSM,
               Q_RANK + KV_RANK + QK_ROPE + D_ATTN + D_SSM + CONV_DIM), axis=1)
    pad = jnp.zeros((D_MODEL, LANES - QK_ROPE - SSD_HEADS), w_in.dtype)
    w_qb = lw['w_q_b'].reshape(Q_RANK, MLA_HEADS, QK_NOPE + QK_ROPE)
    lane_pad = lambda v: jnp.pad(v.reshape(1, -1).astype(F32), ((0, 0), (DT_LANE0, LANES - DT_LANE0 - SSD_HEADS)))
    head_of_col = jnp.arange(D_SSM) // SSD_HEADDIM
    return {
        'norm_pre': row(lw['norm_pre']),
        'w_in': jnp.concatenate([q_a, c_raw, za, zs, xbc, kr, dt, pad], axis=1).astype(BF16),
        'q_a_norm': row(lw['q_a_norm']),
        'w_qb': jnp.concatenate([w_qb[:, :, :QK_NOPE].reshape(Q_RANK, -1),
                                 w_qb[:, :, QK_NOPE:].reshape(Q_RANK, -1)], axis=1).astype(BF16),
        'kv_a_norm': row(lw['kv_a_norm']),
        'w_uk2t': lw['w_uk'].reshape(KV_RANK, MLA_HEADS * QK_NOPE).T.astype(BF16),
        'w_uv2': lw['w_uv'].reshape(KV_RANK, MLA_HEADS * V_HEAD).astype(BF16),
        'w_ukt': jnp.transpose(lw['w_uk'], (1, 2, 0)).astype(BF16),
        'w_uvh': jnp.transpose(lw['w_uv'], (1, 0, 2)).astype(BF16),
        'conv_w': lw['conv_w'].astype(F32),
        'conv_b': row(lw['conv_b']),
        'dt_bias_t': lane_pad(lw['dt_bias']),
        'a_log_t': lane_pad(lw['a_log']),
        'expand': sum((jnp.arange(LANES)[:, None] == DT_LANE0 + k * SSD_HEADS + head_of_col[None, :])
                      for k in range(SPLIT_PIECES)).astype(BF16),
        'd_skip_w': jnp.repeat(lw['d_skip'].astype(F32), SSD_HEADDIM).reshape(1, D_SSM),
        'ssm_norm': row(lw['ssm_norm']),
        'w_out': lw['w_out'].astype(BF16),
        'norm_post': row(lw['norm_post']),
    }


def _prompt_layer(x, w):
    b, s, _ = x.shape
    x2 = x.reshape(b * s, D_MODEL)
    cos, sin = _rope_tables(jnp.arange(s, dtype=jnp.int32))
    q, kt, v, ckv, kr, za, zs, xbc, misc = _proj_prompt(x2, cos, sin, w)
    r3 = lambda a: a.reshape(b, s, a.shape[-1])
    mix_a = _attn_prompt(r3(q), kt, r3(v), r3(za))
    xbc3 = r3(xbc)
    y, state = _ssd_prompt(xbc3, r3(misc), r3(zs), mix_a, x, w)
    h = state.reshape(b, SSD_GROUPS, D_STATE, HEADS_PER_GROUP, SSD_HEADDIM)
    h = jnp.transpose(h, (0, 1, 3, 4, 2)).reshape(b, SSD_HEADS, SSD_HEADDIM, D_STATE)
    return (y.reshape(b, s, D_MODEL), r3(ckv), r3(kr), xbc3[:, s - (CONV_W - 1):, :], h)


def _sample_layer(layer, x, cache_c, cache_r, conv_prev, h0, page_table, w):
    b, s, _ = x.shape
    n = b * s
    past = page_table.shape[1] * PAGE_SIZE
    x2 = x.reshape(n, D_MODEL)
    pos = past + jnp.arange(s, dtype=jnp.int32)
    cos, sin = _rope_tables(jnp.tile(pos, b))
    qlat, qrope, ckv, kr, za, zs, xbc, misc = _proj_sample(x2, cos, sin, w)
    olat = _attn_sample(layer, page_table, qlat.reshape(n, MLA_HEADS, KV_RANK), qrope.reshape(n, MLA_HEADS, QK_ROPE),
                        ckv.reshape(n, 1, KV_RANK), kr.reshape(n, 1, QK_ROPE), cache_c,
                        jnp.swapaxes(cache_r, 2, 3))
    mix_a, conv_new, act, xdt_t, dec_t = _post_sample(
        olat.reshape(n, MLA_HEADS * KV_RANK), za, xbc, misc, conv_prev.reshape(n, (CONV_W - 1) * CONV_DIM), w)
    h, y_t = _state_sample(h0.astype(F32).reshape(n, D_SSM, D_STATE), xdt_t, dec_t, act)
    y = _finish_sample(y_t, act, zs, mix_a, x2, w)
    return (y.reshape(b, s, D_MODEL), ckv.reshape(b, s, KV_RANK), kr.reshape(b, s, QK_ROPE),
            conv_new.reshape(b, CONV_W - 1, CONV_DIM), h.reshape(b, SSD_HEADS, SSD_HEADDIM, D_STATE))


def kernel(x_prompt, x_sample, cache_ckv, cache_krope, state_conv, state_ssm, page_table, norm_pre, w_in,
           q_a_norm, w_q_b, kv_a_norm, w_uk, w_uv, conv_w, conv_b, dt_bias, a_log, d_skip, ssm_norm, w_out,
           norm_post):
    assert x_sample.shape[1] == 1, "the sample path handles one new token per sequence"
    depth = w_in.shape[0]
    y_prompt, y_sample = x_prompt, x_sample
    outs = [[] for _ in range(8)]
    for l in range(depth):
        w = _prep_weights({'norm_pre': norm_pre[l], 'w_in': w_in[l], 'q_a_norm': q_a_norm[l],
                           'w_q_b': w_q_b[l], 'kv_a_norm': kv_a_norm[l], 'w_uk': w_uk[l], 'w_uv': w_uv[l],
                           'conv_w': conv_w[l], 'conv_b': conv_b[l], 'dt_bias': dt_bias[l], 'a_log': a_log[l],
                           'd_skip': d_skip[l], 'ssm_norm': ssm_norm[l], 'w_out': w_out[l],
                           'norm_post': norm_post[l]})
        y_prompt, c1, k1, v1, h1 = _prompt_layer(y_prompt, w)
        y_sample, c2, k2, v2, h2 = _sample_layer(l, y_sample, cache_ckv, cache_krope, state_conv[l],
                                                 state_ssm[l], page_table, w)
        for lst, val in zip(outs, (c1, k1, v1, h1, c2, k2, v2, h2)):
            lst.append(val)
    return (y_prompt, y_sample) + tuple(jnp.stack(o) for o in outs)
```

```python
import functools
import math

import jax
import jax.numpy as jnp
from jax import lax
from jax.experimental import pallas as pl
from jax.experimental.pallas import tpu as pltpu

F32 = jnp.float32
BF16 = jnp.bfloat16

D_MODEL = 1024
PAGE_SIZE = 128
D_MIX = 2 * D_MODEL
D_ATTN = D_MIX // 2
D_SSM = D_MIX - D_ATTN
MLA_HEADS = 8
QK_NOPE = 128
QK_ROPE = 64
ROPE_HALF = QK_ROPE // 2
V_HEAD = D_ATTN // MLA_HEADS
Q_RANK = 384
KV_RANK = 256
ROPE_THETA = 10000.0
SOFTMAX_SCALE = (QK_NOPE + QK_ROPE) ** -0.5
SSD_HEADDIM = 64
SSD_HEADS = D_SSM // SSD_HEADDIM
SSD_GROUPS = 2
HEADS_PER_GROUP = SSD_HEADS // SSD_GROUPS
GROUP_WIDTH = D_SSM // SSD_GROUPS
D_STATE = 128
CONV_W = 4
CONV_DIM = D_SSM + 2 * SSD_GROUPS * D_STATE
CHUNK = 128
EPS = 1e-6
SPLIT_PIECES = 3
NEG_BIG = -1e30

LANES = 128
SUBLANES = 8
VMEM_LIMIT_BYTES = 56 * 1024 * 1024

COL_Q = 0
COL_C = COL_Q + Q_RANK
COL_ZA = COL_C + KV_RANK
COL_ZS = COL_ZA + D_ATTN
COL_XBC = COL_ZS + D_SSM
COL_MISC = COL_XBC + CONV_DIM
D_IN_PAD = COL_MISC + LANES
DT_LANE0 = QK_ROPE
QK_PAD = 2 * LANES

PROJ_ROWS = 256
ATTN_TQ = 256
ATTN_HEADS = 2
SSD_STEP_CHUNKS = 2
DEC_STATE_SEQS = 8
DEC_KV_CHUNK = 2048
DEC_ATTN_SEQS = 2

_NT = (((1,), (1,)), ((), ()))


def _rms(x, w):
    return x * lax.rsqrt(jnp.mean(x * x, axis=-1, keepdims=True) + EPS) * w


def _silu(x):
    return x / (1.0 + jnp.exp(-x))


def _softplus(x):
    return jnp.maximum(x, 0.0) + jnp.log(1.0 + jnp.exp(-jnp.abs(x)))


def _dot(a, b):
    return jnp.dot(a, b, preferred_element_type=F32)


def _rope_tile(x, cos, sin_signed):
    lane = lax.broadcasted_iota(jnp.int32, x.shape, 1)
    first_half = (lane % QK_ROPE) < ROPE_HALF
    partner = jnp.where(first_half,
                        pltpu.roll(x, LANES - ROPE_HALF, 1),
                        pltpu.roll(x, ROPE_HALF, 1))
    return x * cos + partner * sin_signed


def _front(x_ref, npre_ref, win_ref):
    h = _rms(x_ref[...], npre_ref[...]).astype(BF16)

    def seg(lo, hi):
        return _dot(h, win_ref[:, lo:hi])

    return seg


def _proj_prompt_kernel(x_ref, cos_ref, sin_ref, npre_ref, win_ref, qan_ref, wqb_ref, kvn_ref,
                        wukt_ref, wuv_ref,
                        q_ref, kt_ref, v_ref, ckv_ref, kr_ref, za_ref, zs_ref, xbc_ref, misc_ref):
    seg = _front(x_ref, npre_ref, win_ref)
    misc = seg(COL_MISC, D_IN_PAD)
    misc_ref[...] = misc
    cos = cos_ref[...]
    sin = sin_ref[...]
    lane = lax.broadcasted_iota(jnp.int32, misc.shape, 1)
    low = lane < QK_ROPE

    kr_full = _rope_tile(misc, cos, sin)
    kr_ref[...] = kr_full[:, :QK_ROPE]
    kr_lo_t = jnp.where(low, kr_full, 0.0).T.astype(BF16)
    kr_hi_t = jnp.where(low, 0.0, pltpu.roll(kr_full, QK_ROPE, 1)).T.astype(BF16)

    c_raw = seg(COL_C, COL_ZA)
    q_a = seg(COL_Q, COL_C)
    za_ref[...] = seg(COL_ZA, COL_ZS)
    ckv = _rms(c_raw, kvn_ref[...])
    ckv_ref[...] = ckv
    cb = ckv.astype(BF16)
    knope_t = lax.dot_general(wukt_ref[...], cb, _NT, preferred_element_type=F32)
    v_ref[...] = _dot(cb, wuv_ref[...]).astype(BF16)

    qn = _rms(q_a, qan_ref[...]).astype(BF16)
    q = _dot(qn, wqb_ref[...])
    nope_w = MLA_HEADS * QK_NOPE
    for hh in range(MLA_HEADS):
        pair = hh // 2
        r = _rope_tile(q[:, nope_w + pair * LANES: nope_w + (pair + 1) * LANES], cos, sin)
        own = low if hh % 2 == 0 else jnp.logical_not(low)
        base = hh * QK_PAD
        q_ref[:, base:base + LANES] = (q[:, hh * QK_NOPE:(hh + 1) * QK_NOPE] * SOFTMAX_SCALE).astype(BF16)
        q_ref[:, base + LANES:base + QK_PAD] = (jnp.where(own, r, 0.0) * SOFTMAX_SCALE).astype(BF16)
        kt_ref[base:base + LANES, :] = knope_t[hh * QK_NOPE:(hh + 1) * QK_NOPE, :].astype(BF16)
        kt_ref[base + LANES:base + QK_PAD, :] = kr_lo_t if hh % 2 == 0 else kr_hi_t
    zs_ref[...] = seg(COL_ZS, COL_XBC)
    xbc_ref[...] = seg(COL_XBC, COL_MISC)


def _proj_sample_kernel(x_ref, cos_ref, sin_ref, npre_ref, win_ref, qan_ref, wqb_ref, kvn_ref, wukt_ref,
                        qlat_ref, qrope_ref, ckv_ref, kr_ref, za_ref, zs_ref, xbc_ref, misc_ref):
    seg = _front(x_ref, npre_ref, win_ref)
    za_ref[...] = seg(COL_ZA, COL_ZS)
    zs_ref[...] = seg(COL_ZS, COL_XBC)
    xbc_ref[...] = seg(COL_XBC, COL_MISC)
    misc = seg(COL_MISC, D_IN_PAD)
    misc_ref[...] = misc
    cos = cos_ref[...]
    sin = sin_ref[...]
    kr_ref[...] = _rope_tile(misc, cos, sin)[:, :QK_ROPE]
    ckv_ref[...] = _rms(seg(COL_C, COL_ZA), kvn_ref[...])

    qn = _rms(seg(COL_Q, COL_C), qan_ref[...]).astype(BF16)
    q = _dot(qn, wqb_ref[...])
    nope_w = MLA_HEADS * QK_NOPE
    for pair in range(MLA_HEADS // 2):
        lo = nope_w + pair * LANES
        qrope_ref[:, pair * LANES:(pair + 1) * LANES] = _rope_tile(q[:, lo:lo + LANES], cos, sin) * SOFTMAX_SCALE
    for hh in range(MLA_HEADS):
        qh = q[:, hh * QK_NOPE:(hh + 1) * QK_NOPE].astype(BF16)
        qlat_ref[:, hh * KV_RANK:(hh + 1) * KV_RANK] = _dot(qh, wukt_ref[hh]) * SOFTMAX_SCALE


def _full(shape):
    return pl.BlockSpec(shape, lambda *_: (0,) * len(shape))


def _proj_prompt(x2, cos, sin, w):
    n = x2.shape[0]
    tm = PROJ_ROWS
    seq_tiles = cos.shape[0] // tm
    rows = lambda width: pl.BlockSpec((tm, width), lambda i: (i, 0))
    tab = pl.BlockSpec((tm, LANES), lambda i: (i % seq_tiles, 0))
    out_shape = (
        jax.ShapeDtypeStruct((n, MLA_HEADS * QK_PAD), BF16),
        jax.ShapeDtypeStruct((MLA_HEADS * QK_PAD, n), BF16),
        jax.ShapeDtypeStruct((n, D_ATTN), BF16),
        jax.ShapeDtypeStruct((n, KV_RANK), F32),
        jax.ShapeDtypeStruct((n, QK_ROPE), F32),
        jax.ShapeDtypeStruct((n, D_ATTN), F32),
        jax.ShapeDtypeStruct((n, D_SSM), F32),
        jax.ShapeDtypeStruct((n, CONV_DIM), F32),
        jax.ShapeDtypeStruct((n, LANES), F32),
    )
    return pl.pallas_call(
        _proj_prompt_kernel,
        grid=(n // tm,),
        in_specs=[rows(D_MODEL), tab, tab, _full((1, D_MODEL)), _full((D_MODEL, D_IN_PAD)),
                  _full((1, Q_RANK)), _full(w['w_qb'].shape), _full((1, KV_RANK)),
                  _full(w['w_uk2t'].shape), _full(w['w_uv2'].shape)],
        out_specs=tuple(pl.BlockSpec((s.shape[0], tm), lambda i: (0, i)) if k == 1 else rows(s.shape[1])
                        for k, s in enumerate(out_shape)),
        out_shape=out_shape,
        compiler_params=pltpu.CompilerParams(dimension_semantics=("arbitrary",),
                                             vmem_limit_bytes=VMEM_LIMIT_BYTES),
        name="proj_prompt",
    )(x2, cos, sin, w['norm_pre'], w['w_in'], w['q_a_norm'], w['w_qb'], w['kv_a_norm'],
      w['w_uk2t'], w['w_uv2'])


def _proj_sample(x2, cos, sin, w):
    n = x2.shape[0]
    rows = lambda width: pl.BlockSpec((n, width), lambda i: (0, 0))
    out_shape = (
        jax.ShapeDtypeStruct((n, MLA_HEADS * KV_RANK), F32),
        jax.ShapeDtypeStruct((n, MLA_HEADS * QK_ROPE), F32),
        jax.ShapeDtypeStruct((n, KV_RANK), F32),
        jax.ShapeDtypeStruct((n, QK_ROPE), F32),
        jax.ShapeDtypeStruct((n, D_ATTN), F32),
        jax.ShapeDtypeStruct((n, D_SSM), F32),
        jax.ShapeDtypeStruct((n, CONV_DIM), F32),
        jax.ShapeDtypeStruct((n, LANES), F32),
    )
    return pl.pallas_call(
        _proj_sample_kernel,
        grid=(1,),
        in_specs=[rows(D_MODEL), rows(LANES), rows(LANES), _full((1, D_MODEL)), _full((D_MODEL, D_IN_PAD)),
                  _full((1, Q_RANK)), _full(w['w_qb'].shape), _full((1, KV_RANK)),
                  _full(w['w_ukt'].shape)],
        out_specs=tuple(rows(s.shape[1]) for s in out_shape),
        out_shape=out_shape,
        compiler_params=pltpu.CompilerParams(dimension_semantics=("arbitrary",),
                                             vmem_limit_bytes=VMEM_LIMIT_BYTES),
        name="proj_sample",
    )(x2, cos, sin, w['norm_pre'], w['w_in'], w['q_a_norm'], w['w_qb'], w['kv_a_norm'], w['w_ukt'])


def _attn_prompt_kernel(q_ref, kt_ref, v_ref, z_ref, o_ref):
    seq = q_ref.shape[0]
    tq = ATTN_TQ
    row = lax.broadcasted_iota(jnp.int32, (tq, tq), 0)
    col = lax.broadcasted_iota(jnp.int32, (tq, tq), 1)
    causal = col <= row
    for qi in reversed(range(seq // tq)):
        lim = (qi + 1) * tq
        ones_col = (lax.broadcasted_iota(jnp.int32, (lim, LANES), 1) == 0).astype(BF16)
        for hd in range(ATTN_HEADS):
            qk = slice(hd * QK_PAD, (hd + 1) * QK_PAD)
            hv = slice(hd * V_HEAD, (hd + 1) * V_HEAD)
            s = _dot(q_ref[qi * tq:lim, qk], kt_ref[qk, 0:lim])
            diag = jnp.where(causal, s[:, lim - tq:], NEG_BIG)
            s = diag if qi == 0 else jnp.concatenate([s[:, :lim - tq], diag], axis=1)
            m = jnp.max(s, axis=-1, keepdims=True)
            p = jnp.exp(s - m).astype(BF16)
            ol = _dot(p, jnp.concatenate([v_ref[0:lim, hv], ones_col], axis=1))
            o = ol[:, :V_HEAD] / jnp.sum(ol[:, V_HEAD:], axis=-1, keepdims=True)
            o_ref[qi * tq:lim, hv] = (o * _silu(z_ref[qi * tq:lim, hv])).astype(BF16)


def _attn_prompt(q3, kt, v3, z3):
    b, s, _ = q3.shape
    nh = ATTN_HEADS
    hv = pl.BlockSpec((None, s, nh * V_HEAD), lambda i, j: (i, 0, j))
    return pl.pallas_call(
        _attn_prompt_kernel,
        grid=(b, MLA_HEADS // nh),
        in_specs=[pl.BlockSpec((None, s, nh * QK_PAD), lambda i, j: (i, 0, j)),
                  pl.BlockSpec((nh * QK_PAD, s), lambda i, j: (j, i)), hv, hv],
        out_specs=hv,
        out_shape=jax.ShapeDtypeStruct((b, s, D_ATTN), BF16),
        compiler_params=pltpu.CompilerParams(dimension_semantics=("arbitrary", "arbitrary"),
                                             vmem_limit_bytes=VMEM_LIMIT_BYTES),
        name="attn_prompt",
    )(q3, kt, v3, z3)


def _dt_lanes(shape):
    lane = lax.broadcasted_iota(jnp.int32, shape, 1)
    return jnp.logical_and(lane >= DT_LANE0, lane < DT_LANE0 + SSD_HEADS)


def _dt_and_da(misc, dtb, alog):
    dt = jnp.where(_dt_lanes(misc.shape), _softplus(misc + dtb), 0.0)
    return dt, dt * (-jnp.exp(alog))


def _split_bf16(x):
    pieces = []
    for _ in range(SPLIT_PIECES):
        piece = x.astype(BF16).astype(F32)
        pieces.append(piece)
        x = x - piece
    return pieces


def _cumsum_rows(lower_b, x):
    return sum(_dot(lower_b, piece.astype(BF16)) for piece in _split_bf16(x))


def _head_expand(x, expand_ref):
    pieces = _split_bf16(jnp.where(_dt_lanes(x.shape), x, 0.0))
    packed = pieces[0]
    for k in range(1, SPLIT_PIECES):
        packed = packed + pltpu.roll(pieces[k], k * SSD_HEADS, 1)
    return _dot(packed.astype(BF16), expand_ref[...])


def _ssd_finish(y, xs, z, dskip, normw):
    y = y + dskip * xs
    gated = y * _silu(z)
    outs = []
    for g in range(SSD_GROUPS):
        sl = slice(g * GROUP_WIDTH, (g + 1) * GROUP_WIDTH)
        outs.append(_rms(gated[:, sl], normw[:, sl]))
    return jnp.concatenate(outs, axis=1).astype(BF16)


def _out_proj_attn(mix_a, wout_ref):
    return _dot(mix_a, wout_ref[0:D_ATTN, :])


def _out_proj(o_attn, mix_s, wout_ref, npost, x):
    o = o_attn + _dot(mix_s, wout_ref[D_ATTN:D_MIX, :])
    return x + _rms(o, npost)


CONV_PAD = SUBLANES


def _ssd_chunk(act, misc, dtb_ref, alog_ref, expand_ref, state_ref):
    xs = act[:, :D_SSM]
    dt, da = _dt_and_da(misc, dtb_ref[...], alog_ref[...])
    row = lax.broadcasted_iota(jnp.int32, (CHUNK, CHUNK), 0)
    col = lax.broadcasted_iota(jnp.int32, (CHUNK, CHUNK), 1)
    lower = row >= col
    a_cum = _cumsum_rows(lower.astype(BF16), da)
    a_cum_t = a_cum.T
    a_last = a_cum[CHUNK - 1:CHUNK, :]
    xdt = xs * _head_expand(dt, expand_ref)
    decay_out = _head_expand(jnp.exp(a_cum), expand_ref)
    state_decay = decay_out[CHUNK - 1:CHUNK, :]
    xdt_end = (xdt * _head_expand(jnp.exp(a_last - a_cum), expand_ref)).astype(BF16)
    xdt_b = xdt.astype(BF16)
    lane_w = lax.broadcasted_iota(jnp.int32, (CHUNK, LANES), 1)
    first_head = lane_w < SSD_HEADDIM

    ys = []
    for g in range(SSD_GROUPS):
        bm = act[:, D_SSM + g * D_STATE:D_SSM + (g + 1) * D_STATE]
        cm = act[:, D_SSM + (SSD_GROUPS + g) * D_STATE:D_SSM + (SSD_GROUPS + g + 1) * D_STATE]
        bm_b = bm.astype(BF16)
        cm_b = cm.astype(BF16)
        cb = lax.dot_general(cm_b, bm_b, _NT, preferred_element_type=F32)
        gsl = slice(g * GROUP_WIDTH, (g + 1) * GROUP_WIDTH)
        state = state_ref[g]
        y_off = _dot(cm_b, state.astype(BF16)) * decay_out[:, gsl]
        y_diag = []
        for pair in range(HEADS_PER_GROUP // 2):
            halves = []
            x_pair = xdt_b[:, g * GROUP_WIDTH + pair * LANES: g * GROUP_WIDTH + (pair + 1) * LANES]
            for k in range(2):
                lane_h = DT_LANE0 + g * HEADS_PER_GROUP + 2 * pair + k
                seg = a_cum[:, lane_h:lane_h + 1] - a_cum_t[lane_h:lane_h + 1, :]
                decay = jnp.exp(jnp.where(lower, seg, NEG_BIG))
                halves.append(_dot((cb * decay).astype(BF16), x_pair))
            y_diag.append(jnp.where(first_head, halves[0], halves[1]))
        ys.append(jnp.concatenate(y_diag, axis=1) + y_off)
        state_ref[g] = state * state_decay[:, gsl] + _dot(bm.T.astype(BF16), xdt_end[:, gsl])
    return jnp.concatenate(ys, axis=1)


def _ssd_prompt_kernel(xbc_ref, misc_ref, z_ref, mixa_ref, x_ref, cw_ref, cb_ref, dtb_ref, alog_ref, expand_ref,
                       dskip_ref, normw_ref, wout_ref, npost_ref, y_ref, state_ref, xp_ref):
    c = pl.program_id(1)
    rows = xbc_ref.shape[0]

    @pl.when(c == 0)
    def _():
        xp_ref[0:CONV_PAD, :] = jnp.zeros((CONV_PAD, CONV_DIM), F32)
        state_ref[...] = jnp.zeros(state_ref.shape, F32)

    y_ref[...] = _out_proj_attn(mixa_ref[...], wout_ref)
    xp_ref[CONV_PAD:CONV_PAD + rows, :] = xbc_ref[...]
    xp = xp_ref[...]
    acc = cb_ref[...]
    for tap in range(CONV_W):
        back = CONV_W - 1 - tap
        shifted = xp if back == 0 else pltpu.roll(xp, back, 0)
        acc = acc + shifted[CONV_PAD:CONV_PAD + rows, :] * cw_ref[tap:tap + 1, :]
    act_all = _silu(acc)
    mixes = []
    for k in range(rows // CHUNK):
        r0 = k * CHUNK
        act = act_all[r0:r0 + CHUNK, :]
        y = _ssd_chunk(act, misc_ref[r0:r0 + CHUNK, :], dtb_ref, alog_ref, expand_ref, state_ref)
        mixes.append(_ssd_finish(y, act[:, :D_SSM], z_ref[r0:r0 + CHUNK, :], dskip_ref[...], normw_ref[...]))
    xp_ref[CONV_PAD - 3:CONV_PAD, :] = xp_ref[CONV_PAD + rows - 3:CONV_PAD + rows, :]
    y_ref[...] = _out_proj(y_ref[...], jnp.concatenate(mixes, axis=0), wout_ref, npost_ref[...], x_ref[...])


def _ssd_prompt(xbc3, misc3, z3, mix_a3, x3, w):
    b, s, _ = xbc3.shape
    rows = SSD_STEP_CHUNKS * CHUNK
    blk = lambda width: pl.BlockSpec((None, rows, width), lambda i, j: (i, j, 0))
    return pl.pallas_call(
        _ssd_prompt_kernel,
        grid=(b, s // rows),
        in_specs=[blk(CONV_DIM), blk(LANES), blk(D_SSM), blk(D_ATTN), blk(D_MODEL),
                  _full((CONV_W, CONV_DIM)), _full((1, CONV_DIM)),
                  _full((1, LANES)), _full((1, LANES)), _full((LANES, D_SSM)),
                  _full((1, D_SSM)), _full((1, D_SSM)), _full((D_MIX, D_MODEL)), _full((1, D_MODEL))],
        out_specs=(blk(D_MODEL),
                   pl.BlockSpec((None, SSD_GROUPS, D_STATE, GROUP_WIDTH), lambda i, j: (i, 0, 0, 0))),
        out_shape=(jax.ShapeDtypeStruct((b, s, D_MODEL), F32),
                   jax.ShapeDtypeStruct((b, SSD_GROUPS, D_STATE, GROUP_WIDTH), F32)),
        scratch_shapes=[pltpu.VMEM((CONV_PAD + rows, CONV_DIM), F32)],
        compiler_params=pltpu.CompilerParams(dimension_semantics=("arbitrary", "arbitrary"),
                                             vmem_limit_bytes=VMEM_LIMIT_BYTES),
        name="ssd_prompt",
    )(xbc3, misc3, z3, mix_a3, x3, w['conv_w'], w['conv_b'], w['dt_bias_t'], w['a_log_t'], w['expand'],
      w['d_skip_w'], w['ssm_norm'], w['w_out'], w['norm_post'])


def _attn_sample_kernel(layer, pt_ref, qlat_ref, qrope_ref, cnew_ref, rnew_ref, cache_c_ref, cache_rt_ref,
                        o_ref, cbuf, rbuf, sem):
    i = pl.program_id(0)
    n = pl.num_programs(0)
    n_pages = pt_ref.shape[1]
    seqs = qlat_ref.shape[0]
    past = n_pages * PAGE_SIZE
    slot = i % 2

    def page_copies(step, slot_):
        copies = []
        for j in range(seqs):
            buf = slot_ * seqs + j
            for p in range(n_pages):
                page = pt_ref[step * seqs + j, p]
                dst = pl.ds(p * PAGE_SIZE, PAGE_SIZE)
                copies.append(pltpu.make_async_copy(cache_c_ref.at[layer, page], cbuf.at[buf, dst],
                                                    sem.at[0, slot_]))
                copies.append(pltpu.make_async_copy(cache_rt_ref.at[layer, page], rbuf.at[buf, :, dst],
                                                    sem.at[1, slot_]))
        return copies

    @pl.when(i == 0)
    def _():
        for cp in page_copies(0, 0):
            cp.start()

    @pl.when(i + 1 < n)
    def _():
        for cp in page_copies(i + 1, 1 - slot):
            cp.start()

    for cp in page_copies(i, slot):
        cp.wait()

    q_pos = past
    chunks = [pl.ds(c * DEC_KV_CHUNK, DEC_KV_CHUNK) for c in range(past // DEC_KV_CHUNK)]
    for j in range(seqs):
        buf = slot * seqs + j
        qlat = qlat_ref[j]
        qrope = qrope_ref[j]
        cnew = cnew_ref[j]
        rnew = rnew_ref[j]
        s_new = (jnp.sum(qlat * cnew, axis=-1, keepdims=True)
                 + jnp.sum(qrope * rnew, axis=-1, keepdims=True))
        s = jnp.concatenate(
            [lax.dot_general(qlat, cbuf[buf, keys, :], _NT, preferred_element_type=F32)
             + _dot(qrope, rbuf[buf, :, keys]) for keys in chunks], axis=1)
        k_pos = lax.broadcasted_iota(jnp.int32, s.shape, 1)
        s = jnp.where(k_pos <= q_pos, s, NEG_BIG)
        m = jnp.maximum(jnp.max(s, axis=-1, keepdims=True), s_new)
        p = jnp.exp(s - m)
        p_new = jnp.exp(s_new - m)
        l = jnp.sum(p, axis=-1, keepdims=True) + p_new
        acc = p_new * cnew
        for c, keys in enumerate(chunks):
            acc = acc + _dot(p[:, c * DEC_KV_CHUNK:(c + 1) * DEC_KV_CHUNK], cbuf[buf, keys, :])
        o_ref[j] = acc / l


def _attn_sample(layer, page_table, qlat3, qrope3, cnew3, rnew3, cache_c, cache_rt):
    b, n_pages = page_table.shape
    past = n_pages * PAGE_SIZE
    seqs = DEC_ATTN_SEQS
    assert past % DEC_KV_CHUNK == 0 and b % seqs == 0
    per_step = lambda d1, d2: pl.BlockSpec((seqs, d1, d2), lambda i, pt: (i, 0, 0))
    grid_spec = pltpu.PrefetchScalarGridSpec(
        num_scalar_prefetch=1,
        grid=(b // seqs,),
        in_specs=[per_step(MLA_HEADS, KV_RANK), per_step(MLA_HEADS, QK_ROPE), per_step(1, KV_RANK),
                  per_step(1, QK_ROPE), pl.BlockSpec(memory_space=pl.ANY), pl.BlockSpec(memory_space=pl.ANY)],
        out_specs=per_step(MLA_HEADS, KV_RANK),
        scratch_shapes=[pltpu.VMEM((2 * seqs, past, KV_RANK), F32), pltpu.VMEM((2 * seqs, QK_ROPE, past), F32),
                        pltpu.SemaphoreType.DMA((2, 2))],
    )
    return pl.pallas_call(
        functools.partial(_attn_sample_kernel, layer),
        grid_spec=grid_spec,
        out_shape=jax.ShapeDtypeStruct((b, MLA_HEADS, KV_RANK), F32),
        compiler_params=pltpu.CompilerParams(dimension_semantics=("arbitrary",),
                                             vmem_limit_bytes=VMEM_LIMIT_BYTES),
        name="attn_sample",
    )(page_table, qlat3, qrope3, cnew3, rnew3, cache_c, cache_rt)


def _post_sample_kernel(olat_ref, wuv_ref, za_ref, xbc_ref, misc_ref, cprev_ref, cw_ref, cb_ref,
                        dtb_ref, alog_ref, expand_ref,
                        mixa_ref, cnew_ref, act_ref, xdt_t_ref, dec_t_ref):
    for hh in range(MLA_HEADS):
        o = _dot(olat_ref[:, hh * KV_RANK:(hh + 1) * KV_RANK].astype(BF16), wuv_ref[hh])
        vs = slice(hh * V_HEAD, (hh + 1) * V_HEAD)
        mixa_ref[:, vs] = (o * _silu(za_ref[:, vs])).astype(BF16)

    xb = xbc_ref[...]
    acc = cb_ref[...]
    for k in range(CONV_W - 1):
        acc = acc + cprev_ref[:, k * CONV_DIM:(k + 1) * CONV_DIM] * cw_ref[k:k + 1, :]
    acc = acc + xb * cw_ref[CONV_W - 1:CONV_W, :]
    act = _silu(acc)
    act_ref[...] = act
    cnew_ref[:, 0:(CONV_W - 2) * CONV_DIM] = cprev_ref[:, CONV_DIM:(CONV_W - 1) * CONV_DIM]
    cnew_ref[:, (CONV_W - 2) * CONV_DIM:(CONV_W - 1) * CONV_DIM] = xb

    dt, da = _dt_and_da(misc_ref[...], dtb_ref[...], alog_ref[...])
    xdt = act[:, :D_SSM] * _head_expand(dt, expand_ref)
    decay = _head_expand(jnp.exp(da), expand_ref)
    xdt_t_ref[...] = xdt.T
    dec_t_ref[...] = decay.T


def _post_sample(olat2, za, xbc, misc, cprev2, w):
    n = olat2.shape[0]
    full = lambda a: _full(a.shape)
    args = (olat2, w['w_uvh'], za, xbc, misc, cprev2, w['conv_w'], w['conv_b'], w['dt_bias_t'], w['a_log_t'],
            w['expand'])
    out_shape = (
        jax.ShapeDtypeStruct((n, D_ATTN), BF16),
        jax.ShapeDtypeStruct((n, (CONV_W - 1) * CONV_DIM), F32),
        jax.ShapeDtypeStruct((n, CONV_DIM), F32),
        jax.ShapeDtypeStruct((D_SSM, n), F32),
        jax.ShapeDtypeStruct((D_SSM, n), F32),
    )
    return pl.pallas_call(
        _post_sample_kernel,
        grid=(1,),
        in_specs=[full(a) for a in args],
        out_specs=tuple(_full(s.shape) for s in out_shape),
        out_shape=out_shape,
        compiler_params=pltpu.CompilerParams(dimension_semantics=("arbitrary",),
                                             vmem_limit_bytes=VMEM_LIMIT_BYTES),
        name="post_sample",
    )(*args)


def _state_sample_kernel(h0_ref, xdt_t_ref, dec_t_ref, bm_ref, cm_ref, h_ref, y_t_ref):
    t = pl.program_id(0)
    tb = h0_ref.shape[0]
    rows, n_seq = xdt_t_ref.shape

    @pl.when(t == 0)
    def _():
        y_t_ref[...] = jnp.zeros(y_t_ref.shape, F32)

    lane = lax.broadcasted_iota(jnp.int32, (rows, n_seq), 1)
    xdt_t = xdt_t_ref[...]
    dec_t = dec_t_ref[...]
    for j in range(tb):
        own = lane == t * tb + j
        x_col = jnp.sum(jnp.where(own, xdt_t, 0.0), axis=1, keepdims=True)
        d_col = jnp.sum(jnp.where(own, dec_t, 0.0), axis=1, keepdims=True)
        b_rows = jnp.concatenate(
            [jnp.broadcast_to(bm_ref[j:j + 1, g * D_STATE:(g + 1) * D_STATE], (GROUP_WIDTH, D_STATE))
             for g in range(SSD_GROUPS)], axis=0)
        c_rows = jnp.concatenate(
            [jnp.broadcast_to(cm_ref[j:j + 1, g * D_STATE:(g + 1) * D_STATE], (GROUP_WIDTH, D_STATE))
             for g in range(SSD_GROUPS)], axis=0)
        h = d_col * h0_ref[j] + x_col * b_rows
        h_ref[j] = h
        y_col = jnp.sum(h * c_rows, axis=1, keepdims=True)
        y_t_ref[...] = jnp.where(own, y_col, y_t_ref[...])


def _state_sample(h0, xdt_t, dec_t, act):
    n = h0.shape[0]
    tb = DEC_STATE_SEQS
    bc_w = SSD_GROUPS * D_STATE
    st = pl.BlockSpec((tb, D_SSM, D_STATE), lambda i: (i, 0, 0))
    return pl.pallas_call(
        _state_sample_kernel,
        grid=(n // tb,),
        in_specs=[st, _full((D_SSM, n)), _full((D_SSM, n)),
                  pl.BlockSpec((tb, bc_w), lambda i: (i, D_SSM // bc_w)),
                  pl.BlockSpec((tb, bc_w), lambda i: (i, D_SSM // bc_w + 1))],
        out_specs=(st, _full((D_SSM, n))),
        out_shape=(jax.ShapeDtypeStruct(h0.shape, F32), jax.ShapeDtypeStruct((D_SSM, n), F32)),
        compiler_params=pltpu.CompilerParams(dimension_semantics=("arbitrary",),
                                             vmem_limit_bytes=VMEM_LIMIT_BYTES),
        name="state_sample",
    )(h0, xdt_t, dec_t, act, act)


def _finish_sample_kernel(y_t_ref, act_ref, z_ref, dskip_ref, normw_ref, ma_ref, x_ref, wout_ref, npost_ref,
                          y_ref):
    mix_s = _ssd_finish(y_t_ref[...].T, act_ref[...], z_ref[...], dskip_ref[...], normw_ref[...])
    y_ref[...] = _out_proj(_out_proj_attn(ma_ref[...], wout_ref), mix_s, wout_ref, npost_ref[...], x_ref[...])


def _finish_sample(y_t, act, zs, mix_a, x2, w):
    n = x2.shape[0]
    return pl.pallas_call(
        _finish_sample_kernel,
        grid=(1,),
        in_specs=[_full(y_t.shape), pl.BlockSpec((n, D_SSM), lambda i: (0, 0)), _full(zs.shape),
                  _full((1, D_SSM)), _full((1, D_SSM)), _full(mix_a.shape), _full(x2.shape),
                  _full((D_MIX, D_MODEL)), _full((1, D_MODEL))],
        out_specs=_full((n, D_MODEL)),
        out_shape=jax.ShapeDtypeStruct((n, D_MODEL), F32),
        compiler_params=pltpu.CompilerParams(dimension_semantics=("arbitrary",),
                                             vmem_limit_bytes=VMEM_LIMIT_BYTES),
        name="finish_sample",
    )(y_t, act, zs, w['d_skip_w'], w['ssm_norm'], mix_a, x2, w['w_out'], w['norm_post'])


def _rope_tables(pos):
    inv_freq = ROPE_THETA ** (-jnp.arange(ROPE_HALF, dtype=F32) / ROPE_HALF)
    ang = pos.astype(F32)[:, None] * inv_freq[None, :]
    cos, sin = jnp.cos(ang), jnp.sin(ang)
    reps = LANES // QK_ROPE
    return jnp.tile(cos, (1, 2 * reps)), jnp.tile(jnp.concatenate([-sin, sin], axis=1), (1, reps))


def _prep_weights(lw):
    row = lambda v: v.reshape(1, -1).astype(F32)
    w_in = lw['w_in']
    q_a, c_raw, kr, za, zs, xbc, dt = jnp.split(
        w_in, (Q_RANK, Q_RANK + KV_RANK, Q_RANK + KV_RANK + QK_ROPE,
               Q_RANK + KV_RANK + QK_ROPE + D_ATTN, Q_RANK + KV_RANK + QK_ROPE + D_ATTN + D_SSM,
               Q_RANK + KV_RANK + QK_ROPE + D_ATTN + D_SSM + CONV_DIM), axis=1)
    pad = jnp.zeros((D_MODEL, LANES - QK_ROPE - SSD_HEADS), w_in.dtype)
    w_qb = lw['w_q_b'].reshape(Q_RANK, MLA_HEADS, QK_NOPE + QK_ROPE)
    lane_pad = lambda v: jnp.pad(v.reshape(1, -1).astype(F32), ((0, 0), (DT_LANE0, LANES - DT_LANE0 - SSD_HEADS)))
    head_of_col = jnp.arange(D_SSM) // SSD_HEADDIM
    return {
        'norm_pre': row(lw['norm_pre']),
        'w_in': jnp.concatenate([q_a, c_raw, za, zs, xbc, kr, dt, pad], axis=1).astype(BF16),
        'q_a_norm': row(lw['q_a_norm']),
        'w_qb': jnp.concatenate([w_qb[:, :, :QK_NOPE].reshape(Q_RANK, -1),
                                 w_qb[:, :, QK_NOPE:].reshape(Q_RANK, -1)], axis=1).astype(BF16),
        'kv_a_norm': row(lw['kv_a_norm']),
        'w_uk2t': lw['w_uk'].reshape(KV_RANK, MLA_HEADS * QK_NOPE).T.astype(BF16),
        'w_uv2': lw['w_uv'].reshape(KV_RANK, MLA_HEADS * V_HEAD).astype(BF16),
        'w_ukt': jnp.transpose(lw['w_uk'], (1, 2, 0)).astype(BF16),
        'w_uvh': jnp.transpose(lw['w_uv'], (1, 0, 2)).astype(BF16),
        'conv_w': lw['conv_w'].astype(F32),
        'conv_b': row(lw['conv_b']),
        'dt_bias_t': lane_pad(lw['dt_bias']),
        'a_log_t': lane_pad(lw['a_log']),
        'expand': sum((jnp.arange(LANES)[:, None] == DT_LANE0 + k * SSD_HEADS + head_of_col[None, :])
                      for k in range(SPLIT_PIECES)).astype(BF16),
        'd_skip_w': jnp.repeat(lw['d_skip'].astype(F32), SSD_HEADDIM).reshape(1, D_SSM),
        'ssm_norm': row(lw['ssm_norm']),
        'w_out': lw['w_out'].astype(BF16),
        'norm_post': row(lw['norm_post']),
    }


def _prompt_layer(x, w):
    b, s, _ = x.shape
    x2 = x.reshape(b * s, D_MODEL)
    cos, sin = _rope_tables(jnp.arange(s, dtype=jnp.int32))
    q, kt, v, ckv, kr, za, zs, xbc, misc = _proj_prompt(x2, cos, sin, w)
    r3 = lambda a: a.reshape(b, s, a.shape[-1])
    mix_a = _attn_prompt(r3(q), kt, r3(v), r3(za))
    xbc3 = r3(xbc)
    y, state = _ssd_prompt(xbc3, r3(misc), r3(zs), mix_a, x, w)
    h = state.reshape(b, SSD_GROUPS, D_STATE, HEADS_PER_GROUP, SSD_HEADDIM)
    h = jnp.transpose(h, (0, 1, 3, 4, 2)).reshape(b, SSD_HEADS, SSD_HEADDIM, D_STATE)
    return (y.reshape(b, s, D_MODEL), r3(ckv), r3(kr), xbc3[:, s - (CONV_W - 1):, :], h)


def _sample_layer(layer, x, cache_c, cache_r, conv_prev, h0, page_table, w):
    b, s, _ = x.shape
    n = b * s
    past = page_table.shape[1] * PAGE_SIZE
    x2 = x.reshape(n, D_MODEL)
    pos = past + jnp.arange(s, dtype=jnp.int32)
    cos, sin = _rope_tables(jnp.tile(pos, b))
    qlat, qrope, ckv, kr, za, zs, xbc, misc = _proj_sample(x2, cos, sin, w)
    olat = _attn_sample(layer, page_table, qlat.reshape(n, MLA_HEADS, KV_RANK), qrope.reshape(n, MLA_HEADS, QK_ROPE),
                        ckv.reshape(n, 1, KV_RANK), kr.reshape(n, 1, QK_ROPE), cache_c,
                        jnp.swapaxes(cache_r, 2, 3))
    mix_a, conv_new, act, xdt_t, dec_t = _post_sample(
        olat.reshape(n, MLA_HEADS * KV_RANK), za, xbc, misc, conv_prev.reshape(n, (CONV_W - 1) * CONV_DIM), w)
    h, y_t = _state_sample(h0.astype(F32).reshape(n, D_SSM, D_STATE), xdt_t, dec_t, act)
    y = _finish_sample(y_t, act, zs, mix_a, x2, w)
    return (y.reshape(b, s, D_MODEL), ckv.reshape(b, s, KV_RANK), kr.reshape(b, s, QK_ROPE),
            conv_new.reshape(b, CONV_W - 1, CONV_DIM), h.reshape(b, SSD_HEADS, SSD_HEADDIM, D_STATE))


def kernel(x_prompt, x_sample, cache_ckv, cache_krope, state_conv, state_ssm, page_table, norm_pre, w_in,
           q_a_norm, w_q_b, kv_a_norm, w_uk, w_uv, conv_w, conv_b, dt_bias, a_log, d_skip, ssm_norm, w_out,
           norm_post):
    assert x_sample.shape[1] == 1, "the sample path handles one new token per sequence"
    depth = w_in.shape[0]
    y_prompt, y_sample = x_prompt, x_sample
    outs = [[] for _ in range(8)]
    for l in range(depth):
        w = _prep_weights({'norm_pre': norm_pre[l], 'w_in': w_in[l], 'q_a_norm': q_a_norm[l],
                           'w_q_b': w_q_b[l], 'kv_a_norm': kv_a_norm[l], 'w_uk': w_uk[l], 'w_uv': w_uv[l],
                           'conv_w': conv_w[l], 'conv_b': conv_b[l], 'dt_bias': dt_bias[l], 'a_log': a_log[l],
                           'd_skip': d_skip[l], 'ssm_norm': ssm_norm[l], 'w_out': w_out[l],
                           'norm_post': norm_post[l]})
        y_prompt, c1, k1, v1, h1 = _prompt_layer(y_prompt, w)
        y_sample, c2, k2, v2, h2 = _sample_layer(l, y_sample, cache_ckv, cache_krope, state_conv[l],
                                                 state_ssm[l], page_table, w)
        for lst, val in zip(outs, (c1, k1, v1, h1, c2, k2, v2, h2)):
            lst.append(val)
    return (y_prompt, y_sample) + tuple(jnp.stack(o) for o in outs)
```

```python
import functools
import math

import jax
import jax.numpy as jnp
from jax import lax
from jax.experimental import pallas as pl
from jax.experimental.pallas import tpu as pltpu

F32 = jnp.float32
BF16 = jnp.bfloat16

D_MODEL = 1024
PAGE_SIZE = 128
D_MIX = 2 * D_MODEL
D_ATTN = D_MIX // 2
D_SSM = D_MIX - D_ATTN
MLA_HEADS = 8
QK_NOPE = 128
QK_ROPE = 64
ROPE_HALF = QK_ROPE // 2
V_HEAD = D_ATTN // MLA_HEADS
Q_RANK = 384
KV_RANK = 256
ROPE_THETA = 10000.0
SOFTMAX_SCALE = (QK_NOPE + QK_ROPE) ** -0.5
SSD_HEADDIM = 64
SSD_HEADS = D_SSM // SSD_HEADDIM
SSD_GROUPS = 2
HEADS_PER_GROUP = SSD_HEADS // SSD_GROUPS
GROUP_WIDTH = D_SSM // SSD_GROUPS
D_STATE = 128
CONV_W = 4
CONV_DIM = D_SSM + 2 * SSD_GROUPS * D_STATE
CHUNK = 128
EPS = 1e-6
SPLIT_PIECES = 3
NEG_BIG = -1e30

LANES = 128
SUBLANES = 8
VMEM_LIMIT_BYTES = 56 * 1024 * 1024

COL_Q = 0
COL_C = COL_Q + Q_RANK
COL_ZA = COL_C + KV_RANK
COL_ZS = COL_ZA + D_ATTN
COL_XBC = COL_ZS + D_SSM
COL_MISC = COL_XBC + CONV_DIM
D_IN_PAD = COL_MISC + LANES
DT_LANE0 = QK_ROPE
QK_PAD = 2 * LANES

PROJ_ROWS = 256
ATTN_TQ = 256
ATTN_HEADS = 2
SSD_STEP_CHUNKS = 2
MERGE_ROWS = 512
CONV_PAD = SUBLANES
DEC_STATE_SEQS = 8
DEC_KV_CHUNK = 2048
DEC_ATTN_SEQS = 2

_NT = (((1,), (1,)), ((), ()))


def _rms(x, w):
    return x * lax.rsqrt(jnp.mean(x * x, axis=-1, keepdims=True) + EPS) * w


def _silu(x):
    return x / (1.0 + jnp.exp(-x))


def _softplus(x):
    return jnp.maximum(x, 0.0) + jnp.log(1.0 + jnp.exp(-jnp.abs(x)))


def _dot(a, b):
    return jnp.dot(a, b, preferred_element_type=F32)


def _rope_tile(x, cos, sin_signed):
    lane = lax.broadcasted_iota(jnp.int32, x.shape, 1)
    first_half = (lane % QK_ROPE) < ROPE_HALF
    partner = jnp.where(first_half,
                        pltpu.roll(x, LANES - ROPE_HALF, 1),
                        pltpu.roll(x, ROPE_HALF, 1))
    return x * cos + partner * sin_signed


def _front(x_ref, npre_ref, win_ref):
    h = _rms(x_ref[...], npre_ref[...]).astype(BF16)

    def seg(lo, hi):
        return _dot(h, win_ref[:, lo:hi])

    return seg


def _proj_prompt_kernel(seq_tiles, x_ref, cos_ref, sin_ref, npre_ref, win_ref, qan_ref, wqb_ref, kvn_ref,
                        wukt_ref, wuv_ref, cw_ref, cb_ref, dtb_ref, alog_ref, expand_ref, dskip_ref, normw_ref,
                        q_ref, kt_ref, v_ref, ckv_ref, kr_ref, za_ref, mixs_ref, state_ref, tail_ref, xp_ref):
    rows = x_ref.shape[0]
    first = pl.program_id(0) % seq_tiles == 0

    @pl.when(first)
    def _():
        xp_ref[0:CONV_PAD, :] = jnp.zeros((CONV_PAD, CONV_DIM), F32)
        state_ref[...] = jnp.zeros(state_ref.shape, F32)

    seg = _front(x_ref, npre_ref, win_ref)
    xp_ref[CONV_PAD:CONV_PAD + rows, :] = seg(COL_XBC, COL_MISC)
    xp = xp_ref[...]
    acc = cb_ref[...]
    for tap in range(CONV_W):
        back = CONV_W - 1 - tap
        shifted = xp if back == 0 else pltpu.roll(xp, back, 0)
        acc = acc + shifted[CONV_PAD:CONV_PAD + rows, :] * cw_ref[tap:tap + 1, :]
    act_all = _silu(acc)
    tail_ref[...] = xp_ref[rows:rows + CONV_PAD, :]
    xp_ref[CONV_PAD - 3:CONV_PAD, :] = xp_ref[CONV_PAD + rows - 3:CONV_PAD + rows, :]

    misc = seg(COL_MISC, D_IN_PAD)
    cos = cos_ref[...]
    sin = sin_ref[...]
    lane = lax.broadcasted_iota(jnp.int32, misc.shape, 1)
    low = lane < QK_ROPE

    kr_full = _rope_tile(misc, cos, sin)
    kr_ref[...] = kr_full[:, :QK_ROPE]
    kr_lo_t = jnp.where(low, kr_full, 0.0).T.astype(BF16)
    kr_hi_t = jnp.where(low, 0.0, pltpu.roll(kr_full, QK_ROPE, 1)).T.astype(BF16)

    c_raw = seg(COL_C, COL_ZA)
    q_a = seg(COL_Q, COL_C)
    za_ref[...] = seg(COL_ZA, COL_ZS)
    ckv = _rms(c_raw, kvn_ref[...])
    ckv_ref[...] = ckv
    cb = ckv.astype(BF16)
    knope_t = lax.dot_general(wukt_ref[...], cb, _NT, preferred_element_type=F32)
    v_ref[...] = _dot(cb, wuv_ref[...]).astype(BF16)

    qn = _rms(q_a, qan_ref[...]).astype(BF16)
    q = _dot(qn, wqb_ref[...])
    nope_w = MLA_HEADS * QK_NOPE
    for hh in range(MLA_HEADS):
        pair = hh // 2
        r = _rope_tile(q[:, nope_w + pair * LANES: nope_w + (pair + 1) * LANES], cos, sin)
        own = low if hh % 2 == 0 else jnp.logical_not(low)
        base = hh * QK_PAD
        q_ref[:, base:base + LANES] = (q[:, hh * QK_NOPE:(hh + 1) * QK_NOPE] * SOFTMAX_SCALE).astype(BF16)
        q_ref[:, base + LANES:base + QK_PAD] = (jnp.where(own, r, 0.0) * SOFTMAX_SCALE).astype(BF16)
        kt_ref[base:base + LANES, :] = knope_t[hh * QK_NOPE:(hh + 1) * QK_NOPE, :].astype(BF16)
        kt_ref[base + LANES:base + QK_PAD, :] = kr_lo_t if hh % 2 == 0 else kr_hi_t

    zs = seg(COL_ZS, COL_XBC)
    for k in range(rows // CHUNK):
        r0 = k * CHUNK
        act = act_all[r0:r0 + CHUNK, :]
        y = _ssd_chunk(act, misc[r0:r0 + CHUNK, :], dtb_ref, alog_ref, expand_ref, state_ref)
        mixs_ref[r0:r0 + CHUNK, :] = _ssd_finish(y, act[:, :D_SSM], zs[r0:r0 + CHUNK, :], dskip_ref[...],
                                                 normw_ref[...])


def _proj_sample_kernel(x_ref, cos_ref, sin_ref, npre_ref, win_ref, qan_ref, wqb_ref, kvn_ref, wukt_ref,
                        qlat_ref, qrope_ref, ckv_ref, kr_ref, za_ref, zs_ref, xbc_ref, misc_ref):
    seg = _front(x_ref, npre_ref, win_ref)
    za_ref[...] = seg(COL_ZA, COL_ZS)
    zs_ref[...] = seg(COL_ZS, COL_XBC)
    xbc_ref[...] = seg(COL_XBC, COL_MISC)
    misc = seg(COL_MISC, D_IN_PAD)
    misc_ref[...] = misc
    cos = cos_ref[...]
    sin = sin_ref[...]
    kr_ref[...] = _rope_tile(misc, cos, sin)[:, :QK_ROPE]
    ckv_ref[...] = _rms(seg(COL_C, COL_ZA), kvn_ref[...])

    qn = _rms(seg(COL_Q, COL_C), qan_ref[...]).astype(BF16)
    q = _dot(qn, wqb_ref[...])
    nope_w = MLA_HEADS * QK_NOPE
    for pair in range(MLA_HEADS // 2):
        lo = nope_w + pair * LANES
        qrope_ref[:, pair * LANES:(pair + 1) * LANES] = _rope_tile(q[:, lo:lo + LANES], cos, sin) * SOFTMAX_SCALE
    for hh in range(MLA_HEADS):
        qh = q[:, hh * QK_NOPE:(hh + 1) * QK_NOPE].astype(BF16)
        qlat_ref[:, hh * KV_RANK:(hh + 1) * KV_RANK] = _dot(qh, wukt_ref[hh]) * SOFTMAX_SCALE


def _full(shape):
    return pl.BlockSpec(shape, lambda *_: (0,) * len(shape))


def _proj_prompt(x2, cos, sin, w):
    n = x2.shape[0]
    tm = PROJ_ROWS
    seq_tiles = cos.shape[0] // tm
    assert tm == SSD_STEP_CHUNKS * CHUNK
    batch = n // cos.shape[0]
    rows = lambda width: pl.BlockSpec((tm, width), lambda i: (i, 0))
    tab = pl.BlockSpec((tm, LANES), lambda i: (i % seq_tiles, 0))
    out_shape = (
        jax.ShapeDtypeStruct((n, MLA_HEADS * QK_PAD), BF16),
        jax.ShapeDtypeStruct((MLA_HEADS * QK_PAD, n), BF16),
        jax.ShapeDtypeStruct((n, D_ATTN), BF16),
        jax.ShapeDtypeStruct((n, KV_RANK), F32),
        jax.ShapeDtypeStruct((n, QK_ROPE), F32),
        jax.ShapeDtypeStruct((n, D_ATTN), F32),
        jax.ShapeDtypeStruct((n, D_SSM), BF16),
        jax.ShapeDtypeStruct((batch, SSD_GROUPS, D_STATE, GROUP_WIDTH), F32),
        jax.ShapeDtypeStruct((batch, CONV_PAD, CONV_DIM), F32),
    )
    per_seq = lambda shape: pl.BlockSpec((None,) + shape[1:], lambda i: (i // seq_tiles,) + (0,) * (len(shape) - 1))
    out_specs = [rows(s.shape[1]) for s in out_shape[:7]]
    out_specs[1] = pl.BlockSpec((MLA_HEADS * QK_PAD, tm), lambda i: (0, i))
    out_specs += [per_seq(out_shape[7].shape), per_seq(out_shape[8].shape)]
    return pl.pallas_call(
        functools.partial(_proj_prompt_kernel, seq_tiles),
        grid=(n // tm,),
        in_specs=[rows(D_MODEL), tab, tab, _full((1, D_MODEL)), _full((D_MODEL, D_IN_PAD)),
                  _full((1, Q_RANK)), _full(w['w_qb'].shape), _full((1, KV_RANK)),
                  _full(w['w_uk2t'].shape), _full(w['w_uv2'].shape),
                  _full((CONV_W, CONV_DIM)), _full((1, CONV_DIM)), _full((1, LANES)), _full((1, LANES)),
                  _full((LANES, D_SSM)), _full((1, D_SSM)), _full((1, D_SSM))],
        out_specs=tuple(out_specs),
        out_shape=out_shape,
        scratch_shapes=[pltpu.VMEM((CONV_PAD + tm, CONV_DIM), F32)],
        compiler_params=pltpu.CompilerParams(dimension_semantics=("arbitrary",),
                                             vmem_limit_bytes=VMEM_LIMIT_BYTES),
        name="proj_prompt",
    )(x2, cos, sin, w['norm_pre'], w['w_in'], w['q_a_norm'], w['w_qb'], w['kv_a_norm'],
      w['w_uk2t'], w['w_uv2'], w['conv_w'], w['conv_b'], w['dt_bias_t'], w['a_log_t'], w['expand'],
      w['d_skip_w'], w['ssm_norm'])


def _proj_sample(x2, cos, sin, w):
    n = x2.shape[0]
    rows = lambda width: pl.BlockSpec((n, width), lambda i: (0, 0))
    out_shape = (
        jax.ShapeDtypeStruct((n, MLA_HEADS * KV_RANK), F32),
        jax.ShapeDtypeStruct((n, MLA_HEADS * QK_ROPE), F32),
        jax.ShapeDtypeStruct((n, KV_RANK), F32),
        jax.ShapeDtypeStruct((n, QK_ROPE), F32),
        jax.ShapeDtypeStruct((n, D_ATTN), F32),
        jax.ShapeDtypeStruct((n, D_SSM), F32),
        jax.ShapeDtypeStruct((n, CONV_DIM), F32),
        jax.ShapeDtypeStruct((n, LANES), F32),
    )
    return pl.pallas_call(
        _proj_sample_kernel,
        grid=(1,),
        in_specs=[rows(D_MODEL), rows(LANES), rows(LANES), _full((1, D_MODEL)), _full((D_MODEL, D_IN_PAD)),
                  _full((1, Q_RANK)), _full(w['w_qb'].shape), _full((1, KV_RANK)),
                  _full(w['w_ukt'].shape)],
        out_specs=tuple(rows(s.shape[1]) for s in out_shape),
        out_shape=out_shape,
        compiler_params=pltpu.CompilerParams(dimension_semantics=("arbitrary",),
                                             vmem_limit_bytes=VMEM_LIMIT_BYTES),
        name="proj_sample",
    )(x2, cos, sin, w['norm_pre'], w['w_in'], w['q_a_norm'], w['w_qb'], w['kv_a_norm'], w['w_ukt'])


def _attn_prompt_kernel(q_ref, kt_ref, v_ref, z_ref, o_ref):
    seq = q_ref.shape[0]
    tq = ATTN_TQ
    row = lax.broadcasted_iota(jnp.int32, (tq, tq), 0)
    col = lax.broadcasted_iota(jnp.int32, (tq, tq), 1)
    causal = col <= row
    for qi in reversed(range(seq // tq)):
        lim = (qi + 1) * tq
        ones_col = (lax.broadcasted_iota(jnp.int32, (lim, LANES), 1) == 0).astype(BF16)
        for hd in range(ATTN_HEADS):
            qk = slice(hd * QK_PAD, (hd + 1) * QK_PAD)
            hv = slice(hd * V_HEAD, (hd + 1) * V_HEAD)
            s = _dot(q_ref[qi * tq:lim, qk], kt_ref[qk, 0:lim])
            diag = jnp.where(causal, s[:, lim - tq:], NEG_BIG)
            s = diag if qi == 0 else jnp.concatenate([s[:, :lim - tq], diag], axis=1)
            m = jnp.max(s, axis=-1, keepdims=True)
            p = jnp.exp(s - m).astype(BF16)
            ol = _dot(p, jnp.concatenate([v_ref[0:lim, hv], ones_col], axis=1))
            o = ol[:, :V_HEAD] / jnp.sum(ol[:, V_HEAD:], axis=-1, keepdims=True)
            o_ref[qi * tq:lim, hv] = (o * _silu(z_ref[qi * tq:lim, hv])).astype(BF16)


def _attn_prompt(q3, kt, v3, z3):
    b, s, _ = q3.shape
    nh = ATTN_HEADS
    hv = pl.BlockSpec((None, s, nh * V_HEAD), lambda i, j: (i, 0, j))
    return pl.pallas_call(
        _attn_prompt_kernel,
        grid=(b, MLA_HEADS // nh),
        in_specs=[pl.BlockSpec((None, s, nh * QK_PAD), lambda i, j: (i, 0, j)),
                  pl.BlockSpec((nh * QK_PAD, s), lambda i, j: (j, i)), hv, hv],
        out_specs=hv,
        out_shape=jax.ShapeDtypeStruct((b, s, D_ATTN), BF16),
        compiler_params=pltpu.CompilerParams(dimension_semantics=("arbitrary", "arbitrary"),
                                             vmem_limit_bytes=VMEM_LIMIT_BYTES),
        name="attn_prompt",
    )(q3, kt, v3, z3)


def _dt_lanes(shape):
    lane = lax.broadcasted_iota(jnp.int32, shape, 1)
    return jnp.logical_and(lane >= DT_LANE0, lane < DT_LANE0 + SSD_HEADS)


def _dt_and_da(misc, dtb, alog):
    dt = jnp.where(_dt_lanes(misc.shape), _softplus(misc + dtb), 0.0)
    return dt, dt * (-jnp.exp(alog))


def _split_bf16(x):
    pieces = []
    for _ in range(SPLIT_PIECES):
        piece = x.astype(BF16).astype(F32)
        pieces.append(piece)
        x = x - piece
    return pieces


def _cumsum_rows(lower_b, x):
    return sum(_dot(lower_b, piece.astype(BF16)) for piece in _split_bf16(x))


def _head_expand(x, expand_ref):
    pieces = _split_bf16(jnp.where(_dt_lanes(x.shape), x, 0.0))
    packed = pieces[0]
    for k in range(1, SPLIT_PIECES):
        packed = packed + pltpu.roll(pieces[k], k * SSD_HEADS, 1)
    return _dot(packed.astype(BF16), expand_ref[...])


def _ssd_finish(y, xs, z, dskip, normw):
    y = y + dskip * xs
    gated = y * _silu(z)
    outs = []
    for g in range(SSD_GROUPS):
        sl = slice(g * GROUP_WIDTH, (g + 1) * GROUP_WIDTH)
        outs.append(_rms(gated[:, sl], normw[:, sl]))
    return jnp.concatenate(outs, axis=1).astype(BF16)


def _out_proj_attn(mix_a, wout_ref):
    return _dot(mix_a, wout_ref[0:D_ATTN, :])


def _out_proj(o_attn, mix_s, wout_ref, npost, x):
    o = o_attn + _dot(mix_s, wout_ref[D_ATTN:D_MIX, :])
    return x + _rms(o, npost)


def _ssd_chunk(act, misc, dtb_ref, alog_ref, expand_ref, state_ref):
    xs = act[:, :D_SSM]
    dt, da = _dt_and_da(misc, dtb_ref[...], alog_ref[...])
    row = lax.broadcasted_iota(jnp.int32, (CHUNK, CHUNK), 0)
    col = lax.broadcasted_iota(jnp.int32, (CHUNK, CHUNK), 1)
    lower = row >= col
    a_cum = _cumsum_rows(lower.astype(BF16), da)
    a_cum_t = a_cum.T
    a_last = a_cum[CHUNK - 1:CHUNK, :]
    xdt = xs * _head_expand(dt, expand_ref)
    decay_out = _head_expand(jnp.exp(a_cum), expand_ref)
    state_decay = decay_out[CHUNK - 1:CHUNK, :]
    xdt_end = (xdt * _head_expand(jnp.exp(a_last - a_cum), expand_ref)).astype(BF16)
    xdt_b = xdt.astype(BF16)
    lane_w = lax.broadcasted_iota(jnp.int32, (CHUNK, LANES), 1)
    first_head = lane_w < SSD_HEADDIM

    ys = []
    for g in range(SSD_GROUPS):
        bm = act[:, D_SSM + g * D_STATE:D_SSM + (g + 1) * D_STATE]
        cm = act[:, D_SSM + (SSD_GROUPS + g) * D_STATE:D_SSM + (SSD_GROUPS + g + 1) * D_STATE]
        bm_b = bm.astype(BF16)
        cm_b = cm.astype(BF16)
        cb = lax.dot_general(cm_b, bm_b, _NT, preferred_element_type=F32)
        gsl = slice(g * GROUP_WIDTH, (g + 1) * GROUP_WIDTH)
        state = state_ref[g]
        y_off = _dot(cm_b, state.astype(BF16)) * decay_out[:, gsl]
        y_diag = []
        for pair in range(HEADS_PER_GROUP // 2):
            halves = []
            x_pair = xdt_b[:, g * GROUP_WIDTH + pair * LANES: g * GROUP_WIDTH + (pair + 1) * LANES]
            for k in range(2):
                lane_h = DT_LANE0 + g * HEADS_PER_GROUP + 2 * pair + k
                seg = a_cum[:, lane_h:lane_h + 1] - a_cum_t[lane_h:lane_h + 1, :]
                decay = jnp.exp(jnp.where(lower, seg, NEG_BIG))
                halves.append(_dot((cb * decay).astype(BF16), x_pair))
            y_diag.append(jnp.where(first_head, halves[0], halves[1]))
        ys.append(jnp.concatenate(y_diag, axis=1) + y_off)
        state_ref[g] = state * state_decay[:, gsl] + _dot(bm.T.astype(BF16), xdt_end[:, gsl])
    return jnp.concatenate(ys, axis=1)


def _merge_kernel(ma_ref, ms_ref, x_ref, wout_ref, npost_ref, y_ref):
    y_ref[...] = _out_proj(_out_proj_attn(ma_ref[...], wout_ref), ms_ref[...], wout_ref, npost_ref[...], x_ref[...])


def _merge(mix_a, mix_s, x2, w):
    n = x2.shape[0]
    tm = MERGE_ROWS
    rows = lambda width: pl.BlockSpec((tm, width), lambda i: (i, 0))
    return pl.pallas_call(
        _merge_kernel,
        grid=(n // tm,),
        in_specs=[rows(D_ATTN), rows(D_SSM), rows(D_MODEL), _full((D_MIX, D_MODEL)), _full((1, D_MODEL))],
        out_specs=rows(D_MODEL),
        out_shape=jax.ShapeDtypeStruct((n, D_MODEL), F32),
        compiler_params=pltpu.CompilerParams(dimension_semantics=("arbitrary",),
                                             vmem_limit_bytes=VMEM_LIMIT_BYTES),
        name="merge",
    )(mix_a, mix_s, x2, w['w_out'], w['norm_post'])


def _attn_sample_kernel(layer, pt_ref, qlat_ref, qrope_ref, cnew_ref, rnew_ref, cache_c_ref, cache_rt_ref,
                        o_ref, cbuf, rbuf, sem):
    i = pl.program_id(0)
    n = pl.num_programs(0)
    n_pages = pt_ref.shape[1]
    seqs = qlat_ref.shape[0]
    past = n_pages * PAGE_SIZE
    slot = i % 2

    def page_copies(step, slot_):
        copies = []
        for j in range(seqs):
            buf = slot_ * seqs + j
            for p in range(n_pages):
                page = pt_ref[step * seqs + j, p]
                dst = pl.ds(p * PAGE_SIZE, PAGE_SIZE)
                copies.append(pltpu.make_async_copy(cache_c_ref.at[layer, page], cbuf.at[buf, dst],
                                                    sem.at[0, slot_]))
                copies.append(pltpu.make_async_copy(cache_rt_ref.at[layer, page], rbuf.at[buf, :, dst],
                                                    sem.at[1, slot_]))
        return copies

    @pl.when(i == 0)
    def _():
        for cp in page_copies(0, 0):
            cp.start()

    @pl.when(i + 1 < n)
    def _():
        for cp in page_copies(i + 1, 1 - slot):
            cp.start()

    for cp in page_copies(i, slot):
        cp.wait()

    q_pos = past
    chunks = [pl.ds(c * DEC_KV_CHUNK, DEC_KV_CHUNK) for c in range(past // DEC_KV_CHUNK)]
    for j in range(seqs):
        buf = slot * seqs + j
        qlat = qlat_ref[j]
        qrope = qrope_ref[j]
        cnew = cnew_ref[j]
        rnew = rnew_ref[j]
        s_new = (jnp.sum(qlat * cnew, axis=-1, keepdims=True)
                 + jnp.sum(qrope * rnew, axis=-1, keepdims=True))
        s = jnp.concatenate(
            [lax.dot_general(qlat, cbuf[buf, keys, :], _NT, preferred_element_type=F32)
             + _dot(qrope, rbuf[buf, :, keys]) for keys in chunks], axis=1)
        k_pos = lax.broadcasted_iota(jnp.int32, s.shape, 1)
        s = jnp.where(k_pos <= q_pos, s, NEG_BIG)
        m = jnp.maximum(jnp.max(s, axis=-1, keepdims=True), s_new)
        p = jnp.exp(s - m)
        p_new = jnp.exp(s_new - m)
        l = jnp.sum(p, axis=-1, keepdims=True) + p_new
        acc = p_new * cnew
        for c, keys in enumerate(chunks):
            acc = acc + _dot(p[:, c * DEC_KV_CHUNK:(c + 1) * DEC_KV_CHUNK], cbuf[buf, keys, :])
        o_ref[j] = acc / l


def _attn_sample(layer, page_table, qlat3, qrope3, cnew3, rnew3, cache_c, cache_rt):
    b, n_pages = page_table.shape
    past = n_pages * PAGE_SIZE
    seqs = DEC_ATTN_SEQS
    assert past % DEC_KV_CHUNK == 0 and b % seqs == 0
    per_step = lambda d1, d2: pl.BlockSpec((seqs, d1, d2), lambda i, pt: (i, 0, 0))
    grid_spec = pltpu.PrefetchScalarGridSpec(
        num_scalar_prefetch=1,
        grid=(b // seqs,),
        in_specs=[per_step(MLA_HEADS, KV_RANK), per_step(MLA_HEADS, QK_ROPE), per_step(1, KV_RANK),
                  per_step(1, QK_ROPE), pl.BlockSpec(memory_space=pl.ANY), pl.BlockSpec(memory_space=pl.ANY)],
        out_specs=per_step(MLA_HEADS, KV_RANK),
        scratch_shapes=[pltpu.VMEM((2 * seqs, past, KV_RANK), F32), pltpu.VMEM((2 * seqs, QK_ROPE, past), F32),
                        pltpu.SemaphoreType.DMA((2, 2))],
    )
    return pl.pallas_call(
        functools.partial(_attn_sample_kernel, layer),
        grid_spec=grid_spec,
        out_shape=jax.ShapeDtypeStruct((b, MLA_HEADS, KV_RANK), F32),
        compiler_params=pltpu.CompilerParams(dimension_semantics=("arbitrary",),
                                             vmem_limit_bytes=VMEM_LIMIT_BYTES),
        name="attn_sample",
    )(page_table, qlat3, qrope3, cnew3, rnew3, cache_c, cache_rt)


def _post_sample_kernel(olat_ref, wuv_ref, za_ref, xbc_ref, misc_ref, cprev_ref, cw_ref, cb_ref,
                        dtb_ref, alog_ref, expand_ref,
                        mixa_ref, cnew_ref, act_ref, xdt_t_ref, dec_t_ref):
    for hh in range(MLA_HEADS):
        o = _dot(olat_ref[:, hh * KV_RANK:(hh + 1) * KV_RANK].astype(BF16), wuv_ref[hh])
        vs = slice(hh * V_HEAD, (hh + 1) * V_HEAD)
        mixa_ref[:, vs] = (o * _silu(za_ref[:, vs])).astype(BF16)

    xb = xbc_ref[...]
    acc = cb_ref[...]
    for k in range(CONV_W - 1):
        acc = acc + cprev_ref[:, k * CONV_DIM:(k + 1) * CONV_DIM] * cw_ref[k:k + 1, :]
    acc = acc + xb * cw_ref[CONV_W - 1:CONV_W, :]
    act = _silu(acc)
    act_ref[...] = act
    cnew_ref[:, 0:(CONV_W - 2) * CONV_DIM] = cprev_ref[:, CONV_DIM:(CONV_W - 1) * CONV_DIM]
    cnew_ref[:, (CONV_W - 2) * CONV_DIM:(CONV_W - 1) * CONV_DIM] = xb

    dt, da = _dt_and_da(misc_ref[...], dtb_ref[...], alog_ref[...])
    xdt = act[:, :D_SSM] * _head_expand(dt, expand_ref)
    decay = _head_expand(jnp.exp(da), expand_ref)
    xdt_t_ref[...] = xdt.T
    dec_t_ref[...] = decay.T


def _post_sample(olat2, za, xbc, misc, cprev2, w):
    n = olat2.shape[0]
    full = lambda a: _full(a.shape)
    args = (olat2, w['w_uvh'], za, xbc, misc, cprev2, w['conv_w'], w['conv_b'], w['dt_bias_t'], w['a_log_t'],
            w['expand'])
    out_shape = (
        jax.ShapeDtypeStruct((n, D_ATTN), BF16),
        jax.ShapeDtypeStruct((n, (CONV_W - 1) * CONV_DIM), F32),
        jax.ShapeDtypeStruct((n, CONV_DIM), F32),
        jax.ShapeDtypeStruct((D_SSM, n), F32),
        jax.ShapeDtypeStruct((D_SSM, n), F32),
    )
    return pl.pallas_call(
        _post_sample_kernel,
        grid=(1,),
        in_specs=[full(a) for a in args],
        out_specs=tuple(_full(s.shape) for s in out_shape),
        out_shape=out_shape,
        compiler_params=pltpu.CompilerParams(dimension_semantics=("arbitrary",),
                                             vmem_limit_bytes=VMEM_LIMIT_BYTES),
        name="post_sample",
    )(*args)


def _state_sample_kernel(h0_ref, xdt_t_ref, dec_t_ref, bm_ref, cm_ref, h_ref, y_t_ref):
    t = pl.program_id(0)
    tb = h0_ref.shape[0]
    rows, n_seq = xdt_t_ref.shape

    @pl.when(t == 0)
    def _():
        y_t_ref[...] = jnp.zeros(y_t_ref.shape, F32)

    lane = lax.broadcasted_iota(jnp.int32, (rows, n_seq), 1)
    xdt_t = xdt_t_ref[...]
    dec_t = dec_t_ref[...]
    for j in range(tb):
        own = lane == t * tb + j
        x_col = jnp.sum(jnp.where(own, xdt_t, 0.0), axis=1, keepdims=True)
        d_col = jnp.sum(jnp.where(own, dec_t, 0.0), axis=1, keepdims=True)
        b_rows = jnp.concatenate(
            [jnp.broadcast_to(bm_ref[j:j + 1, g * D_STATE:(g + 1) * D_STATE], (GROUP_WIDTH, D_STATE))
             for g in range(SSD_GROUPS)], axis=0)
        c_rows = jnp.concatenate(
            [jnp.broadcast_to(cm_ref[j:j + 1, g * D_STATE:(g + 1) * D_STATE], (GROUP_WIDTH, D_STATE))
             for g in range(SSD_GROUPS)], axis=0)
        h = d_col * h0_ref[j] + x_col * b_rows
        h_ref[j] = h
        y_col = jnp.sum(h * c_rows, axis=1, keepdims=True)
        y_t_ref[...] = jnp.where(own, y_col, y_t_ref[...])


def _state_sample(h0, xdt_t, dec_t, act):
    n = h0.shape[0]
    tb = DEC_STATE_SEQS
    bc_w = SSD_GROUPS * D_STATE
    st = pl.BlockSpec((tb, D_SSM, D_STATE), lambda i: (i, 0, 0))
    return pl.pallas_call(
        _state_sample_kernel,
        grid=(n // tb,),
        in_specs=[st, _full((D_SSM, n)), _full((D_SSM, n)),
                  pl.BlockSpec((tb, bc_w), lambda i: (i, D_SSM // bc_w)),
                  pl.BlockSpec((tb, bc_w), lambda i: (i, D_SSM // bc_w + 1))],
        out_specs=(st, _full((D_SSM, n))),
        out_shape=(jax.ShapeDtypeStruct(h0.shape, F32), jax.ShapeDtypeStruct((D_SSM, n), F32)),
        compiler_params=pltpu.CompilerParams(dimension_semantics=("arbitrary",),
                                             vmem_limit_bytes=VMEM_LIMIT_BYTES),
        name="state_sample",
    )(h0, xdt_t, dec_t, act, act)


def _finish_sample_kernel(y_t_ref, act_ref, z_ref, dskip_ref, normw_ref, ma_ref, x_ref, wout_ref, npost_ref,
                          y_ref):
    mix_s = _ssd_finish(y_t_ref[...].T, act_ref[...], z_ref[...], dskip_ref[...], normw_ref[...])
    y_ref[...] = _out_proj(_out_proj_attn(ma_ref[...], wout_ref), mix_s, wout_ref, npost_ref[...], x_ref[...])


def _finish_sample(y_t, act, zs, mix_a, x2, w):
    n = x2.shape[0]
    return pl.pallas_call(
        _finish_sample_kernel,
        grid=(1,),
        in_specs=[_full(y_t.shape), pl.BlockSpec((n, D_SSM), lambda i: (0, 0)), _full(zs.shape),
                  _full((1, D_SSM)), _full((1, D_SSM)), _full(mix_a.shape), _full(x2.shape),
                  _full((D_MIX, D_MODEL)), _full((1, D_MODEL))],
        out_specs=_full((n, D_MODEL)),
        out_shape=jax.ShapeDtypeStruct((n, D_MODEL), F32),
        compiler_params=pltpu.CompilerParams(dimension_semantics=("arbitrary",),
                                             vmem_limit_bytes=VMEM_LIMIT_BYTES),
        name="finish_sample",
    )(y_t, act, zs, w['d_skip_w'], w['ssm_norm'], mix_a, x2, w['w_out'], w['norm_post'])


def _rope_tables(pos):
    inv_freq = ROPE_THETA ** (-jnp.arange(ROPE_HALF, dtype=F32) / ROPE_HALF)
    ang = pos.astype(F32)[:, None] * inv_freq[None, :]
    cos, sin = jnp.cos(ang), jnp.sin(ang)
    reps = LANES // QK_ROPE
    return jnp.tile(cos, (1, 2 * reps)), jnp.tile(jnp.concatenate([-sin, sin], axis=1), (1, reps))


def _prep_weights(lw):
    row = lambda v: v.reshape(1, -1).astype(F32)
    w_in = lw['w_in']
    q_a, c_raw, kr, za, zs, xbc, dt = jnp.split(
        w_in, (Q_RANK, Q_RANK + KV_RANK, Q_RANK + KV_RANK + QK_ROPE,
               Q_RANK + KV_RANK + QK_ROPE + D_ATTN, Q_RANK + KV_RANK + QK_ROPE + D_ATTN + D_SSM,
               Q_RANK + KV_RANK + QK_ROPE + D_ATTN + D_SSM + CONV_DIM), axis=1)
    pad = jnp.zeros((D_MODEL, LANES - QK_ROPE - SSD_HEADS), w_in.dtype)
    w_qb = lw['w_q_b'].reshape(Q_RANK, MLA_HEADS, QK_NOPE + QK_ROPE)
    lane_pad = lambda v: jnp.pad(v.reshape(1, -1).astype(F32), ((0, 0), (DT_LANE0, LANES - DT_LANE0 - SSD_HEADS)))
    head_of_col = jnp.arange(D_SSM) // SSD_HEADDIM
    return {
        'norm_pre': row(lw['norm_pre']),
        'w_in': jnp.concatenate([q_a, c_raw, za, zs, xbc, kr, dt, pad], axis=1).astype(BF16),
        'q_a_norm': row(lw['q_a_norm']),
        'w_qb': jnp.concatenate([w_qb[:, :, :QK_NOPE].reshape(Q_RANK, -1),
                                 w_qb[:, :, QK_NOPE:].reshape(Q_RANK, -1)], axis=1).astype(BF16),
        'kv_a_norm': row(lw['kv_a_norm']),
        'w_uk2t': lw['w_uk'].reshape(KV_RANK, MLA_HEADS * QK_NOPE).T.astype(BF16),
        'w_uv2': lw['w_uv'].reshape(KV_RANK, MLA_HEADS * V_HEAD).astype(BF16),
        'w_ukt': jnp.transpose(lw['w_uk'], (1, 2, 0)).astype(BF16),
        'w_uvh': jnp.transpose(lw['w_uv'], (1, 0, 2)).astype(BF16),
        'conv_w': lw['conv_w'].astype(F32),
        'conv_b': row(lw['conv_b']),
        'dt_bias_t': lane_pad(lw['dt_bias']),
        'a_log_t': lane_pad(lw['a_log']),
        'expand': sum((jnp.arange(LANES)[:, None] == DT_LANE0 + k * SSD_HEADS + head_of_col[None, :])
                      for k in range(SPLIT_PIECES)).astype(BF16),
        'd_skip_w': jnp.repeat(lw['d_skip'].astype(F32), SSD_HEADDIM).reshape(1, D_SSM),
        'ssm_norm': row(lw['ssm_norm']),
        'w_out': lw['w_out'].astype(BF16),
        'norm_post': row(lw['norm_post']),
    }


def _prompt_layer(x, w):
    b, s, _ = x.shape
    x2 = x.reshape(b * s, D_MODEL)
    cos, sin = _rope_tables(jnp.arange(s, dtype=jnp.int32))
    q, kt, v, ckv, kr, za, mix_s, state, tail = _proj_prompt(x2, cos, sin, w)
    r3 = lambda a: a.reshape(b, s, a.shape[-1])
    mix_a = _attn_prompt(r3(q), kt, r3(v), r3(za))
    y = _merge(mix_a.reshape(b * s, D_ATTN), mix_s, x2, w)
    h = state.reshape(b, SSD_GROUPS, D_STATE, HEADS_PER_GROUP, SSD_HEADDIM)
    h = jnp.transpose(h, (0, 1, 3, 4, 2)).reshape(b, SSD_HEADS, SSD_HEADDIM, D_STATE)
    return (y.reshape(b, s, D_MODEL), r3(ckv), r3(kr), tail[:, CONV_PAD - (CONV_W - 1):, :], h)


def _sample_layer(layer, x, cache_c, cache_r, conv_prev, h0, page_table, w):
    b, s, _ = x.shape
    n = b * s
    past = page_table.shape[1] * PAGE_SIZE
    x2 = x.reshape(n, D_MODEL)
    pos = past + jnp.arange(s, dtype=jnp.int32)
    cos, sin = _rope_tables(jnp.tile(pos, b))
    qlat, qrope, ckv, kr, za, zs, xbc, misc = _proj_sample(x2, cos, sin, w)
    olat = _attn_sample(layer, page_table, qlat.reshape(n, MLA_HEADS, KV_RANK), qrope.reshape(n, MLA_HEADS, QK_ROPE),
                        ckv.reshape(n, 1, KV_RANK), kr.reshape(n, 1, QK_ROPE), cache_c,
                        jnp.swapaxes(cache_r, 2, 3))
    mix_a, conv_new, act, xdt_t, dec_t = _post_sample(
        olat.reshape(n, MLA_HEADS * KV_RANK), za, xbc, misc, conv_prev.reshape(n, (CONV_W - 1) * CONV_DIM), w)
    h, y_t = _state_sample(h0.astype(F32).reshape(n, D_SSM, D_STATE), xdt_t, dec_t, act)
    y = _finish_sample(y_t, act, zs, mix_a, x2, w)
    return (y.reshape(b, s, D_MODEL), ckv.reshape(b, s, KV_RANK), kr.reshape(b, s, QK_ROPE),
            conv_new.reshape(b, CONV_W - 1, CONV_DIM), h.reshape(b, SSD_HEADS, SSD_HEADDIM, D_STATE))


def kernel(x_prompt, x_sample, cache_ckv, cache_krope, state_conv, state_ssm, page_table, norm_pre, w_in,
           q_a_norm, w_q_b, kv_a_norm, w_uk, w_uv, conv_w, conv_b, dt_bias, a_log, d_skip, ssm_norm, w_out,
           norm_post):
    assert x_sample.shape[1] == 1, "the sample path handles one new token per sequence"
    depth = w_in.shape[0]
    y_prompt, y_sample = x_prompt, x_sample
    outs = [[] for _ in range(8)]
    for l in range(depth):
        w = _prep_weights({'norm_pre': norm_pre[l], 'w_in': w_in[l], 'q_a_norm': q_a_norm[l],
                           'w_q_b': w_q_b[l], 'kv_a_norm': kv_a_norm[l], 'w_uk': w_uk[l], 'w_uv': w_uv[l],
                           'conv_w': conv_w[l], 'conv_b': conv_b[l], 'dt_bias': dt_bias[l], 'a_log': a_log[l],
                           'd_skip': d_skip[l], 'ssm_norm': ssm_norm[l], 'w_out': w_out[l],
                           'norm_post': norm_post[l]})
        y_prompt, c1, k1, v1, h1 = _prompt_layer(y_prompt, w)
        y_sample, c2, k2, v2, h2 = _sample_layer(l, y_sample, cache_ckv, cache_krope, state_conv[l],
                                                 state_ssm[l], page_table, w)
        for lst, val in zip(outs, (c1, k1, v1, h1, c2, k2, v2, h2)):
            lst.append(val)
    return (y_prompt, y_sample) + tuple(jnp.stack(o) for o in outs)
```

```python
import functools
import math

import jax
import jax.numpy as jnp
from jax import lax
from jax.experimental import pallas as pl
from jax.experimental.pallas import tpu as pltpu

F32 = jnp.float32
BF16 = jnp.bfloat16

D_MODEL = 1024
PAGE_SIZE = 128
D_MIX = 2 * D_MODEL
D_ATTN = D_MIX // 2
D_SSM = D_MIX - D_ATTN
MLA_HEADS = 8
QK_NOPE = 128
QK_ROPE = 64
ROPE_HALF = QK_ROPE // 2
V_HEAD = D_ATTN // MLA_HEADS
Q_RANK = 384
KV_RANK = 256
ROPE_THETA = 10000.0
SOFTMAX_SCALE = (QK_NOPE + QK_ROPE) ** -0.5
SSD_HEADDIM = 64
SSD_HEADS = D_SSM // SSD_HEADDIM
SSD_GROUPS = 2
HEADS_PER_GROUP = SSD_HEADS // SSD_GROUPS
GROUP_WIDTH = D_SSM // SSD_GROUPS
D_STATE = 128
CONV_W = 4
CONV_DIM = D_SSM + 2 * SSD_GROUPS * D_STATE
CHUNK = 128
EPS = 1e-6
SPLIT_PIECES = 3
NEG_BIG = -1e30

LANES = 128
SUBLANES = 8
VMEM_LIMIT_BYTES = 56 * 1024 * 1024

COL_Q = 0
COL_C = COL_Q + Q_RANK
COL_ZA = COL_C + KV_RANK
COL_ZS = COL_ZA + D_ATTN
COL_XBC = COL_ZS + D_SSM
COL_MISC = COL_XBC + CONV_DIM
D_IN_PAD = COL_MISC + LANES
DT_LANE0 = QK_ROPE
QK_PAD = 2 * LANES

PROJ_ROWS = 256
ATTN_TQ = 256
ATTN_HEADS = 4
SSD_STEP_CHUNKS = 2
MERGE_ROWS = 512
CONV_PAD = SUBLANES
DEC_STATE_SEQS = 8
DEC_KV_CHUNK = 2048
DEC_ATTN_SEQS = 2

_NT = (((1,), (1,)), ((), ()))


def _rms(x, w):
    return x * lax.rsqrt(jnp.mean(x * x, axis=-1, keepdims=True) + EPS) * w


def _silu(x):
    return x / (1.0 + jnp.exp(-x))


def _softplus(x):
    return jnp.maximum(x, 0.0) + jnp.log(1.0 + jnp.exp(-jnp.abs(x)))


def _dot(a, b):
    return jnp.dot(a, b, preferred_element_type=F32)


def _rope_tile(x, cos, sin_signed):
    lane = lax.broadcasted_iota(jnp.int32, x.shape, 1)
    first_half = (lane % QK_ROPE) < ROPE_HALF
    partner = jnp.where(first_half,
                        pltpu.roll(x, LANES - ROPE_HALF, 1),
                        pltpu.roll(x, ROPE_HALF, 1))
    return x * cos + partner * sin_signed


def _front(x_ref, npre_ref, win_ref):
    h = _rms(x_ref[...], npre_ref[...]).astype(BF16)

    def seg(lo, hi):
        return _dot(h, win_ref[:, lo:hi])

    return seg


def _proj_prompt_kernel(seq_tiles, x_ref, cos_ref, sin_ref, npre_ref, win_ref, qan_ref, wqb_ref, kvn_ref,
                        wukt_ref, wuv_ref, cw_ref, cb_ref, dtb_ref, alog_ref, expand_ref, dskip_ref, normw_ref,
                        q_ref, kt_ref, v_ref, ckv_ref, kr_ref, za_ref, mixs_ref, state_ref, tail_ref, xp_ref):
    rows = x_ref.shape[0]
    first = pl.program_id(0) % seq_tiles == 0

    @pl.when(first)
    def _():
        xp_ref[0:CONV_PAD, :] = jnp.zeros((CONV_PAD, CONV_DIM), F32)
        state_ref[...] = jnp.zeros(state_ref.shape, F32)

    seg = _front(x_ref, npre_ref, win_ref)
    xp_ref[CONV_PAD:CONV_PAD + rows, :] = seg(COL_XBC, COL_MISC)
    xp = xp_ref[...]
    acc = cb_ref[...]
    for tap in range(CONV_W):
        back = CONV_W - 1 - tap
        shifted = xp if back == 0 else pltpu.roll(xp, back, 0)
        acc = acc + shifted[CONV_PAD:CONV_PAD + rows, :] * cw_ref[tap:tap + 1, :]
    act_all = _silu(acc)
    tail_ref[...] = xp_ref[rows:rows + CONV_PAD, :]
    xp_ref[CONV_PAD - 3:CONV_PAD, :] = xp_ref[CONV_PAD + rows - 3:CONV_PAD + rows, :]

    misc = seg(COL_MISC, D_IN_PAD)
    cos = cos_ref[...]
    sin = sin_ref[...]
    lane = lax.broadcasted_iota(jnp.int32, misc.shape, 1)
    low = lane < QK_ROPE

    kr_full = _rope_tile(misc, cos, sin)
    kr_t = jnp.where(low, kr_full, 0.0).T
    kr_ref[...] = kr_t[:QK_ROPE, :]
    kr_lo_t = kr_t.astype(BF16)
    kr_hi_t = jnp.where(low, 0.0, pltpu.roll(kr_full, QK_ROPE, 1)).T.astype(BF16)

    c_raw = seg(COL_C, COL_ZA)
    q_a = seg(COL_Q, COL_C)
    za_ref[...] = seg(COL_ZA, COL_ZS)
    ckv = _rms(c_raw, kvn_ref[...])
    ckv_ref[...] = ckv
    cb = ckv.astype(BF16)
    knope_t = lax.dot_general(wukt_ref[...], cb, _NT, preferred_element_type=F32)
    v_ref[...] = _dot(cb, wuv_ref[...]).astype(BF16)

    qn = _rms(q_a, qan_ref[...]).astype(BF16)
    q = _dot(qn, wqb_ref[...])
    nope_w = MLA_HEADS * QK_NOPE
    for hh in range(MLA_HEADS):
        pair = hh // 2
        r = _rope_tile(q[:, nope_w + pair * LANES: nope_w + (pair + 1) * LANES], cos, sin)
        own = low if hh % 2 == 0 else jnp.logical_not(low)
        base = hh * QK_PAD
        q_ref[:, base:base + LANES] = (q[:, hh * QK_NOPE:(hh + 1) * QK_NOPE] * SOFTMAX_SCALE).astype(BF16)
        q_ref[:, base + LANES:base + QK_PAD] = (jnp.where(own, r, 0.0) * SOFTMAX_SCALE).astype(BF16)
        kt_ref[base:base + LANES, :] = knope_t[hh * QK_NOPE:(hh + 1) * QK_NOPE, :].astype(BF16)
        kt_ref[base + LANES:base + QK_PAD, :] = kr_lo_t if hh % 2 == 0 else kr_hi_t

    zs = seg(COL_ZS, COL_XBC)
    for k in range(rows // CHUNK):
        r0 = k * CHUNK
        act = act_all[r0:r0 + CHUNK, :]
        y = _ssd_chunk(act, misc[r0:r0 + CHUNK, :], dtb_ref, alog_ref, expand_ref, state_ref)
        mixs_ref[r0:r0 + CHUNK, :] = _ssd_finish(y, act[:, :D_SSM], zs[r0:r0 + CHUNK, :], dskip_ref[...],
                                                 normw_ref[...])


def _proj_sample_kernel(x_ref, cos_ref, sin_ref, npre_ref, win_ref, qan_ref, wqb_ref, kvn_ref, wukt_ref,
                        qlat_ref, qrope_ref, ckv_ref, kr_ref, za_ref, zs_ref, xbc_ref, misc_ref):
    seg = _front(x_ref, npre_ref, win_ref)
    za_ref[...] = seg(COL_ZA, COL_ZS)
    zs_ref[...] = seg(COL_ZS, COL_XBC)
    xbc_ref[...] = seg(COL_XBC, COL_MISC)
    misc = seg(COL_MISC, D_IN_PAD)
    misc_ref[...] = misc
    cos = cos_ref[...]
    sin = sin_ref[...]
    kr_ref[...] = _rope_tile(misc, cos, sin)[:, :QK_ROPE]
    ckv_ref[...] = _rms(seg(COL_C, COL_ZA), kvn_ref[...])

    qn = _rms(seg(COL_Q, COL_C), qan_ref[...]).astype(BF16)
    q = _dot(qn, wqb_ref[...])
    nope_w = MLA_HEADS * QK_NOPE
    for pair in range(MLA_HEADS // 2):
        lo = nope_w + pair * LANES
        qrope_ref[:, pair * LANES:(pair + 1) * LANES] = _rope_tile(q[:, lo:lo + LANES], cos, sin) * SOFTMAX_SCALE
    for hh in range(MLA_HEADS):
        qh = q[:, hh * QK_NOPE:(hh + 1) * QK_NOPE].astype(BF16)
        qlat_ref[:, hh * KV_RANK:(hh + 1) * KV_RANK] = _dot(qh, wukt_ref[hh]) * SOFTMAX_SCALE


def _full(shape):
    return pl.BlockSpec(shape, lambda *_: (0,) * len(shape))


def _proj_prompt(x2, cos, sin, w):
    n = x2.shape[0]
    tm = PROJ_ROWS
    seq_tiles = cos.shape[0] // tm
    assert tm == SSD_STEP_CHUNKS * CHUNK
    batch = n // cos.shape[0]
    rows = lambda width: pl.BlockSpec((tm, width), lambda i: (i, 0))
    tab = pl.BlockSpec((tm, LANES), lambda i: (i % seq_tiles, 0))
    out_shape = (
        jax.ShapeDtypeStruct((n, MLA_HEADS * QK_PAD), BF16),
        jax.ShapeDtypeStruct((MLA_HEADS * QK_PAD, n), BF16),
        jax.ShapeDtypeStruct((n, D_ATTN), BF16),
        jax.ShapeDtypeStruct((n, KV_RANK), F32),
        jax.ShapeDtypeStruct((batch, QK_ROPE, cos.shape[0]), F32),
        jax.ShapeDtypeStruct((n, D_ATTN), F32),
        jax.ShapeDtypeStruct((n, D_SSM), BF16),
        jax.ShapeDtypeStruct((batch, SSD_GROUPS, D_STATE, GROUP_WIDTH), F32),
        jax.ShapeDtypeStruct((batch, CONV_PAD, CONV_DIM), F32),
    )
    per_seq = lambda shape: pl.BlockSpec((None,) + shape[1:], lambda i: (i // seq_tiles,) + (0,) * (len(shape) - 1))
    out_specs = [rows(s.shape[1]) for s in out_shape[:7]]
    out_specs[1] = pl.BlockSpec((MLA_HEADS * QK_PAD, tm), lambda i: (0, i))
    out_specs[4] = pl.BlockSpec((None, QK_ROPE, tm), lambda i: (i // seq_tiles, 0, i % seq_tiles))
    out_specs += [per_seq(out_shape[7].shape), per_seq(out_shape[8].shape)]
    return pl.pallas_call(
        functools.partial(_proj_prompt_kernel, seq_tiles),
        grid=(n // tm,),
        in_specs=[rows(D_MODEL), tab, tab, _full((1, D_MODEL)), _full((D_MODEL, D_IN_PAD)),
                  _full((1, Q_RANK)), _full(w['w_qb'].shape), _full((1, KV_RANK)),
                  _full(w['w_uk2t'].shape), _full(w['w_uv2'].shape),
                  _full((CONV_W, CONV_DIM)), _full((1, CONV_DIM)), _full((1, LANES)), _full((1, LANES)),
                  _full((LANES, D_SSM)), _full((1, D_SSM)), _full((1, D_SSM))],
        out_specs=tuple(out_specs),
        out_shape=out_shape,
        scratch_shapes=[pltpu.VMEM((CONV_PAD + tm, CONV_DIM), F32)],
        compiler_params=pltpu.CompilerParams(dimension_semantics=("arbitrary",),
                                             vmem_limit_bytes=VMEM_LIMIT_BYTES),
        name="proj_prompt",
    )(x2, cos, sin, w['norm_pre'], w['w_in'], w['q_a_norm'], w['w_qb'], w['kv_a_norm'],
      w['w_uk2t'], w['w_uv2'], w['conv_w'], w['conv_b'], w['dt_bias_t'], w['a_log_t'], w['expand'],
      w['d_skip_w'], w['ssm_norm'])


def _proj_sample(x2, cos, sin, w):
    n = x2.shape[0]
    rows = lambda width: pl.BlockSpec((n, width), lambda i: (0, 0))
    out_shape = (
        jax.ShapeDtypeStruct((n, MLA_HEADS * KV_RANK), F32),
        jax.ShapeDtypeStruct((n, MLA_HEADS * QK_ROPE), F32),
        jax.ShapeDtypeStruct((n, KV_RANK), F32),
        jax.ShapeDtypeStruct((n, QK_ROPE), F32),
        jax.ShapeDtypeStruct((n, D_ATTN), F32),
        jax.ShapeDtypeStruct((n, D_SSM), F32),
        jax.ShapeDtypeStruct((n, CONV_DIM), F32),
        jax.ShapeDtypeStruct((n, LANES), F32),
    )
    return pl.pallas_call(
        _proj_sample_kernel,
        grid=(1,),
        in_specs=[rows(D_MODEL), rows(LANES), rows(LANES), _full((1, D_MODEL)), _full((D_MODEL, D_IN_PAD)),
                  _full((1, Q_RANK)), _full(w['w_qb'].shape), _full((1, KV_RANK)),
                  _full(w['w_ukt'].shape)],
        out_specs=tuple(rows(s.shape[1]) for s in out_shape),
        out_shape=out_shape,
        compiler_params=pltpu.CompilerParams(dimension_semantics=("arbitrary",),
                                             vmem_limit_bytes=VMEM_LIMIT_BYTES),
        name="proj_sample",
    )(x2, cos, sin, w['norm_pre'], w['w_in'], w['q_a_norm'], w['w_qb'], w['kv_a_norm'], w['w_ukt'])


def _attn_prompt_kernel(q_ref, kt_ref, v_ref, z_ref, o_ref):
    seq = q_ref.shape[0]
    tq = ATTN_TQ
    row = lax.broadcasted_iota(jnp.int32, (tq, tq), 0)
    col = lax.broadcasted_iota(jnp.int32, (tq, tq), 1)
    causal = col <= row
    for qi in reversed(range(seq // tq)):
        lim = (qi + 1) * tq
        ones_col = (lax.broadcasted_iota(jnp.int32, (lim, LANES), 1) == 0).astype(BF16)
        for hd in range(ATTN_HEADS):
            qk = slice(hd * QK_PAD, (hd + 1) * QK_PAD)
            hv = slice(hd * V_HEAD, (hd + 1) * V_HEAD)
            s = _dot(q_ref[qi * tq:lim, qk], kt_ref[qk, 0:lim])
            diag = jnp.where(causal, s[:, lim - tq:], NEG_BIG)
            s = diag if qi == 0 else jnp.concatenate([s[:, :lim - tq], diag], axis=1)
            m = jnp.max(s, axis=-1, keepdims=True)
            p = jnp.exp(s - m).astype(BF16)
            ol = _dot(p, jnp.concatenate([v_ref[0:lim, hv], ones_col], axis=1))
            o = ol[:, :V_HEAD] / jnp.sum(ol[:, V_HEAD:], axis=-1, keepdims=True)
            o_ref[qi * tq:lim, hv] = (o * _silu(z_ref[qi * tq:lim, hv])).astype(BF16)


def _attn_prompt(q3, kt, v3, z3):
    b, s, _ = q3.shape
    nh = ATTN_HEADS
    hv = pl.BlockSpec((None, s, nh * V_HEAD), lambda i, j: (i, 0, j))
    return pl.pallas_call(
        _attn_prompt_kernel,
        grid=(b, MLA_HEADS // nh),
        in_specs=[pl.BlockSpec((None, s, nh * QK_PAD), lambda i, j: (i, 0, j)),
                  pl.BlockSpec((nh * QK_PAD, s), lambda i, j: (j, i)), hv, hv],
        out_specs=hv,
        out_shape=jax.ShapeDtypeStruct((b, s, D_ATTN), BF16),
        compiler_params=pltpu.CompilerParams(dimension_semantics=("arbitrary", "arbitrary"),
                                             vmem_limit_bytes=VMEM_LIMIT_BYTES),
        name="attn_prompt",
    )(q3, kt, v3, z3)


def _dt_lanes(shape):
    lane = lax.broadcasted_iota(jnp.int32, shape, 1)
    return jnp.logical_and(lane >= DT_LANE0, lane < DT_LANE0 + SSD_HEADS)


def _dt_and_da(misc, dtb, alog):
    dt = jnp.where(_dt_lanes(misc.shape), _softplus(misc + dtb), 0.0)
    return dt, dt * (-jnp.exp(alog))


def _split_bf16(x):
    pieces = []
    for _ in range(SPLIT_PIECES):
        piece = x.astype(BF16).astype(F32)
        pieces.append(piece)
        x = x - piece
    return pieces


def _cumsum_rows(lower_b, x):
    return sum(_dot(lower_b, piece.astype(BF16)) for piece in _split_bf16(x))


def _head_expand(x, expand_ref):
    pieces = _split_bf16(jnp.where(_dt_lanes(x.shape), x, 0.0))
    packed = pieces[0]
    for k in range(1, SPLIT_PIECES):
        packed = packed + pltpu.roll(pieces[k], k * SSD_HEADS, 1)
    return _dot(packed.astype(BF16), expand_ref[...])


def _ssd_finish(y, xs, z, dskip, normw):
    y = y + dskip * xs
    gated = y * _silu(z)
    outs = []
    for g in range(SSD_GROUPS):
        sl = slice(g * GROUP_WIDTH, (g + 1) * GROUP_WIDTH)
        outs.append(_rms(gated[:, sl], normw[:, sl]))
    return jnp.concatenate(outs, axis=1).astype(BF16)


def _out_proj_attn(mix_a, wout_ref):
    return _dot(mix_a, wout_ref[0:D_ATTN, :])


def _out_proj(o_attn, mix_s, wout_ref, npost, x):
    o = o_attn + _dot(mix_s, wout_ref[D_ATTN:D_MIX, :])
    return x + _rms(o, npost)


def _ssd_chunk(act, misc, dtb_ref, alog_ref, expand_ref, state_ref):
    xs = act[:, :D_SSM]
    dt, da = _dt_and_da(misc, dtb_ref[...], alog_ref[...])
    row = lax.broadcasted_iota(jnp.int32, (CHUNK, CHUNK), 0)
    col = lax.broadcasted_iota(jnp.int32, (CHUNK, CHUNK), 1)
    lower = row >= col
    a_cum = _cumsum_rows(lower.astype(BF16), da)
    a_cum_t = a_cum.T
    a_last = a_cum[CHUNK - 1:CHUNK, :]
    xdt = xs * _head_expand(dt, expand_ref)
    decay_out = _head_expand(jnp.exp(a_cum), expand_ref)
    state_decay = decay_out[CHUNK - 1:CHUNK, :]
    xdt_end = (xdt * _head_expand(jnp.exp(a_last - a_cum), expand_ref)).astype(BF16)
    xdt_b = xdt.astype(BF16)
    lane_w = lax.broadcasted_iota(jnp.int32, (CHUNK, LANES), 1)
    first_head = lane_w < SSD_HEADDIM

    ys = []
    for g in range(SSD_GROUPS):
        bm = act[:, D_SSM + g * D_STATE:D_SSM + (g + 1) * D_STATE]
        cm = act[:, D_SSM + (SSD_GROUPS + g) * D_STATE:D_SSM + (SSD_GROUPS + g + 1) * D_STATE]
        bm_b = bm.astype(BF16)
        cm_b = cm.astype(BF16)
        cb = lax.dot_general(cm_b, bm_b, _NT, preferred_element_type=F32)
        gsl = slice(g * GROUP_WIDTH, (g + 1) * GROUP_WIDTH)
        state = state_ref[g]
        y_off = _dot(cm_b, state.astype(BF16)) * decay_out[:, gsl]
        y_diag = []
        for pair in range(HEADS_PER_GROUP // 2):
            halves = []
            x_pair = xdt_b[:, g * GROUP_WIDTH + pair * LANES: g * GROUP_WIDTH + (pair + 1) * LANES]
            for k in range(2):
                lane_h = DT_LANE0 + g * HEADS_PER_GROUP + 2 * pair + k
                seg = a_cum[:, lane_h:lane_h + 1] - a_cum_t[lane_h:lane_h + 1, :]
                decay = jnp.exp(jnp.where(lower, seg, NEG_BIG))
                halves.append(_dot((cb * decay).astype(BF16), x_pair))
            y_diag.append(jnp.where(first_head, halves[0], halves[1]))
        ys.append(jnp.concatenate(y_diag, axis=1) + y_off)
        state_ref[g] = state * state_decay[:, gsl] + _dot(bm.T.astype(BF16), xdt_end[:, gsl])
    return jnp.concatenate(ys, axis=1)


def _merge_kernel(ma_ref, ms_ref, x_ref, wout_ref, npost_ref, y_ref):
    y_ref[...] = _out_proj(_out_proj_attn(ma_ref[...], wout_ref), ms_ref[...], wout_ref, npost_ref[...], x_ref[...])


def _merge(mix_a, mix_s, x2, w):
    n = x2.shape[0]
    tm = MERGE_ROWS
    rows = lambda width: pl.BlockSpec((tm, width), lambda i: (i, 0))
    return pl.pallas_call(
        _merge_kernel,
        grid=(n // tm,),
        in_specs=[rows(D_ATTN), rows(D_SSM), rows(D_MODEL), _full((D_MIX, D_MODEL)), _full((1, D_MODEL))],
        out_specs=rows(D_MODEL),
        out_shape=jax.ShapeDtypeStruct((n, D_MODEL), F32),
        compiler_params=pltpu.CompilerParams(dimension_semantics=("arbitrary",),
                                             vmem_limit_bytes=VMEM_LIMIT_BYTES),
        name="merge",
    )(mix_a, mix_s, x2, w['w_out'], w['norm_post'])


def _attn_sample_kernel(layer, pt_ref, qlat_ref, qrope_ref, cnew_ref, rnew_ref, cache_c_ref, cache_rt_ref,
                        o_ref, cbuf, rbuf, sem):
    i = pl.program_id(0)
    n = pl.num_programs(0)
    n_pages = pt_ref.shape[1]
    seqs = qlat_ref.shape[0]
    past = n_pages * PAGE_SIZE
    slot = i % 2

    def page_copies(step, slot_):
        copies = []
        for j in range(seqs):
            buf = slot_ * seqs + j
            for p in range(n_pages):
                page = pt_ref[step * seqs + j, p]
                dst = pl.ds(p * PAGE_SIZE, PAGE_SIZE)
                copies.append(pltpu.make_async_copy(cache_c_ref.at[layer, page], cbuf.at[buf, dst],
                                                    sem.at[0, slot_]))
                copies.append(pltpu.make_async_copy(cache_rt_ref.at[layer, page], rbuf.at[buf, :, dst],
                                                    sem.at[1, slot_]))
        return copies

    @pl.when(i == 0)
    def _():
        for cp in page_copies(0, 0):
            cp.start()

    @pl.when(i + 1 < n)
    def _():
        for cp in page_copies(i + 1, 1 - slot):
            cp.start()

    for cp in page_copies(i, slot):
        cp.wait()

    q_pos = past
    chunks = [pl.ds(c * DEC_KV_CHUNK, DEC_KV_CHUNK) for c in range(past // DEC_KV_CHUNK)]
    for j in range(seqs):
        buf = slot * seqs + j
        qlat = qlat_ref[j]
        qrope = qrope_ref[j]
        cnew = cnew_ref[j]
        rnew = rnew_ref[j]
        s_new = (jnp.sum(qlat * cnew, axis=-1, keepdims=True)
                 + jnp.sum(qrope * rnew, axis=-1, keepdims=True))
        s = jnp.concatenate(
            [lax.dot_general(qlat, cbuf[buf, keys, :], _NT, preferred_element_type=F32)
             + _dot(qrope, rbuf[buf, :, keys]) for keys in chunks], axis=1)
        k_pos = lax.broadcasted_iota(jnp.int32, s.shape, 1)
        s = jnp.where(k_pos <= q_pos, s, NEG_BIG)
        m = jnp.maximum(jnp.max(s, axis=-1, keepdims=True), s_new)
        p = jnp.exp(s - m)
        p_new = jnp.exp(s_new - m)
        l = jnp.sum(p, axis=-1, keepdims=True) + p_new
        acc = p_new * cnew
        for c, keys in enumerate(chunks):
            acc = acc + _dot(p[:, c * DEC_KV_CHUNK:(c + 1) * DEC_KV_CHUNK], cbuf[buf, keys, :])
        o_ref[j] = acc / l


def _attn_sample(layer, page_table, qlat3, qrope3, cnew3, rnew3, cache_c, cache_rt):
    b, n_pages = page_table.shape
    past = n_pages * PAGE_SIZE
    seqs = DEC_ATTN_SEQS
    assert past % DEC_KV_CHUNK == 0 and b % seqs == 0
    per_step = lambda d1, d2: pl.BlockSpec((seqs, d1, d2), lambda i, pt: (i, 0, 0))
    grid_spec = pltpu.PrefetchScalarGridSpec(
        num_scalar_prefetch=1,
        grid=(b // seqs,),
        in_specs=[per_step(MLA_HEADS, KV_RANK), per_step(MLA_HEADS, QK_ROPE), per_step(1, KV_RANK),
                  per_step(1, QK_ROPE), pl.BlockSpec(memory_space=pl.ANY), pl.BlockSpec(memory_space=pl.ANY)],
        out_specs=per_step(MLA_HEADS, KV_RANK),
        scratch_shapes=[pltpu.VMEM((2 * seqs, past, KV_RANK), F32), pltpu.VMEM((2 * seqs, QK_ROPE, past), F32),
                        pltpu.SemaphoreType.DMA((2, 2))],
    )
    return pl.pallas_call(
        functools.partial(_attn_sample_kernel, layer),
        grid_spec=grid_spec,
        out_shape=jax.ShapeDtypeStruct((b, MLA_HEADS, KV_RANK), F32),
        compiler_params=pltpu.CompilerParams(dimension_semantics=("arbitrary",),
                                             vmem_limit_bytes=VMEM_LIMIT_BYTES),
        name="attn_sample",
    )(page_table, qlat3, qrope3, cnew3, rnew3, cache_c, cache_rt)


def _post_sample_kernel(olat_ref, wuv_ref, za_ref, xbc_ref, misc_ref, cprev_ref, cw_ref, cb_ref,
                        dtb_ref, alog_ref, expand_ref,
                        mixa_ref, cnew_ref, act_ref, xdt_t_ref, dec_t_ref):
    for hh in range(MLA_HEADS):
        o = _dot(olat_ref[:, hh * KV_RANK:(hh + 1) * KV_RANK].astype(BF16), wuv_ref[hh])
        vs = slice(hh * V_HEAD, (hh + 1) * V_HEAD)
        mixa_ref[:, vs] = (o * _silu(za_ref[:, vs])).astype(BF16)

    xb = xbc_ref[...]
    acc = cb_ref[...]
    for k in range(CONV_W - 1):
        acc = acc + cprev_ref[k] * cw_ref[k:k + 1, :]
    acc = acc + xb * cw_ref[CONV_W - 1:CONV_W, :]
    act = _silu(acc)
    act_ref[...] = act
    for k in range(CONV_W - 2):
        cnew_ref[k] = cprev_ref[k + 1]
    cnew_ref[CONV_W - 2] = xb

    dt, da = _dt_and_da(misc_ref[...], dtb_ref[...], alog_ref[...])
    xdt = act[:, :D_SSM] * _head_expand(dt, expand_ref)
    decay = _head_expand(jnp.exp(da), expand_ref)
    xdt_t_ref[...] = xdt.T
    dec_t_ref[...] = decay.T


def _post_sample(olat2, za, xbc, misc, cprev3, w):
    n = olat2.shape[0]
    full = lambda a: _full(a.shape)
    args = (olat2, w['w_uvh'], za, xbc, misc, cprev3, w['conv_w'], w['conv_b'], w['dt_bias_t'], w['a_log_t'],
            w['expand'])
    out_shape = (
        jax.ShapeDtypeStruct((n, D_ATTN), BF16),
        jax.ShapeDtypeStruct((CONV_W - 1, n, CONV_DIM), F32),
        jax.ShapeDtypeStruct((n, CONV_DIM), F32),
        jax.ShapeDtypeStruct((D_SSM, n), F32),
        jax.ShapeDtypeStruct((D_SSM, n), F32),
    )
    return pl.pallas_call(
        _post_sample_kernel,
        grid=(1,),
        in_specs=[full(a) for a in args],
        out_specs=tuple(_full(s.shape) for s in out_shape),
        out_shape=out_shape,
        compiler_params=pltpu.CompilerParams(dimension_semantics=("arbitrary",),
                                             vmem_limit_bytes=VMEM_LIMIT_BYTES),
        name="post_sample",
    )(*args)


def _state_sample_kernel(h0_ref, xdt_t_ref, dec_t_ref, bm_ref, cm_ref, h_ref, y_t_ref):
    t = pl.program_id(0)
    tb = h0_ref.shape[0]
    rows, n_seq = xdt_t_ref.shape

    @pl.when(t == 0)
    def _():
        y_t_ref[...] = jnp.zeros(y_t_ref.shape, F32)

    lane = lax.broadcasted_iota(jnp.int32, (rows, n_seq), 1)
    xdt_t = xdt_t_ref[...]
    dec_t = dec_t_ref[...]
    for j in range(tb):
        own = lane == t * tb + j
        x_col = jnp.sum(jnp.where(own, xdt_t, 0.0), axis=1, keepdims=True)
        d_col = jnp.sum(jnp.where(own, dec_t, 0.0), axis=1, keepdims=True)
        b_rows = jnp.concatenate(
            [jnp.broadcast_to(bm_ref[j:j + 1, g * D_STATE:(g + 1) * D_STATE], (GROUP_WIDTH, D_STATE))
             for g in range(SSD_GROUPS)], axis=0)
        c_rows = jnp.concatenate(
            [jnp.broadcast_to(cm_ref[j:j + 1, g * D_STATE:(g + 1) * D_STATE], (GROUP_WIDTH, D_STATE))
             for g in range(SSD_GROUPS)], axis=0)
        h = d_col * h0_ref[j] + x_col * b_rows
        h_ref[j] = h
        y_col = jnp.sum(h * c_rows, axis=1, keepdims=True)
        y_t_ref[...] = jnp.where(own, y_col, y_t_ref[...])


def _state_sample(h0, xdt_t, dec_t, act):
    n = h0.shape[0]
    tb = DEC_STATE_SEQS
    bc_w = SSD_GROUPS * D_STATE
    st = pl.BlockSpec((tb, D_SSM, D_STATE), lambda i: (i, 0, 0))
    return pl.pallas_call(
        _state_sample_kernel,
        grid=(n // tb,),
        in_specs=[st, _full((D_SSM, n)), _full((D_SSM, n)),
                  pl.BlockSpec((tb, bc_w), lambda i: (i, D_SSM // bc_w)),
                  pl.BlockSpec((tb, bc_w), lambda i: (i, D_SSM // bc_w + 1))],
        out_specs=(st, _full((D_SSM, n))),
        out_shape=(jax.ShapeDtypeStruct(h0.shape, F32), jax.ShapeDtypeStruct((D_SSM, n), F32)),
        compiler_params=pltpu.CompilerParams(dimension_semantics=("arbitrary",),
                                             vmem_limit_bytes=VMEM_LIMIT_BYTES),
        name="state_sample",
    )(h0, xdt_t, dec_t, act, act)


def _finish_sample_kernel(y_t_ref, act_ref, z_ref, dskip_ref, normw_ref, ma_ref, x_ref, wout_ref, npost_ref,
                          y_ref):
    mix_s = _ssd_finish(y_t_ref[...].T, act_ref[...], z_ref[...], dskip_ref[...], normw_ref[...])
    y_ref[...] = _out_proj(_out_proj_attn(ma_ref[...], wout_ref), mix_s, wout_ref, npost_ref[...], x_ref[...])


def _finish_sample(y_t, act, zs, mix_a, x2, w):
    n = x2.shape[0]
    return pl.pallas_call(
        _finish_sample_kernel,
        grid=(1,),
        in_specs=[_full(y_t.shape), pl.BlockSpec((n, D_SSM), lambda i: (0, 0)), _full(zs.shape),
                  _full((1, D_SSM)), _full((1, D_SSM)), _full(mix_a.shape), _full(x2.shape),
                  _full((D_MIX, D_MODEL)), _full((1, D_MODEL))],
        out_specs=_full((n, D_MODEL)),
        out_shape=jax.ShapeDtypeStruct((n, D_MODEL), F32),
        compiler_params=pltpu.CompilerParams(dimension_semantics=("arbitrary",),
                                             vmem_limit_bytes=VMEM_LIMIT_BYTES),
        name="finish_sample",
    )(y_t, act, zs, w['d_skip_w'], w['ssm_norm'], mix_a, x2, w['w_out'], w['norm_post'])


ORIG_KR = Q_RANK + KV_RANK
ORIG_ZA = ORIG_KR + QK_ROPE
ORIG_DT = ORIG_ZA + D_ATTN + D_SSM + CONV_DIM
PREP_COLS = 128


def _prep_w_in_kernel(wt_ref, o_ref):
    def put(tile, src_rows):
        o_ref[:, tile * LANES:(tile + 1) * LANES] = src_rows.T.astype(BF16)

    for t in range(COL_ZA // LANES):
        put(t, wt_ref[t * LANES:(t + 1) * LANES, :])
    for t in range((COL_MISC - COL_ZA) // LANES):
        put(COL_ZA // LANES + t, wt_ref[ORIG_ZA + t * LANES:ORIG_ZA + (t + 1) * LANES, :])
    pad = jnp.zeros((LANES - QK_ROPE - SSD_HEADS, wt_ref.shape[1]), F32)
    put(COL_MISC // LANES, jnp.concatenate([wt_ref[ORIG_KR:ORIG_ZA, :], wt_ref[ORIG_DT:ORIG_DT + SSD_HEADS, :], pad],
                                           axis=0))


def _prep_w_in(w_in):
    k, n = w_in.shape
    assert n == ORIG_DT + SSD_HEADS and k % PREP_COLS == 0
    return pl.pallas_call(
        _prep_w_in_kernel,
        grid=(k // PREP_COLS,),
        in_specs=[pl.BlockSpec((n, PREP_COLS), lambda i: (0, i))],
        out_specs=pl.BlockSpec((PREP_COLS, D_IN_PAD), lambda i: (i, 0)),
        out_shape=jax.ShapeDtypeStruct((k, D_IN_PAD), BF16),
        compiler_params=pltpu.CompilerParams(dimension_semantics=("arbitrary",),
                                             vmem_limit_bytes=VMEM_LIMIT_BYTES),
        name="prep_w_in",
    )(w_in.T)


def _rope_tables(pos):
    inv_freq = ROPE_THETA ** (-jnp.arange(ROPE_HALF, dtype=F32) / ROPE_HALF)
    ang = pos.astype(F32)[:, None] * inv_freq[None, :]
    cos, sin = jnp.cos(ang), jnp.sin(ang)
    reps = LANES // QK_ROPE
    return jnp.tile(cos, (1, 2 * reps)), jnp.tile(jnp.concatenate([-sin, sin], axis=1), (1, reps))


def _prep_weights(lw):
    row = lambda v: v.reshape(1, -1).astype(F32)
    w_qb = lw['w_q_b'].reshape(Q_RANK, MLA_HEADS, QK_NOPE + QK_ROPE)
    lane_pad = lambda v: jnp.pad(v.reshape(1, -1).astype(F32), ((0, 0), (DT_LANE0, LANES - DT_LANE0 - SSD_HEADS)))
    head_of_col = jnp.arange(D_SSM) // SSD_HEADDIM
    return {
        'norm_pre': row(lw['norm_pre']),
        'w_in': _prep_w_in(lw['w_in']),
        'q_a_norm': row(lw['q_a_norm']),
        'w_qb': jnp.concatenate([w_qb[:, :, :QK_NOPE].reshape(Q_RANK, -1),
                                 w_qb[:, :, QK_NOPE:].reshape(Q_RANK, -1)], axis=1).astype(BF16),
        'kv_a_norm': row(lw['kv_a_norm']),
        'w_uk2t': lw['w_uk'].reshape(KV_RANK, MLA_HEADS * QK_NOPE).T.astype(BF16),
        'w_uv2': lw['w_uv'].reshape(KV_RANK, MLA_HEADS * V_HEAD).astype(BF16),
        'w_ukt': jnp.transpose(lw['w_uk'], (1, 2, 0)).astype(BF16),
        'w_uvh': jnp.transpose(lw['w_uv'], (1, 0, 2)).astype(BF16),
        'conv_w': lw['conv_w'].astype(F32),
        'conv_b': row(lw['conv_b']),
        'dt_bias_t': lane_pad(lw['dt_bias']),
        'a_log_t': lane_pad(lw['a_log']),
        'expand': sum((jnp.arange(LANES)[:, None] == DT_LANE0 + k * SSD_HEADS + head_of_col[None, :])
                      for k in range(SPLIT_PIECES)).astype(BF16),
        'd_skip_w': jnp.repeat(lw['d_skip'].astype(F32), SSD_HEADDIM).reshape(1, D_SSM),
        'ssm_norm': row(lw['ssm_norm']),
        'w_out': lw['w_out'].astype(BF16),
        'norm_post': row(lw['norm_post']),
    }


def _prompt_layer(x, w):
    b, s, _ = x.shape
    x2 = x.reshape(b * s, D_MODEL)
    cos, sin = _rope_tables(jnp.arange(s, dtype=jnp.int32))
    q, kt, v, ckv, kr, za, mix_s, state, tail = _proj_prompt(x2, cos, sin, w)
    r3 = lambda a: a.reshape(b, s, a.shape[-1])
    mix_a = _attn_prompt(r3(q), kt, r3(v), r3(za))
    y = _merge(mix_a.reshape(b * s, D_ATTN), mix_s, x2, w)
    h = state.reshape(b, SSD_GROUPS, D_STATE, HEADS_PER_GROUP, SSD_HEADDIM)
    h = jnp.transpose(h, (0, 1, 3, 4, 2)).reshape(b, SSD_HEADS, SSD_HEADDIM, D_STATE)
    return (y.reshape(b, s, D_MODEL), r3(ckv), jnp.swapaxes(kr, 1, 2), tail[:, CONV_PAD - (CONV_W - 1):, :], h)


def _sample_layer(layer, x, cache_c, cache_r, conv_prev, h0, page_table, w):
    b, s, _ = x.shape
    n = b * s
    past = page_table.shape[1] * PAGE_SIZE
    x2 = x.reshape(n, D_MODEL)
    pos = past + jnp.arange(s, dtype=jnp.int32)
    cos, sin = _rope_tables(jnp.tile(pos, b))
    qlat, qrope, ckv, kr, za, zs, xbc, misc = _proj_sample(x2, cos, sin, w)
    olat = _attn_sample(layer, page_table, qlat.reshape(n, MLA_HEADS, KV_RANK), qrope.reshape(n, MLA_HEADS, QK_ROPE),
                        ckv.reshape(n, 1, KV_RANK), kr.reshape(n, 1, QK_ROPE), cache_c,
                        jnp.swapaxes(cache_r, 2, 3))
    mix_a, conv_new, act, xdt_t, dec_t = _post_sample(
        olat.reshape(n, MLA_HEADS * KV_RANK), za, xbc, misc, jnp.swapaxes(conv_prev, 0, 1), w)
    h, y_t = _state_sample(h0.astype(F32).reshape(n, D_SSM, D_STATE), xdt_t, dec_t, act)
    y = _finish_sample(y_t, act, zs, mix_a, x2, w)
    return (y.reshape(b, s, D_MODEL), ckv.reshape(b, s, KV_RANK), kr.reshape(b, s, QK_ROPE),
            jnp.swapaxes(conv_new, 0, 1), h.reshape(b, SSD_HEADS, SSD_HEADDIM, D_STATE))


def kernel(x_prompt, x_sample, cache_ckv, cache_krope, state_conv, state_ssm, page_table, norm_pre, w_in,
           q_a_norm, w_q_b, kv_a_norm, w_uk, w_uv, conv_w, conv_b, dt_bias, a_log, d_skip, ssm_norm, w_out,
           norm_post):
    assert x_sample.shape[1] == 1, "the sample path handles one new token per sequence"
    depth = w_in.shape[0]
    y_prompt, y_sample = x_prompt, x_sample
    outs = [[] for _ in range(8)]
    for l in range(depth):
        w = _prep_weights({'norm_pre': norm_pre[l], 'w_in': w_in[l], 'q_a_norm': q_a_norm[l],
                           'w_q_b': w_q_b[l], 'kv_a_norm': kv_a_norm[l], 'w_uk': w_uk[l], 'w_uv': w_uv[l],
                           'conv_w': conv_w[l], 'conv_b': conv_b[l], 'dt_bias': dt_bias[l], 'a_log': a_log[l],
                           'd_skip': d_skip[l], 'ssm_norm': ssm_norm[l], 'w_out': w_out[l],
                           'norm_post': norm_post[l]})
        y_prompt, c1, k1, v1, h1 = _prompt_layer(y_prompt, w)
        y_sample, c2, k2, v2, h2 = _sample_layer(l, y_sample, cache_ckv, cache_krope, state_conv[l],
                                                 state_ssm[l], page_table, w)
        for lst, val in zip(outs, (c1, k1, v1, h1, c2, k2, v2, h2)):
            lst.append(val)
    return (y_prompt, y_sample) + tuple(jnp.stack(o) for o in outs)
```

```python
import functools
import math

import jax
import jax.numpy as jnp
from jax import lax
from jax.experimental import pallas as pl
from jax.experimental.pallas import tpu as pltpu

F32 = jnp.float32
BF16 = jnp.bfloat16

D_MODEL = 1024
PAGE_SIZE = 128
D_MIX = 2 * D_MODEL
D_ATTN = D_MIX // 2
D_SSM = D_MIX - D_ATTN
MLA_HEADS = 8
QK_NOPE = 128
QK_ROPE = 64
ROPE_HALF = QK_ROPE // 2
V_HEAD = D_ATTN // MLA_HEADS
Q_RANK = 384
KV_RANK = 256
ROPE_THETA = 10000.0
SOFTMAX_SCALE = (QK_NOPE + QK_ROPE) ** -0.5
SSD_HEADDIM = 64
SSD_HEADS = D_SSM // SSD_HEADDIM
SSD_GROUPS = 2
HEADS_PER_GROUP = SSD_HEADS // SSD_GROUPS
GROUP_WIDTH = D_SSM // SSD_GROUPS
D_STATE = 128
CONV_W = 4
CONV_DIM = D_SSM + 2 * SSD_GROUPS * D_STATE
CHUNK = 128
EPS = 1e-6
SPLIT_PIECES = 3
NEG_BIG = -1e30

LANES = 128
SUBLANES = 8
VMEM_LIMIT_BYTES = 56 * 1024 * 1024

COL_Q = 0
COL_C = COL_Q + Q_RANK
COL_ZA = COL_C + KV_RANK
COL_ZS = COL_ZA + D_ATTN
COL_XBC = COL_ZS + D_SSM
COL_MISC = COL_XBC + CONV_DIM
D_IN_PAD = COL_MISC + LANES
DT_LANE0 = QK_ROPE
QK_PAD = 2 * LANES

PROJ_ROWS = 512
ATTN_TQ = 256
ATTN_HEADS = 4
SSD_STEP_CHUNKS = 4
MERGE_ROWS = 512
CONV_PAD = SUBLANES
DEC_STATE_SEQS = 8
DEC_KV_CHUNK = 2048
DEC_ATTN_SEQS = 2

_NT = (((1,), (1,)), ((), ()))


def _rms(x, w):
    return x * lax.rsqrt(jnp.mean(x * x, axis=-1, keepdims=True) + EPS) * w


def _silu(x):
    return x / (1.0 + jnp.exp(-x))


def _softplus(x):
    return jnp.maximum(x, 0.0) + jnp.log(1.0 + jnp.exp(-jnp.abs(x)))


def _dot(a, b):
    return jnp.dot(a, b, preferred_element_type=F32)


def _rope_tile(x, cos, sin_signed):
    lane = lax.broadcasted_iota(jnp.int32, x.shape, 1)
    first_half = (lane % QK_ROPE) < ROPE_HALF
    partner = jnp.where(first_half,
                        pltpu.roll(x, LANES - ROPE_HALF, 1),
                        pltpu.roll(x, ROPE_HALF, 1))
    return x * cos + partner * sin_signed


def _front(x_ref, npre_ref, win_ref):
    h = _rms(x_ref[...], npre_ref[...]).astype(BF16)

    def seg(lo, hi):
        return _dot(h, win_ref[:, lo:hi])

    return seg


def _proj_prompt_kernel(seq_tiles, x_ref, cos_ref, sin_ref, npre_ref, win_ref, qan_ref, wqb_ref, kvn_ref,
                        wukt_ref, wuv_ref, cw_ref, cb_ref, dtb_ref, alog_ref, expand_ref, dskip_ref, normw_ref,
                        q_ref, kt_ref, v_ref, ckv_ref, kr_ref, za_ref, mixs_ref, state_ref, tail_ref, xp_ref):
    rows = x_ref.shape[0]
    first = pl.program_id(0) % seq_tiles == 0

    @pl.when(first)
    def _():
        xp_ref[0:CONV_PAD, :] = jnp.zeros((CONV_PAD, CONV_DIM), F32)
        state_ref[...] = jnp.zeros(state_ref.shape, F32)

    seg = _front(x_ref, npre_ref, win_ref)
    xp_ref[CONV_PAD:CONV_PAD + rows, :] = seg(COL_XBC, COL_MISC)
    xp = xp_ref[...]
    acc = cb_ref[...]
    for tap in range(CONV_W):
        back = CONV_W - 1 - tap
        shifted = xp if back == 0 else pltpu.roll(xp, back, 0)
        acc = acc + shifted[CONV_PAD:CONV_PAD + rows, :] * cw_ref[tap:tap + 1, :]
    act_all = _silu(acc)
    tail_ref[...] = xp_ref[rows:rows + CONV_PAD, :]
    xp_ref[CONV_PAD - 3:CONV_PAD, :] = xp_ref[CONV_PAD + rows - 3:CONV_PAD + rows, :]

    misc = seg(COL_MISC, D_IN_PAD)
    cos = cos_ref[...]
    sin = sin_ref[...]
    lane = lax.broadcasted_iota(jnp.int32, misc.shape, 1)
    low = lane < QK_ROPE

    kr_full = _rope_tile(misc, cos, sin)
    kr_t = jnp.where(low, kr_full, 0.0).T
    kr_ref[...] = kr_t[:QK_ROPE, :]
    kr_lo_t = kr_t.astype(BF16)
    kr_hi_t = jnp.where(low, 0.0, pltpu.roll(kr_full, QK_ROPE, 1)).T.astype(BF16)

    c_raw = seg(COL_C, COL_ZA)
    q_a = seg(COL_Q, COL_C)
    za_ref[...] = seg(COL_ZA, COL_ZS)
    ckv = _rms(c_raw, kvn_ref[...])
    ckv_ref[...] = ckv
    cb = ckv.astype(BF16)
    knope_t = lax.dot_general(wukt_ref[...], cb, _NT, preferred_element_type=F32)
    v_ref[...] = _dot(cb, wuv_ref[...]).astype(BF16)

    qn = _rms(q_a, qan_ref[...]).astype(BF16)
    q = _dot(qn, wqb_ref[...])
    nope_w = MLA_HEADS * QK_NOPE
    for hh in range(MLA_HEADS):
        pair = hh // 2
        r = _rope_tile(q[:, nope_w + pair * LANES: nope_w + (pair + 1) * LANES], cos, sin)
        own = low if hh % 2 == 0 else jnp.logical_not(low)
        base = hh * QK_PAD
        q_ref[:, base:base + LANES] = (q[:, hh * QK_NOPE:(hh + 1) * QK_NOPE] * SOFTMAX_SCALE).astype(BF16)
        q_ref[:, base + LANES:base + QK_PAD] = (jnp.where(own, r, 0.0) * SOFTMAX_SCALE).astype(BF16)
        kt_ref[base:base + LANES, :] = knope_t[hh * QK_NOPE:(hh + 1) * QK_NOPE, :].astype(BF16)
        kt_ref[base + LANES:base + QK_PAD, :] = kr_lo_t if hh % 2 == 0 else kr_hi_t

    zs = seg(COL_ZS, COL_XBC)
    for k in range(rows // CHUNK):
        r0 = k * CHUNK
        act = act_all[r0:r0 + CHUNK, :]
        y = _ssd_chunk(act, misc[r0:r0 + CHUNK, :], dtb_ref, alog_ref, expand_ref, state_ref)
        mixs_ref[r0:r0 + CHUNK, :] = _ssd_finish(y, act[:, :D_SSM], zs[r0:r0 + CHUNK, :], dskip_ref[...],
                                                 normw_ref[...])


def _proj_sample_kernel(x_ref, cos_ref, sin_ref, npre_ref, win_ref, qan_ref, wqb_ref, kvn_ref, wukt_ref,
                        qlat_ref, qrope_ref, ckv_ref, kr_ref, za_ref, zs_ref, xbc_ref, misc_ref):
    seg = _front(x_ref, npre_ref, win_ref)
    za_ref[...] = seg(COL_ZA, COL_ZS)
    zs_ref[...] = seg(COL_ZS, COL_XBC)
    xbc_ref[...] = seg(COL_XBC, COL_MISC)
    misc = seg(COL_MISC, D_IN_PAD)
    misc_ref[...] = misc
    cos = cos_ref[...]
    sin = sin_ref[...]
    kr_ref[...] = _rope_tile(misc, cos, sin)[:, :QK_ROPE]
    ckv_ref[...] = _rms(seg(COL_C, COL_ZA), kvn_ref[...])

    qn = _rms(seg(COL_Q, COL_C), qan_ref[...]).astype(BF16)
    q = _dot(qn, wqb_ref[...])
    nope_w = MLA_HEADS * QK_NOPE
    for pair in range(MLA_HEADS // 2):
        lo = nope_w + pair * LANES
        qrope_ref[:, pair * LANES:(pair + 1) * LANES] = _rope_tile(q[:, lo:lo + LANES], cos, sin) * SOFTMAX_SCALE
    for hh in range(MLA_HEADS):
        qh = q[:, hh * QK_NOPE:(hh + 1) * QK_NOPE].astype(BF16)
        qlat_ref[:, hh * KV_RANK:(hh + 1) * KV_RANK] = _dot(qh, wukt_ref[hh]) * SOFTMAX_SCALE


def _full(shape):
    return pl.BlockSpec(shape, lambda *_: (0,) * len(shape))


def _proj_prompt(x2, cos, sin, w):
    n = x2.shape[0]
    tm = PROJ_ROWS
    seq_tiles = cos.shape[0] // tm
    assert tm == SSD_STEP_CHUNKS * CHUNK
    batch = n // cos.shape[0]
    rows = lambda width: pl.BlockSpec((tm, width), lambda i: (i, 0))
    tab = pl.BlockSpec((tm, LANES), lambda i: (i % seq_tiles, 0))
    out_shape = (
        jax.ShapeDtypeStruct((n, MLA_HEADS * QK_PAD), BF16),
        jax.ShapeDtypeStruct((MLA_HEADS * QK_PAD, n), BF16),
        jax.ShapeDtypeStruct((n, D_ATTN), BF16),
        jax.ShapeDtypeStruct((n, KV_RANK), F32),
        jax.ShapeDtypeStruct((batch, QK_ROPE, cos.shape[0]), F32),
        jax.ShapeDtypeStruct((n, D_ATTN), F32),
        jax.ShapeDtypeStruct((n, D_SSM), BF16),
        jax.ShapeDtypeStruct((batch, SSD_GROUPS, D_STATE, GROUP_WIDTH), F32),
        jax.ShapeDtypeStruct((batch, CONV_PAD, CONV_DIM), F32),
    )
    per_seq = lambda shape: pl.BlockSpec((None,) + shape[1:], lambda i: (i // seq_tiles,) + (0,) * (len(shape) - 1))
    out_specs = [rows(s.shape[1]) for s in out_shape[:7]]
    out_specs[1] = pl.BlockSpec((MLA_HEADS * QK_PAD, tm), lambda i: (0, i))
    out_specs[4] = pl.BlockSpec((None, QK_ROPE, tm), lambda i: (i // seq_tiles, 0, i % seq_tiles))
    out_specs += [per_seq(out_shape[7].shape), per_seq(out_shape[8].shape)]
    return pl.pallas_call(
        functools.partial(_proj_prompt_kernel, seq_tiles),
        grid=(n // tm,),
        in_specs=[rows(D_MODEL), tab, tab, _full((1, D_MODEL)), _full((D_MODEL, D_IN_PAD)),
                  _full((1, Q_RANK)), _full(w['w_qb'].shape), _full((1, KV_RANK)),
                  _full(w['w_uk2t'].shape), _full(w['w_uv2'].shape),
                  _full((CONV_W, CONV_DIM)), _full((1, CONV_DIM)), _full((1, LANES)), _full((1, LANES)),
                  _full((LANES, D_SSM)), _full((1, D_SSM)), _full((1, D_SSM))],
        out_specs=tuple(out_specs),
        out_shape=out_shape,
        scratch_shapes=[pltpu.VMEM((CONV_PAD + tm, CONV_DIM), F32)],
        compiler_params=pltpu.CompilerParams(dimension_semantics=("arbitrary",),
                                             vmem_limit_bytes=VMEM_LIMIT_BYTES),
        name="proj_prompt",
    )(x2, cos, sin, w['norm_pre'], w['w_in'], w['q_a_norm'], w['w_qb'], w['kv_a_norm'],
      w['w_uk2t'], w['w_uv2'], w['conv_w'], w['conv_b'], w['dt_bias_t'], w['a_log_t'], w['expand'],
      w['d_skip_w'], w['ssm_norm'])


def _proj_sample(x2, cos, sin, w):
    n = x2.shape[0]
    rows = lambda width: pl.BlockSpec((n, width), lambda i: (0, 0))
    out_shape = (
        jax.ShapeDtypeStruct((n, MLA_HEADS * KV_RANK), F32),
        jax.ShapeDtypeStruct((n, MLA_HEADS * QK_ROPE), F32),
        jax.ShapeDtypeStruct((n, KV_RANK), F32),
        jax.ShapeDtypeStruct((n, QK_ROPE), F32),
        jax.ShapeDtypeStruct((n, D_ATTN), F32),
        jax.ShapeDtypeStruct((n, D_SSM), F32),
        jax.ShapeDtypeStruct((n, CONV_DIM), F32),
        jax.ShapeDtypeStruct((n, LANES), F32),
    )
    return pl.pallas_call(
        _proj_sample_kernel,
        grid=(1,),
        in_specs=[rows(D_MODEL), rows(LANES), rows(LANES), _full((1, D_MODEL)), _full((D_MODEL, D_IN_PAD)),
                  _full((1, Q_RANK)), _full(w['w_qb'].shape), _full((1, KV_RANK)),
                  _full(w['w_ukt'].shape)],
        out_specs=tuple(rows(s.shape[1]) for s in out_shape),
        out_shape=out_shape,
        compiler_params=pltpu.CompilerParams(dimension_semantics=("arbitrary",),
                                             vmem_limit_bytes=VMEM_LIMIT_BYTES),
        name="proj_sample",
    )(x2, cos, sin, w['norm_pre'], w['w_in'], w['q_a_norm'], w['w_qb'], w['kv_a_norm'], w['w_ukt'])


def _attn_prompt_kernel(q_ref, kt_ref, v_ref, z_ref, o_ref):
    seq = q_ref.shape[0]
    tq = ATTN_TQ
    row = lax.broadcasted_iota(jnp.int32, (tq, tq), 0)
    col = lax.broadcasted_iota(jnp.int32, (tq, tq), 1)
    causal = col <= row
    for qi in reversed(range(seq // tq)):
        lim = (qi + 1) * tq
        ones_col = (lax.broadcasted_iota(jnp.int32, (lim, LANES), 1) == 0).astype(BF16)
        for hd in range(ATTN_HEADS):
            qk = slice(hd * QK_PAD, (hd + 1) * QK_PAD)
            hv = slice(hd * V_HEAD, (hd + 1) * V_HEAD)
            s = _dot(q_ref[qi * tq:lim, qk], kt_ref[qk, 0:lim])
            diag = jnp.where(causal, s[:, lim - tq:], NEG_BIG)
            s = diag if qi == 0 else jnp.concatenate([s[:, :lim - tq], diag], axis=1)
            m = jnp.max(s, axis=-1, keepdims=True)
            p = jnp.exp(s - m).astype(BF16)
            ol = _dot(p, jnp.concatenate([v_ref[0:lim, hv], ones_col], axis=1))
            o = ol[:, :V_HEAD] / jnp.sum(ol[:, V_HEAD:], axis=-1, keepdims=True)
            o_ref[qi * tq:lim, hv] = (o * _silu(z_ref[qi * tq:lim, hv])).astype(BF16)


def _attn_prompt(q3, kt, v3, z3):
    b, s, _ = q3.shape
    nh = ATTN_HEADS
    hv = pl.BlockSpec((None, s, nh * V_HEAD), lambda i, j: (i, 0, j))
    return pl.pallas_call(
        _attn_prompt_kernel,
        grid=(b, MLA_HEADS // nh),
        in_specs=[pl.BlockSpec((None, s, nh * QK_PAD), lambda i, j: (i, 0, j)),
                  pl.BlockSpec((nh * QK_PAD, s), lambda i, j: (j, i)), hv, hv],
        out_specs=hv,
        out_shape=jax.ShapeDtypeStruct((b, s, D_ATTN), BF16),
        compiler_params=pltpu.CompilerParams(dimension_semantics=("arbitrary", "arbitrary"),
                                             vmem_limit_bytes=VMEM_LIMIT_BYTES),
        name="attn_prompt",
    )(q3, kt, v3, z3)


def _dt_lanes(shape):
    lane = lax.broadcasted_iota(jnp.int32, shape, 1)
    return jnp.logical_and(lane >= DT_LANE0, lane < DT_LANE0 + SSD_HEADS)


def _dt_and_da(misc, dtb, alog):
    dt = jnp.where(_dt_lanes(misc.shape), _softplus(misc + dtb), 0.0)
    return dt, dt * (-jnp.exp(alog))


def _split_bf16(x):
    pieces = []
    for _ in range(SPLIT_PIECES):
        piece = x.astype(BF16).astype(F32)
        pieces.append(piece)
        x = x - piece
    return pieces


def _cumsum_rows(lower_b, x):
    return sum(_dot(lower_b, piece.astype(BF16)) for piece in _split_bf16(x))


def _head_expand(x, expand_ref):
    pieces = _split_bf16(jnp.where(_dt_lanes(x.shape), x, 0.0))
    packed = pieces[0]
    for k in range(1, SPLIT_PIECES):
        packed = packed + pltpu.roll(pieces[k], k * SSD_HEADS, 1)
    return _dot(packed.astype(BF16), expand_ref[...])


def _ssd_finish(y, xs, z, dskip, normw):
    y = y + dskip * xs
    gated = y * _silu(z)
    outs = []
    for g in range(SSD_GROUPS):
        sl = slice(g * GROUP_WIDTH, (g + 1) * GROUP_WIDTH)
        outs.append(_rms(gated[:, sl], normw[:, sl]))
    return jnp.concatenate(outs, axis=1).astype(BF16)


def _out_proj_attn(mix_a, wout_ref):
    return _dot(mix_a, wout_ref[0:D_ATTN, :])


def _out_proj(o_attn, mix_s, wout_ref, npost, x):
    o = o_attn + _dot(mix_s, wout_ref[D_ATTN:D_MIX, :])
    return x + _rms(o, npost)


def _ssd_chunk(act, misc, dtb_ref, alog_ref, expand_ref, state_ref):
    xs = act[:, :D_SSM]
    dt, da = _dt_and_da(misc, dtb_ref[...], alog_ref[...])
    row = lax.broadcasted_iota(jnp.int32, (CHUNK, CHUNK), 0)
    col = lax.broadcasted_iota(jnp.int32, (CHUNK, CHUNK), 1)
    lower = row >= col
    a_cum = _cumsum_rows(lower.astype(BF16), da)
    a_cum_t = a_cum.T
    a_last = a_cum[CHUNK - 1:CHUNK, :]
    xdt = xs * _head_expand(dt, expand_ref)
    decay_out = _head_expand(jnp.exp(a_cum), expand_ref)
    state_decay = decay_out[CHUNK - 1:CHUNK, :]
    xdt_end = (xdt * _head_expand(jnp.exp(a_last - a_cum), expand_ref)).astype(BF16)
    xdt_b = xdt.astype(BF16)
    lane_w = lax.broadcasted_iota(jnp.int32, (CHUNK, LANES), 1)
    first_head = lane_w < SSD_HEADDIM

    ys = []
    for g in range(SSD_GROUPS):
        bm = act[:, D_SSM + g * D_STATE:D_SSM + (g + 1) * D_STATE]
        cm = act[:, D_SSM + (SSD_GROUPS + g) * D_STATE:D_SSM + (SSD_GROUPS + g + 1) * D_STATE]
        bm_b = bm.astype(BF16)
        cm_b = cm.astype(BF16)
        cb = lax.dot_general(cm_b, bm_b, _NT, preferred_element_type=F32)
        gsl = slice(g * GROUP_WIDTH, (g + 1) * GROUP_WIDTH)
        state = state_ref[g]
        y_off = _dot(cm_b, state.astype(BF16)) * decay_out[:, gsl]
        y_diag = []
        for pair in range(HEADS_PER_GROUP // 2):
            halves = []
            x_pair = xdt_b[:, g * GROUP_WIDTH + pair * LANES: g * GROUP_WIDTH + (pair + 1) * LANES]
            for k in range(2):
                lane_h = DT_LANE0 + g * HEADS_PER_GROUP + 2 * pair + k
                seg = a_cum[:, lane_h:lane_h + 1] - a_cum_t[lane_h:lane_h + 1, :]
                decay = jnp.exp(jnp.where(lower, seg, NEG_BIG))
                halves.append(_dot((cb * decay).astype(BF16), x_pair))
            y_diag.append(jnp.where(first_head, halves[0], halves[1]))
        ys.append(jnp.concatenate(y_diag, axis=1) + y_off)
        state_ref[g] = state * state_decay[:, gsl] + _dot(bm.T.astype(BF16), xdt_end[:, gsl])
    return jnp.concatenate(ys, axis=1)


def _merge_kernel(ma_ref, ms_ref, x_ref, wout_ref, npost_ref, y_ref):
    y_ref[...] = _out_proj(_out_proj_attn(ma_ref[...], wout_ref), ms_ref[...], wout_ref, npost_ref[...], x_ref[...])


def _merge(mix_a, mix_s, x2, w):
    n = x2.shape[0]
    tm = MERGE_ROWS
    rows = lambda width: pl.BlockSpec((tm, width), lambda i: (i, 0))
    return pl.pallas_call(
        _merge_kernel,
        grid=(n // tm,),
        in_specs=[rows(D_ATTN), rows(D_SSM), rows(D_MODEL), _full((D_MIX, D_MODEL)), _full((1, D_MODEL))],
        out_specs=rows(D_MODEL),
        out_shape=jax.ShapeDtypeStruct((n, D_MODEL), F32),
        compiler_params=pltpu.CompilerParams(dimension_semantics=("arbitrary",),
                                             vmem_limit_bytes=VMEM_LIMIT_BYTES),
        name="merge",
    )(mix_a, mix_s, x2, w['w_out'], w['norm_post'])


def _attn_sample_kernel(layer, pt_ref, qlat_ref, qrope_ref, cnew_ref, rnew_ref, cache_c_ref, cache_rt_ref,
                        o_ref, cbuf, rbuf, sem):
    i = pl.program_id(0)
    n = pl.num_programs(0)
    n_pages = pt_ref.shape[1]
    seqs = qlat_ref.shape[0]
    past = n_pages * PAGE_SIZE
    slot = i % 2

    def page_copies(step, slot_):
        copies = []
        for j in range(seqs):
            buf = slot_ * seqs + j
            for p in range(n_pages):
                page = pt_ref[step * seqs + j, p]
                dst = pl.ds(p * PAGE_SIZE, PAGE_SIZE)
                copies.append(pltpu.make_async_copy(cache_c_ref.at[layer, page], cbuf.at[buf, dst],
                                                    sem.at[0, slot_]))
                copies.append(pltpu.make_async_copy(cache_rt_ref.at[layer, page], rbuf.at[buf, :, dst],
                                                    sem.at[1, slot_]))
        return copies

    def start_all(copies):
        for idx, cp in enumerate(copies):
            cp.start(priority=(idx // 2 + idx) % 2)

    @pl.when(i == 0)
    def _():
        start_all(page_copies(0, 0))

    @pl.when(i + 1 < n)
    def _():
        start_all(page_copies(i + 1, 1 - slot))

    for cp in page_copies(i, slot):
        cp.wait()

    q_pos = past
    chunks = [pl.ds(c * DEC_KV_CHUNK, DEC_KV_CHUNK) for c in range(past // DEC_KV_CHUNK)]
    for j in range(seqs):
        buf = slot * seqs + j
        qlat = qlat_ref[j]
        qrope = qrope_ref[j]
        cnew = cnew_ref[j]
        rnew = rnew_ref[j]
        s_new = (jnp.sum(qlat * cnew, axis=-1, keepdims=True)
                 + jnp.sum(qrope * rnew, axis=-1, keepdims=True))
        s = jnp.concatenate(
            [lax.dot_general(qlat, cbuf[buf, keys, :], _NT, preferred_element_type=F32)
             + _dot(qrope, rbuf[buf, :, keys]) for keys in chunks], axis=1)
        k_pos = lax.broadcasted_iota(jnp.int32, s.shape, 1)
        s = jnp.where(k_pos <= q_pos, s, NEG_BIG)
        m = jnp.maximum(jnp.max(s, axis=-1, keepdims=True), s_new)
        p = jnp.exp(s - m)
        p_new = jnp.exp(s_new - m)
        l = jnp.sum(p, axis=-1, keepdims=True) + p_new
        acc = p_new * cnew
        for c, keys in enumerate(chunks):
            acc = acc + _dot(p[:, c * DEC_KV_CHUNK:(c + 1) * DEC_KV_CHUNK], cbuf[buf, keys, :])
        o_ref[j] = acc / l


def _attn_sample(layer, page_table, qlat3, qrope3, cnew3, rnew3, cache_c, cache_rt):
    b, n_pages = page_table.shape
    past = n_pages * PAGE_SIZE
    seqs = DEC_ATTN_SEQS
    assert past % DEC_KV_CHUNK == 0 and b % seqs == 0
    per_step = lambda d1, d2: pl.BlockSpec((seqs, d1, d2), lambda i, pt: (i, 0, 0))
    grid_spec = pltpu.PrefetchScalarGridSpec(
        num_scalar_prefetch=1,
        grid=(b // seqs,),
        in_specs=[per_step(MLA_HEADS, KV_RANK), per_step(MLA_HEADS, QK_ROPE), per_step(1, KV_RANK),
                  per_step(1, QK_ROPE), pl.BlockSpec(memory_space=pl.ANY), pl.BlockSpec(memory_space=pl.ANY)],
        out_specs=per_step(MLA_HEADS, KV_RANK),
        scratch_shapes=[pltpu.VMEM((2 * seqs, past, KV_RANK), F32), pltpu.VMEM((2 * seqs, QK_ROPE, past), F32),
                        pltpu.SemaphoreType.DMA((2, 2))],
    )
    return pl.pallas_call(
        functools.partial(_attn_sample_kernel, layer),
        grid_spec=grid_spec,
        out_shape=jax.ShapeDtypeStruct((b, MLA_HEADS, KV_RANK), F32),
        compiler_params=pltpu.CompilerParams(dimension_semantics=("arbitrary",),
                                             vmem_limit_bytes=VMEM_LIMIT_BYTES),
        name="attn_sample",
    )(page_table, qlat3, qrope3, cnew3, rnew3, cache_c, cache_rt)


def _post_sample_kernel(olat_ref, wuv_ref, za_ref, xbc_ref, misc_ref, cprev_ref, cw_ref, cb_ref,
                        dtb_ref, alog_ref, expand_ref,
                        mixa_ref, cnew_ref, act_ref, xdt_t_ref, dec_t_ref):
    for hh in range(MLA_HEADS):
        o = _dot(olat_ref[:, hh * KV_RANK:(hh + 1) * KV_RANK].astype(BF16), wuv_ref[hh])
        vs = slice(hh * V_HEAD, (hh + 1) * V_HEAD)
        mixa_ref[:, vs] = (o * _silu(za_ref[:, vs])).astype(BF16)

    xb = xbc_ref[...]
    acc = cb_ref[...]
    for k in range(CONV_W - 1):
        acc = acc + cprev_ref[k] * cw_ref[k:k + 1, :]
    acc = acc + xb * cw_ref[CONV_W - 1:CONV_W, :]
    act = _silu(acc)
    act_ref[...] = act
    for k in range(CONV_W - 2):
        cnew_ref[k] = cprev_ref[k + 1]
    cnew_ref[CONV_W - 2] = xb

    dt, da = _dt_and_da(misc_ref[...], dtb_ref[...], alog_ref[...])
    xdt = act[:, :D_SSM] * _head_expand(dt, expand_ref)
    decay = _head_expand(jnp.exp(da), expand_ref)
    xdt_t_ref[...] = xdt.T
    dec_t_ref[...] = decay.T


def _post_sample(olat2, za, xbc, misc, cprev3, w):
    n = olat2.shape[0]
    full = lambda a: _full(a.shape)
    args = (olat2, w['w_uvh'], za, xbc, misc, cprev3, w['conv_w'], w['conv_b'], w['dt_bias_t'], w['a_log_t'],
            w['expand'])
    out_shape = (
        jax.ShapeDtypeStruct((n, D_ATTN), BF16),
        jax.ShapeDtypeStruct((CONV_W - 1, n, CONV_DIM), F32),
        jax.ShapeDtypeStruct((n, CONV_DIM), F32),
        jax.ShapeDtypeStruct((D_SSM, n), F32),
        jax.ShapeDtypeStruct((D_SSM, n), F32),
    )
    return pl.pallas_call(
        _post_sample_kernel,
        grid=(1,),
        in_specs=[full(a) for a in args],
        out_specs=tuple(_full(s.shape) for s in out_shape),
        out_shape=out_shape,
        compiler_params=pltpu.CompilerParams(dimension_semantics=("arbitrary",),
                                             vmem_limit_bytes=VMEM_LIMIT_BYTES),
        name="post_sample",
    )(*args)


def _state_sample_kernel(h0_ref, xdt_t_ref, dec_t_ref, bm_ref, cm_ref, h_ref, y_t_ref):
    t = pl.program_id(0)
    tb = h0_ref.shape[0]
    rows, n_seq = xdt_t_ref.shape

    @pl.when(t == 0)
    def _():
        y_t_ref[...] = jnp.zeros(y_t_ref.shape, F32)

    lane = lax.broadcasted_iota(jnp.int32, (rows, n_seq), 1)
    xdt_t = xdt_t_ref[...]
    dec_t = dec_t_ref[...]
    for j in range(tb):
        own = lane == t * tb + j
        x_col = jnp.sum(jnp.where(own, xdt_t, 0.0), axis=1, keepdims=True)
        d_col = jnp.sum(jnp.where(own, dec_t, 0.0), axis=1, keepdims=True)
        b_rows = jnp.concatenate(
            [jnp.broadcast_to(bm_ref[j:j + 1, g * D_STATE:(g + 1) * D_STATE], (GROUP_WIDTH, D_STATE))
             for g in range(SSD_GROUPS)], axis=0)
        c_rows = jnp.concatenate(
            [jnp.broadcast_to(cm_ref[j:j + 1, g * D_STATE:(g + 1) * D_STATE], (GROUP_WIDTH, D_STATE))
             for g in range(SSD_GROUPS)], axis=0)
        h = d_col * h0_ref[j] + x_col * b_rows
        h_ref[j] = h
        y_col = jnp.sum(h * c_rows, axis=1, keepdims=True)
        y_t_ref[...] = jnp.where(own, y_col, y_t_ref[...])


def _state_sample(h0, xdt_t, dec_t, act):
    n = h0.shape[0]
    tb = DEC_STATE_SEQS
    bc_w = SSD_GROUPS * D_STATE
    st = pl.BlockSpec((tb, D_SSM, D_STATE), lambda i: (i, 0, 0))
    return pl.pallas_call(
        _state_sample_kernel,
        grid=(n // tb,),
        in_specs=[st, _full((D_SSM, n)), _full((D_SSM, n)),
                  pl.BlockSpec((tb, bc_w), lambda i: (i, D_SSM // bc_w)),
                  pl.BlockSpec((tb, bc_w), lambda i: (i, D_SSM // bc_w + 1))],
        out_specs=(st, _full((D_SSM, n))),
        out_shape=(jax.ShapeDtypeStruct(h0.shape, F32), jax.ShapeDtypeStruct((D_SSM, n), F32)),
        compiler_params=pltpu.CompilerParams(dimension_semantics=("arbitrary",),
                                             vmem_limit_bytes=VMEM_LIMIT_BYTES),
        name="state_sample",
    )(h0, xdt_t, dec_t, act, act)


def _finish_sample_kernel(y_t_ref, act_ref, z_ref, dskip_ref, normw_ref, ma_ref, x_ref, wout_ref, npost_ref,
                          y_ref):
    mix_s = _ssd_finish(y_t_ref[...].T, act_ref[...], z_ref[...], dskip_ref[...], normw_ref[...])
    y_ref[...] = _out_proj(_out_proj_attn(ma_ref[...], wout_ref), mix_s, wout_ref, npost_ref[...], x_ref[...])


def _finish_sample(y_t, act, zs, mix_a, x2, w):
    n = x2.shape[0]
    return pl.pallas_call(
        _finish_sample_kernel,
        grid=(1,),
        in_specs=[_full(y_t.shape), pl.BlockSpec((n, D_SSM), lambda i: (0, 0)), _full(zs.shape),
                  _full((1, D_SSM)), _full((1, D_SSM)), _full(mix_a.shape), _full(x2.shape),
                  _full((D_MIX, D_MODEL)), _full((1, D_MODEL))],
        out_specs=_full((n, D_MODEL)),
        out_shape=jax.ShapeDtypeStruct((n, D_MODEL), F32),
        compiler_params=pltpu.CompilerParams(dimension_semantics=("arbitrary",),
                                             vmem_limit_bytes=VMEM_LIMIT_BYTES),
        name="finish_sample",
    )(y_t, act, zs, w['d_skip_w'], w['ssm_norm'], mix_a, x2, w['w_out'], w['norm_post'])


ORIG_KR = Q_RANK + KV_RANK
ORIG_ZA = ORIG_KR + QK_ROPE
ORIG_DT = ORIG_ZA + D_ATTN + D_SSM + CONV_DIM
PREP_COLS = 128


def _prep_w_in_kernel(wt_ref, o_ref):
    def put(tile, src_rows):
        o_ref[:, tile * LANES:(tile + 1) * LANES] = src_rows.T.astype(BF16)

    for t in range(COL_ZA // LANES):
        put(t, wt_ref[t * LANES:(t + 1) * LANES, :])
    for t in range((COL_MISC - COL_ZA) // LANES):
        put(COL_ZA // LANES + t, wt_ref[ORIG_ZA + t * LANES:ORIG_ZA + (t + 1) * LANES, :])
    pad = jnp.zeros((LANES - QK_ROPE - SSD_HEADS, wt_ref.shape[1]), F32)
    put(COL_MISC // LANES, jnp.concatenate([wt_ref[ORIG_KR:ORIG_ZA, :], wt_ref[ORIG_DT:ORIG_DT + SSD_HEADS, :], pad],
                                           axis=0))


def _prep_w_in(w_in):
    k, n = w_in.shape
    assert n == ORIG_DT + SSD_HEADS and k % PREP_COLS == 0
    return pl.pallas_call(
        _prep_w_in_kernel,
        grid=(k // PREP_COLS,),
        in_specs=[pl.BlockSpec((n, PREP_COLS), lambda i: (0, i))],
        out_specs=pl.BlockSpec((PREP_COLS, D_IN_PAD), lambda i: (i, 0)),
        out_shape=jax.ShapeDtypeStruct((k, D_IN_PAD), BF16),
        compiler_params=pltpu.CompilerParams(dimension_semantics=("arbitrary",),
                                             vmem_limit_bytes=VMEM_LIMIT_BYTES),
        name="prep_w_in",
    )(w_in.T)


def _rope_tables(pos):
    inv_freq = ROPE_THETA ** (-jnp.arange(ROPE_HALF, dtype=F32) / ROPE_HALF)
    ang = pos.astype(F32)[:, None] * inv_freq[None, :]
    cos, sin = jnp.cos(ang), jnp.sin(ang)
    reps = LANES // QK_ROPE
    return jnp.tile(cos, (1, 2 * reps)), jnp.tile(jnp.concatenate([-sin, sin], axis=1), (1, reps))


def _prep_weights(lw):
    row = lambda v: v.reshape(1, -1).astype(F32)
    w_qb = lw['w_q_b'].reshape(Q_RANK, MLA_HEADS, QK_NOPE + QK_ROPE)
    lane_pad = lambda v: jnp.pad(v.reshape(1, -1).astype(F32), ((0, 0), (DT_LANE0, LANES - DT_LANE0 - SSD_HEADS)))
    head_of_col = jnp.arange(D_SSM) // SSD_HEADDIM
    return {
        'norm_pre': row(lw['norm_pre']),
        'w_in': _prep_w_in(lw['w_in']),
        'q_a_norm': row(lw['q_a_norm']),
        'w_qb': jnp.concatenate([w_qb[:, :, :QK_NOPE].reshape(Q_RANK, -1),
                                 w_qb[:, :, QK_NOPE:].reshape(Q_RANK, -1)], axis=1).astype(BF16),
        'kv_a_norm': row(lw['kv_a_norm']),
        'w_uk2t': lw['w_uk'].reshape(KV_RANK, MLA_HEADS * QK_NOPE).T.astype(BF16),
        'w_uv2': lw['w_uv'].reshape(KV_RANK, MLA_HEADS * V_HEAD).astype(BF16),
        'w_ukt': jnp.transpose(lw['w_uk'], (1, 2, 0)).astype(BF16),
        'w_uvh': jnp.transpose(lw['w_uv'], (1, 0, 2)).astype(BF16),
        'conv_w': lw['conv_w'].astype(F32),
        'conv_b': row(lw['conv_b']),
        'dt_bias_t': lane_pad(lw['dt_bias']),
        'a_log_t': lane_pad(lw['a_log']),
        'expand': sum((jnp.arange(LANES)[:, None] == DT_LANE0 + k * SSD_HEADS + head_of_col[None, :])
                      for k in range(SPLIT_PIECES)).astype(BF16),
        'd_skip_w': jnp.repeat(lw['d_skip'].astype(F32), SSD_HEADDIM).reshape(1, D_SSM),
        'ssm_norm': row(lw['ssm_norm']),
        'w_out': lw['w_out'].astype(BF16),
        'norm_post': row(lw['norm_post']),
    }


def _prompt_layer(x, w):
    b, s, _ = x.shape
    x2 = x.reshape(b * s, D_MODEL)
    cos, sin = _rope_tables(jnp.arange(s, dtype=jnp.int32))
    q, kt, v, ckv, kr, za, mix_s, state, tail = _proj_prompt(x2, cos, sin, w)
    r3 = lambda a: a.reshape(b, s, a.shape[-1])
    mix_a = _attn_prompt(r3(q), kt, r3(v), r3(za))
    y = _merge(mix_a.reshape(b * s, D_ATTN), mix_s, x2, w)
    h = state.reshape(b, SSD_GROUPS, D_STATE, HEADS_PER_GROUP, SSD_HEADDIM)
    h = jnp.transpose(h, (0, 1, 3, 4, 2)).reshape(b, SSD_HEADS, SSD_HEADDIM, D_STATE)
    return (y.reshape(b, s, D_MODEL), r3(ckv), jnp.swapaxes(kr, 1, 2), tail[:, CONV_PAD - (CONV_W - 1):, :], h)


def _sample_layer(layer, x, cache_c, cache_r, conv_prev, h0, page_table, w):
    b, s, _ = x.shape
    n = b * s
    past = page_table.shape[1] * PAGE_SIZE
    x2 = x.reshape(n, D_MODEL)
    pos = past + jnp.arange(s, dtype=jnp.int32)
    cos, sin = _rope_tables(jnp.tile(pos, b))
    qlat, qrope, ckv, kr, za, zs, xbc, misc = _proj_sample(x2, cos, sin, w)
    olat = _attn_sample(layer, page_table, qlat.reshape(n, MLA_HEADS, KV_RANK), qrope.reshape(n, MLA_HEADS, QK_ROPE),
                        ckv.reshape(n, 1, KV_RANK), kr.reshape(n, 1, QK_ROPE), cache_c,
                        jnp.swapaxes(cache_r, 2, 3))
    mix_a, conv_new, act, xdt_t, dec_t = _post_sample(
        olat.reshape(n, MLA_HEADS * KV_RANK), za, xbc, misc, jnp.swapaxes(conv_prev, 0, 1), w)
    h, y_t = _state_sample(h0.astype(F32).reshape(n, D_SSM, D_STATE), xdt_t, dec_t, act)
    y = _finish_sample(y_t, act, zs, mix_a, x2, w)
    return (y.reshape(b, s, D_MODEL), ckv.reshape(b, s, KV_RANK), kr.reshape(b, s, QK_ROPE),
            jnp.swapaxes(conv_new, 0, 1), h.reshape(b, SSD_HEADS, SSD_HEADDIM, D_STATE))


def kernel(x_prompt, x_sample, cache_ckv, cache_krope, state_conv, state_ssm, page_table, norm_pre, w_in,
           q_a_norm, w_q_b, kv_a_norm, w_uk, w_uv, conv_w, conv_b, dt_bias, a_log, d_skip, ssm_norm, w_out,
           norm_post):
    assert x_sample.shape[1] == 1, "the sample path handles one new token per sequence"
    depth = w_in.shape[0]
    y_prompt, y_sample = x_prompt, x_sample
    outs = [[] for _ in range(8)]
    for l in range(depth):
        w = _prep_weights({'norm_pre': norm_pre[l], 'w_in': w_in[l], 'q_a_norm': q_a_norm[l],
                           'w_q_b': w_q_b[l], 'kv_a_norm': kv_a_norm[l], 'w_uk': w_uk[l], 'w_uv': w_uv[l],
                           'conv_w': conv_w[l], 'conv_b': conv_b[l], 'dt_bias': dt_bias[l], 'a_log': a_log[l],
                           'd_skip': d_skip[l], 'ssm_norm': ssm_norm[l], 'w_out': w_out[l],
                           'norm_post': norm_post[l]})
        y_prompt, c1, k1, v1, h1 = _prompt_layer(y_prompt, w)
        y_sample, c2, k2, v2, h2 = _sample_layer(l, y_sample, cache_ckv, cache_krope, state_conv[l],
                                                 state_ssm[l], page_table, w)
        for lst, val in zip(outs, (c1, k1, v1, h1, c2, k2, v2, h2)):
            lst.append(val)
    return (y_prompt, y_sample) + tuple(jnp.stack(o) for o in outs)
```

```python
import functools
import math

import jax
import jax.numpy as jnp
from jax import lax
from jax.experimental import pallas as pl
from jax.experimental.pallas import tpu as pltpu

F32 = jnp.float32
BF16 = jnp.bfloat16

D_MODEL = 1024
PAGE_SIZE = 128
D_MIX = 2 * D_MODEL
D_ATTN = D_MIX // 2
D_SSM = D_MIX - D_ATTN
MLA_HEADS = 8
QK_NOPE = 128
QK_ROPE = 64
ROPE_HALF = QK_ROPE // 2
V_HEAD = D_ATTN // MLA_HEADS
Q_RANK = 384
KV_RANK = 256
ROPE_THETA = 10000.0
SOFTMAX_SCALE = (QK_NOPE + QK_ROPE) ** -0.5
SSD_HEADDIM = 64
SSD_HEADS = D_SSM // SSD_HEADDIM
SSD_GROUPS = 2
HEADS_PER_GROUP = SSD_HEADS // SSD_GROUPS
GROUP_WIDTH = D_SSM // SSD_GROUPS
D_STATE = 128
CONV_W = 4
CONV_DIM = D_SSM + 2 * SSD_GROUPS * D_STATE
CHUNK = 128
EPS = 1e-6
SPLIT_PIECES = 3
NEG_BIG = -1e30

LANES = 128
SUBLANES = 8
VMEM_LIMIT_BYTES = 56 * 1024 * 1024

COL_Q = 0
COL_C = COL_Q + Q_RANK
COL_ZA = COL_C + KV_RANK
COL_ZS = COL_ZA + D_ATTN
COL_XBC = COL_ZS + D_SSM
COL_MISC = COL_XBC + CONV_DIM
D_IN_PAD = COL_MISC + LANES
DT_LANE0 = QK_ROPE
QK_PAD = 2 * LANES

PROJ_ROWS = 512
ATTN_TQ = 256
ATTN_HEADS = 4
SSD_STEP_CHUNKS = 4
MERGE_ROWS = 1024
CONV_PAD = SUBLANES
DEC_STATE_SEQS = 8
DEC_KV_CHUNK = 2048
DEC_ATTN_SEQS = 2

_NT = (((1,), (1,)), ((), ()))


def _rms(x, w):
    return x * lax.rsqrt(jnp.mean(x * x, axis=-1, keepdims=True) + EPS) * w


def _silu(x):
    return x / (1.0 + jnp.exp(-x))


def _softplus(x):
    return jnp.maximum(x, 0.0) + jnp.log(1.0 + jnp.exp(-jnp.abs(x)))


def _dot(a, b):
    return jnp.dot(a, b, preferred_element_type=F32)


def _rope_tile(x, cos, sin_signed):
    lane = lax.broadcasted_iota(jnp.int32, x.shape, 1)
    first_half = (lane % QK_ROPE) < ROPE_HALF
    partner = jnp.where(first_half,
                        pltpu.roll(x, LANES - ROPE_HALF, 1),
                        pltpu.roll(x, ROPE_HALF, 1))
    return x * cos + partner * sin_signed


def _front(x_ref, npre_ref, win_ref):
    h = _rms(x_ref[...], npre_ref[...]).astype(BF16)

    def seg(lo, hi):
        return _dot(h, win_ref[:, lo:hi])

    return seg


def _proj_prompt_kernel(seq_tiles, x_ref, cos_ref, sin_ref, npre_ref, win_ref, qan_ref, wqb_ref, kvn_ref,
                        wukt_ref, wuv_ref, cw_ref, cb_ref, dtb_ref, alog_ref, expand_ref, dskip_ref, normw_ref,
                        q_ref, kt_ref, v_ref, ckv_ref, kr_ref, za_ref, mixs_ref, state_ref, tail_ref, xp_ref):
    rows = x_ref.shape[0]
    first = pl.program_id(0) % seq_tiles == 0

    @pl.when(first)
    def _():
        xp_ref[0:CONV_PAD, :] = jnp.zeros((CONV_PAD, CONV_DIM), F32)
        state_ref[...] = jnp.zeros(state_ref.shape, F32)

    seg = _front(x_ref, npre_ref, win_ref)
    xp_ref[CONV_PAD:CONV_PAD + rows, :] = seg(COL_XBC, COL_MISC)
    xp = xp_ref[...]
    acc = cb_ref[...]
    for tap in range(CONV_W):
        back = CONV_W - 1 - tap
        shifted = xp if back == 0 else pltpu.roll(xp, back, 0)
        acc = acc + shifted[CONV_PAD:CONV_PAD + rows, :] * cw_ref[tap:tap + 1, :]
    act_all = _silu(acc)
    tail_ref[...] = xp_ref[rows:rows + CONV_PAD, :]
    xp_ref[CONV_PAD - 3:CONV_PAD, :] = xp_ref[CONV_PAD + rows - 3:CONV_PAD + rows, :]

    misc = seg(COL_MISC, D_IN_PAD)
    cos = cos_ref[...]
    sin = sin_ref[...]
    lane = lax.broadcasted_iota(jnp.int32, misc.shape, 1)
    low = lane < QK_ROPE

    kr_full = _rope_tile(misc, cos, sin)
    kr_t = jnp.where(low, kr_full, 0.0).T
    kr_ref[...] = kr_t[:QK_ROPE, :]
    kr_lo_t = kr_t.astype(BF16)
    kr_hi_t = jnp.where(low, 0.0, pltpu.roll(kr_full, QK_ROPE, 1)).T.astype(BF16)

    c_raw = seg(COL_C, COL_ZA)
    q_a = seg(COL_Q, COL_C)
    za_ref[...] = seg(COL_ZA, COL_ZS)
    ckv = _rms(c_raw, kvn_ref[...])
    ckv_ref[...] = ckv
    cb = ckv.astype(BF16)
    knope_t = lax.dot_general(wukt_ref[...], cb, _NT, preferred_element_type=F32)
    v_ref[...] = _dot(cb, wuv_ref[...]).astype(BF16)

    qn = _rms(q_a, qan_ref[...]).astype(BF16)
    q = _dot(qn, wqb_ref[...])
    nope_w = MLA_HEADS * QK_NOPE
    for hh in range(MLA_HEADS):
        pair = hh // 2
        r = _rope_tile(q[:, nope_w + pair * LANES: nope_w + (pair + 1) * LANES], cos, sin)
        own = low if hh % 2 == 0 else jnp.logical_not(low)
        base = hh * QK_PAD
        q_ref[:, base:base + LANES] = (q[:, hh * QK_NOPE:(hh + 1) * QK_NOPE] * SOFTMAX_SCALE).astype(BF16)
        q_ref[:, base + LANES:base + QK_PAD] = (jnp.where(own, r, 0.0) * SOFTMAX_SCALE).astype(BF16)
        kt_ref[base:base + LANES, :] = knope_t[hh * QK_NOPE:(hh + 1) * QK_NOPE, :].astype(BF16)
        kt_ref[base + LANES:base + QK_PAD, :] = kr_lo_t if hh % 2 == 0 else kr_hi_t

    zs = seg(COL_ZS, COL_XBC)
    for k in range(rows // CHUNK):
        r0 = k * CHUNK
        act = act_all[r0:r0 + CHUNK, :]
        y = _ssd_chunk(act, misc[r0:r0 + CHUNK, :], dtb_ref, alog_ref, expand_ref, state_ref)
        mixs_ref[r0:r0 + CHUNK, :] = _ssd_finish(y, act[:, :D_SSM], zs[r0:r0 + CHUNK, :], dskip_ref[...],
                                                 normw_ref[...])


def _proj_sample_kernel(x_ref, cos_ref, sin_ref, npre_ref, win_ref, qan_ref, wqb_ref, kvn_ref, wukt_ref,
                        qlat_ref, qrope_ref, ckv_ref, kr_ref, za_ref, zs_ref, xbc_ref, misc_ref):
    seg = _front(x_ref, npre_ref, win_ref)
    za_ref[...] = seg(COL_ZA, COL_ZS)
    zs_ref[...] = seg(COL_ZS, COL_XBC)
    xbc_ref[...] = seg(COL_XBC, COL_MISC)
    misc = seg(COL_MISC, D_IN_PAD)
    misc_ref[...] = misc
    cos = cos_ref[...]
    sin = sin_ref[...]
    kr_ref[...] = _rope_tile(misc, cos, sin)[:, :QK_ROPE]
    ckv_ref[...] = _rms(seg(COL_C, COL_ZA), kvn_ref[...])

    qn = _rms(seg(COL_Q, COL_C), qan_ref[...]).astype(BF16)
    q = _dot(qn, wqb_ref[...])
    nope_w = MLA_HEADS * QK_NOPE
    for pair in range(MLA_HEADS // 2):
        lo = nope_w + pair * LANES
        qrope_ref[:, pair * LANES:(pair + 1) * LANES] = _rope_tile(q[:, lo:lo + LANES], cos, sin) * SOFTMAX_SCALE
    for hh in range(MLA_HEADS):
        qh = q[:, hh * QK_NOPE:(hh + 1) * QK_NOPE].astype(BF16)
        qlat_ref[:, hh * KV_RANK:(hh + 1) * KV_RANK] = _dot(qh, wukt_ref[hh]) * SOFTMAX_SCALE


def _full(shape):
    return pl.BlockSpec(shape, lambda *_: (0,) * len(shape))


def _proj_prompt(x2, cos, sin, w):
    n = x2.shape[0]
    tm = PROJ_ROWS
    seq_tiles = cos.shape[0] // tm
    assert tm == SSD_STEP_CHUNKS * CHUNK
    batch = n // cos.shape[0]
    rows = lambda width: pl.BlockSpec((tm, width), lambda i: (i, 0))
    tab = pl.BlockSpec((tm, LANES), lambda i: (i % seq_tiles, 0))
    out_shape = (
        jax.ShapeDtypeStruct((n, MLA_HEADS * QK_PAD), BF16),
        jax.ShapeDtypeStruct((MLA_HEADS * QK_PAD, n), BF16),
        jax.ShapeDtypeStruct((n, D_ATTN), BF16),
        jax.ShapeDtypeStruct((n, KV_RANK), F32),
        jax.ShapeDtypeStruct((batch, QK_ROPE, cos.shape[0]), F32),
        jax.ShapeDtypeStruct((n, D_ATTN), F32),
        jax.ShapeDtypeStruct((n, D_SSM), BF16),
        jax.ShapeDtypeStruct((batch, SSD_GROUPS, D_STATE, GROUP_WIDTH), F32),
        jax.ShapeDtypeStruct((batch, CONV_PAD, CONV_DIM), F32),
    )
    per_seq = lambda shape: pl.BlockSpec((None,) + shape[1:], lambda i: (i // seq_tiles,) + (0,) * (len(shape) - 1))
    out_specs = [rows(s.shape[1]) for s in out_shape[:7]]
    out_specs[1] = pl.BlockSpec((MLA_HEADS * QK_PAD, tm), lambda i: (0, i))
    out_specs[4] = pl.BlockSpec((None, QK_ROPE, tm), lambda i: (i // seq_tiles, 0, i % seq_tiles))
    out_specs += [per_seq(out_shape[7].shape), per_seq(out_shape[8].shape)]
    return pl.pallas_call(
        functools.partial(_proj_prompt_kernel, seq_tiles),
        grid=(n // tm,),
        in_specs=[rows(D_MODEL), tab, tab, _full((1, D_MODEL)), _full((D_MODEL, D_IN_PAD)),
                  _full((1, Q_RANK)), _full(w['w_qb'].shape), _full((1, KV_RANK)),
                  _full(w['w_uk2t'].shape), _full(w['w_uv2'].shape),
                  _full((CONV_W, CONV_DIM)), _full((1, CONV_DIM)), _full((1, LANES)), _full((1, LANES)),
                  _full((LANES, D_SSM)), _full((1, D_SSM)), _full((1, D_SSM))],
        out_specs=tuple(out_specs),
        out_shape=out_shape,
        scratch_shapes=[pltpu.VMEM((CONV_PAD + tm, CONV_DIM), F32)],
        compiler_params=pltpu.CompilerParams(dimension_semantics=("arbitrary",),
                                             vmem_limit_bytes=VMEM_LIMIT_BYTES),
        name="proj_prompt",
    )(x2, cos, sin, w['norm_pre'], w['w_in'], w['q_a_norm'], w['w_qb'], w['kv_a_norm'],
      w['w_uk2t'], w['w_uv2'], w['conv_w'], w['conv_b'], w['dt_bias_t'], w['a_log_t'], w['expand'],
      w['d_skip_w'], w['ssm_norm'])


def _proj_sample(x2, cos, sin, w):
    n = x2.shape[0]
    rows = lambda width: pl.BlockSpec((n, width), lambda i: (0, 0))
    out_shape = (
        jax.ShapeDtypeStruct((n, MLA_HEADS * KV_RANK), F32),
        jax.ShapeDtypeStruct((n, MLA_HEADS * QK_ROPE), F32),
        jax.ShapeDtypeStruct((n, KV_RANK), F32),
        jax.ShapeDtypeStruct((n, QK_ROPE), F32),
        jax.ShapeDtypeStruct((n, D_ATTN), F32),
        jax.ShapeDtypeStruct((n, D_SSM), F32),
        jax.ShapeDtypeStruct((n, CONV_DIM), F32),
        jax.ShapeDtypeStruct((n, LANES), F32),
    )
    return pl.pallas_call(
        _proj_sample_kernel,
        grid=(1,),
        in_specs=[rows(D_MODEL), rows(LANES), rows(LANES), _full((1, D_MODEL)), _full((D_MODEL, D_IN_PAD)),
                  _full((1, Q_RANK)), _full(w['w_qb'].shape), _full((1, KV_RANK)),
                  _full(w['w_ukt'].shape)],
        out_specs=tuple(rows(s.shape[1]) for s in out_shape),
        out_shape=out_shape,
        compiler_params=pltpu.CompilerParams(dimension_semantics=("arbitrary",),
                                             vmem_limit_bytes=VMEM_LIMIT_BYTES),
        name="proj_sample",
    )(x2, cos, sin, w['norm_pre'], w['w_in'], w['q_a_norm'], w['w_qb'], w['kv_a_norm'], w['w_ukt'])


def _attn_prompt_kernel(q_ref, kt_ref, v_ref, z_ref, o_ref):
    seq = q_ref.shape[0]
    tq = ATTN_TQ
    row = lax.broadcasted_iota(jnp.int32, (tq, tq), 0)
    col = lax.broadcasted_iota(jnp.int32, (tq, tq), 1)
    causal = col <= row
    for qi in reversed(range(seq // tq)):
        lim = (qi + 1) * tq
        ones_col = (lax.broadcasted_iota(jnp.int32, (lim, LANES), 1) == 0).astype(BF16)
        for hd in range(ATTN_HEADS):
            qk = slice(hd * QK_PAD, (hd + 1) * QK_PAD)
            hv = slice(hd * V_HEAD, (hd + 1) * V_HEAD)
            s = _dot(q_ref[qi * tq:lim, qk], kt_ref[qk, 0:lim])
            diag = jnp.where(causal, s[:, lim - tq:], NEG_BIG)
            s = diag if qi == 0 else jnp.concatenate([s[:, :lim - tq], diag], axis=1)
            m = jnp.max(s, axis=-1, keepdims=True)
            p = jnp.exp(s - m).astype(BF16)
            ol = _dot(p, jnp.concatenate([v_ref[0:lim, hv], ones_col], axis=1))
            o = ol[:, :V_HEAD] / jnp.sum(ol[:, V_HEAD:], axis=-1, keepdims=True)
            o_ref[qi * tq:lim, hv] = (o * _silu(z_ref[qi * tq:lim, hv])).astype(BF16)


def _attn_prompt(q3, kt, v3, z3):
    b, s, _ = q3.shape
    nh = ATTN_HEADS
    hv = pl.BlockSpec((None, s, nh * V_HEAD), lambda i, j: (i, 0, j))
    return pl.pallas_call(
        _attn_prompt_kernel,
        grid=(b, MLA_HEADS // nh),
        in_specs=[pl.BlockSpec((None, s, nh * QK_PAD), lambda i, j: (i, 0, j)),
                  pl.BlockSpec((nh * QK_PAD, s), lambda i, j: (j, i)), hv, hv],
        out_specs=hv,
        out_shape=jax.ShapeDtypeStruct((b, s, D_ATTN), BF16),
        compiler_params=pltpu.CompilerParams(dimension_semantics=("arbitrary", "arbitrary"),
                                             vmem_limit_bytes=VMEM_LIMIT_BYTES),
        name="attn_prompt",
    )(q3, kt, v3, z3)


def _dt_lanes(shape):
    lane = lax.broadcasted_iota(jnp.int32, shape, 1)
    return jnp.logical_and(lane >= DT_LANE0, lane < DT_LANE0 + SSD_HEADS)


def _dt_and_da(misc, dtb, alog):
    dt = jnp.where(_dt_lanes(misc.shape), _softplus(misc + dtb), 0.0)
    return dt, dt * (-jnp.exp(alog))


def _split_bf16(x):
    pieces = []
    for _ in range(SPLIT_PIECES):
        piece = x.astype(BF16).astype(F32)
        pieces.append(piece)
        x = x - piece
    return pieces


def _cumsum_rows(lower_b, x):
    return sum(_dot(lower_b, piece.astype(BF16)) for piece in _split_bf16(x))


def _head_expand(x, expand_ref):
    pieces = _split_bf16(jnp.where(_dt_lanes(x.shape), x, 0.0))
    packed = pieces[0]
    for k in range(1, SPLIT_PIECES):
        packed = packed + pltpu.roll(pieces[k], k * SSD_HEADS, 1)
    return _dot(packed.astype(BF16), expand_ref[...])


def _ssd_finish(y, xs, z, dskip, normw):
    y = y + dskip * xs
    gated = y * _silu(z)
    outs = []
    for g in range(SSD_GROUPS):
        sl = slice(g * GROUP_WIDTH, (g + 1) * GROUP_WIDTH)
        outs.append(_rms(gated[:, sl], normw[:, sl]))
    return jnp.concatenate(outs, axis=1).astype(BF16)


def _out_proj_attn(mix_a, wout_ref):
    return _dot(mix_a, wout_ref[0:D_ATTN, :])


def _out_proj(o_attn, mix_s, wout_ref, npost, x):
    o = o_attn + _dot(mix_s, wout_ref[D_ATTN:D_MIX, :])
    return x + _rms(o, npost)


def _ssd_chunk(act, misc, dtb_ref, alog_ref, expand_ref, state_ref):
    xs = act[:, :D_SSM]
    dt, da = _dt_and_da(misc, dtb_ref[...], alog_ref[...])
    row = lax.broadcasted_iota(jnp.int32, (CHUNK, CHUNK), 0)
    col = lax.broadcasted_iota(jnp.int32, (CHUNK, CHUNK), 1)
    lower = row >= col
    a_cum = _cumsum_rows(lower.astype(BF16), da)
    a_cum_t = a_cum.T
    a_last = a_cum[CHUNK - 1:CHUNK, :]
    xdt = xs * _head_expand(dt, expand_ref)
    decay_out = _head_expand(jnp.exp(a_cum), expand_ref)
    state_decay = decay_out[CHUNK - 1:CHUNK, :]
    xdt_end = (xdt * _head_expand(jnp.exp(a_last - a_cum), expand_ref)).astype(BF16)
    xdt_b = xdt.astype(BF16)
    lane_w = lax.broadcasted_iota(jnp.int32, (CHUNK, LANES), 1)
    first_head = lane_w < SSD_HEADDIM

    ys = []
    for g in range(SSD_GROUPS):
        bm = act[:, D_SSM + g * D_STATE:D_SSM + (g + 1) * D_STATE]
        cm = act[:, D_SSM + (SSD_GROUPS + g) * D_STATE:D_SSM + (SSD_GROUPS + g + 1) * D_STATE]
        bm_b = bm.astype(BF16)
        cm_b = cm.astype(BF16)
        cb = lax.dot_general(cm_b, bm_b, _NT, preferred_element_type=F32)
        gsl = slice(g * GROUP_WIDTH, (g + 1) * GROUP_WIDTH)
        state = state_ref[g]
        y_off = _dot(cm_b, state.astype(BF16)) * decay_out[:, gsl]
        y_diag = []
        for pair in range(HEADS_PER_GROUP // 2):
            halves = []
            x_pair = xdt_b[:, g * GROUP_WIDTH + pair * LANES: g * GROUP_WIDTH + (pair + 1) * LANES]
            for k in range(2):
                lane_h = DT_LANE0 + g * HEADS_PER_GROUP + 2 * pair + k
                seg = a_cum[:, lane_h:lane_h + 1] - a_cum_t[lane_h:lane_h + 1, :]
                decay = jnp.exp(jnp.where(lower, seg, NEG_BIG))
                halves.append(_dot((cb * decay).astype(BF16), x_pair))
            y_diag.append(jnp.where(first_head, halves[0], halves[1]))
        ys.append(jnp.concatenate(y_diag, axis=1) + y_off)
        state_ref[g] = state * state_decay[:, gsl] + _dot(bm.T.astype(BF16), xdt_end[:, gsl])
    return jnp.concatenate(ys, axis=1)


def _merge_kernel(ma_ref, ms_ref, x_ref, wout_ref, npost_ref, y_ref):
    y_ref[...] = _out_proj(_out_proj_attn(ma_ref[...], wout_ref), ms_ref[...], wout_ref, npost_ref[...], x_ref[...])


def _merge(mix_a, mix_s, x2, w):
    n = x2.shape[0]
    tm = MERGE_ROWS
    rows = lambda width: pl.BlockSpec((tm, width), lambda i: (i, 0))
    return pl.pallas_call(
        _merge_kernel,
        grid=(n // tm,),
        in_specs=[rows(D_ATTN), rows(D_SSM), rows(D_MODEL), _full((D_MIX, D_MODEL)), _full((1, D_MODEL))],
        out_specs=rows(D_MODEL),
        out_shape=jax.ShapeDtypeStruct((n, D_MODEL), F32),
        compiler_params=pltpu.CompilerParams(dimension_semantics=("arbitrary",),
                                             vmem_limit_bytes=VMEM_LIMIT_BYTES),
        name="merge",
    )(mix_a, mix_s, x2, w['w_out'], w['norm_post'])


def _attn_sample_kernel(layer, pt_ref, qlat_ref, qrope_ref, cnew_ref, rnew_ref, cache_c_ref, cache_rt_ref,
                        o_ref, cbuf, rbuf, sem):
    i = pl.program_id(0)
    n = pl.num_programs(0)
    n_pages = pt_ref.shape[1]
    seqs = qlat_ref.shape[0]
    past = n_pages * PAGE_SIZE
    slot = i % 2

    def page_copies(step, slot_):
        copies = []
        for j in range(seqs):
            buf = slot_ * seqs + j
            for p in range(n_pages):
                page = pt_ref[step * seqs + j, p]
                dst = pl.ds(p * PAGE_SIZE, PAGE_SIZE)
                copies.append(pltpu.make_async_copy(cache_c_ref.at[layer, page], cbuf.at[buf, dst],
                                                    sem.at[0, slot_]))
                copies.append(pltpu.make_async_copy(cache_rt_ref.at[layer, page], rbuf.at[buf, :, dst],
                                                    sem.at[1, slot_]))
        return copies

    @pl.when(i == 0)
    def _():
        for cp in page_copies(0, 0):
            cp.start()

    @pl.when(i + 1 < n)
    def _():
        for cp in page_copies(i + 1, 1 - slot):
            cp.start()

    for cp in page_copies(i, slot):
        cp.wait()

    q_pos = past
    chunks = [pl.ds(c * DEC_KV_CHUNK, DEC_KV_CHUNK) for c in range(past // DEC_KV_CHUNK)]
    for j in range(seqs):
        buf = slot * seqs + j
        qlat = qlat_ref[j]
        qrope = qrope_ref[j]
        cnew = cnew_ref[j]
        rnew = rnew_ref[j]
        s_new = (jnp.sum(qlat * cnew, axis=-1, keepdims=True)
                 + jnp.sum(qrope * rnew, axis=-1, keepdims=True))
        s = jnp.concatenate(
            [lax.dot_general(qlat, cbuf[buf, keys, :], _NT, preferred_element_type=F32)
             + _dot(qrope, rbuf[buf, :, keys]) for keys in chunks], axis=1)
        k_pos = lax.broadcasted_iota(jnp.int32, s.shape, 1)
        s = jnp.where(k_pos <= q_pos, s, NEG_BIG)
        m = jnp.maximum(jnp.max(s, axis=-1, keepdims=True), s_new)
        p = jnp.exp(s - m)
        p_new = jnp.exp(s_new - m)
        l = jnp.sum(p, axis=-1, keepdims=True) + p_new
        acc = p_new * cnew
        for c, keys in enumerate(chunks):
            acc = acc + _dot(p[:, c * DEC_KV_CHUNK:(c + 1) * DEC_KV_CHUNK], cbuf[buf, keys, :])
        o_ref[j] = acc / l


def _attn_sample(layer, page_table, qlat3, qrope3, cnew3, rnew3, cache_c, cache_rt):
    b, n_pages = page_table.shape
    past = n_pages * PAGE_SIZE
    seqs = DEC_ATTN_SEQS
    assert past % DEC_KV_CHUNK == 0 and b % seqs == 0
    per_step = lambda d1, d2: pl.BlockSpec((seqs, d1, d2), lambda i, pt: (i, 0, 0))
    grid_spec = pltpu.PrefetchScalarGridSpec(
        num_scalar_prefetch=1,
        grid=(b // seqs,),
        in_specs=[per_step(MLA_HEADS, KV_RANK), per_step(MLA_HEADS, QK_ROPE), per_step(1, KV_RANK),
                  per_step(1, QK_ROPE), pl.BlockSpec(memory_space=pl.ANY), pl.BlockSpec(memory_space=pl.ANY)],
        out_specs=per_step(MLA_HEADS, KV_RANK),
        scratch_shapes=[pltpu.VMEM((2 * seqs, past, KV_RANK), F32), pltpu.VMEM((2 * seqs, QK_ROPE, past), F32),
                        pltpu.SemaphoreType.DMA((2, 2))],
    )
    return pl.pallas_call(
        functools.partial(_attn_sample_kernel, layer),
        grid_spec=grid_spec,
        out_shape=jax.ShapeDtypeStruct((b, MLA_HEADS, KV_RANK), F32),
        compiler_params=pltpu.CompilerParams(dimension_semantics=("arbitrary",),
                                             vmem_limit_bytes=VMEM_LIMIT_BYTES),
        name="attn_sample",
    )(page_table, qlat3, qrope3, cnew3, rnew3, cache_c, cache_rt)


def _post_sample_kernel(olat_ref, wuv_ref, za_ref, xbc_ref, misc_ref, cprev_ref, cw_ref, cb_ref,
                        dtb_ref, alog_ref, expand_ref,
                        mixa_ref, cnew_ref, act_ref, xdt_t_ref, dec_t_ref):
    for hh in range(MLA_HEADS):
        o = _dot(olat_ref[:, hh * KV_RANK:(hh + 1) * KV_RANK].astype(BF16), wuv_ref[hh])
        vs = slice(hh * V_HEAD, (hh + 1) * V_HEAD)
        mixa_ref[:, vs] = (o * _silu(za_ref[:, vs])).astype(BF16)

    xb = xbc_ref[...]
    acc = cb_ref[...]
    for k in range(CONV_W - 1):
        acc = acc + cprev_ref[k] * cw_ref[k:k + 1, :]
    acc = acc + xb * cw_ref[CONV_W - 1:CONV_W, :]
    act = _silu(acc)
    act_ref[...] = act
    for k in range(CONV_W - 2):
        cnew_ref[k] = cprev_ref[k + 1]
    cnew_ref[CONV_W - 2] = xb

    dt, da = _dt_and_da(misc_ref[...], dtb_ref[...], alog_ref[...])
    xdt = act[:, :D_SSM] * _head_expand(dt, expand_ref)
    decay = _head_expand(jnp.exp(da), expand_ref)
    xdt_t_ref[...] = xdt.T
    dec_t_ref[...] = decay.T


def _post_sample(olat2, za, xbc, misc, cprev3, w):
    n = olat2.shape[0]
    full = lambda a: _full(a.shape)
    args = (olat2, w['w_uvh'], za, xbc, misc, cprev3, w['conv_w'], w['conv_b'], w['dt_bias_t'], w['a_log_t'],
            w['expand'])
    out_shape = (
        jax.ShapeDtypeStruct((n, D_ATTN), BF16),
        jax.ShapeDtypeStruct((CONV_W - 1, n, CONV_DIM), F32),
        jax.ShapeDtypeStruct((n, CONV_DIM), F32),
        jax.ShapeDtypeStruct((D_SSM, n), F32),
        jax.ShapeDtypeStruct((D_SSM, n), F32),
    )
    return pl.pallas_call(
        _post_sample_kernel,
        grid=(1,),
        in_specs=[full(a) for a in args],
        out_specs=tuple(_full(s.shape) for s in out_shape),
        out_shape=out_shape,
        compiler_params=pltpu.CompilerParams(dimension_semantics=("arbitrary",),
                                             vmem_limit_bytes=VMEM_LIMIT_BYTES),
        name="post_sample",
    )(*args)


def _state_sample_kernel(h0_ref, xdt_t_ref, dec_t_ref, bm_ref, cm_ref, h_ref, y_t_ref):
    t = pl.program_id(0)
    tb = h0_ref.shape[0]
    rows, n_seq = xdt_t_ref.shape

    @pl.when(t == 0)
    def _():
        y_t_ref[...] = jnp.zeros(y_t_ref.shape, F32)

    lane = lax.broadcasted_iota(jnp.int32, (rows, n_seq), 1)
    xdt_t = xdt_t_ref[...]
    dec_t = dec_t_ref[...]
    for j in range(tb):
        own = lane == t * tb + j
        x_col = jnp.sum(jnp.where(own, xdt_t, 0.0), axis=1, keepdims=True)
        d_col = jnp.sum(jnp.where(own, dec_t, 0.0), axis=1, keepdims=True)
        b_rows = jnp.concatenate(
            [jnp.broadcast_to(bm_ref[j:j + 1, g * D_STATE:(g + 1) * D_STATE], (GROUP_WIDTH, D_STATE))
             for g in range(SSD_GROUPS)], axis=0)
        c_rows = jnp.concatenate(
            [jnp.broadcast_to(cm_ref[j:j + 1, g * D_STATE:(g + 1) * D_STATE], (GROUP_WIDTH, D_STATE))
             for g in range(SSD_GROUPS)], axis=0)
        h = d_col * h0_ref[j] + x_col * b_rows
        h_ref[j] = h
        y_col = jnp.sum(h * c_rows, axis=1, keepdims=True)
        y_t_ref[...] = jnp.where(own, y_col, y_t_ref[...])


def _state_sample(h0, xdt_t, dec_t, act):
    n = h0.shape[0]
    tb = DEC_STATE_SEQS
    bc_w = SSD_GROUPS * D_STATE
    st = pl.BlockSpec((tb, D_SSM, D_STATE), lambda i: (i, 0, 0))
    return pl.pallas_call(
        _state_sample_kernel,
        grid=(n // tb,),
        in_specs=[st, _full((D_SSM, n)), _full((D_SSM, n)),
                  pl.BlockSpec((tb, bc_w), lambda i: (i, D_SSM // bc_w)),
                  pl.BlockSpec((tb, bc_w), lambda i: (i, D_SSM // bc_w + 1))],
        out_specs=(st, _full((D_SSM, n))),
        out_shape=(jax.ShapeDtypeStruct(h0.shape, F32), jax.ShapeDtypeStruct((D_SSM, n), F32)),
        compiler_params=pltpu.CompilerParams(dimension_semantics=("arbitrary",),
                                             vmem_limit_bytes=VMEM_LIMIT_BYTES),
        name="state_sample",
    )(h0, xdt_t, dec_t, act, act)


def _finish_sample_kernel(y_t_ref, act_ref, z_ref, dskip_ref, normw_ref, ma_ref, x_ref, wout_ref, npost_ref,
                          y_ref):
    mix_s = _ssd_finish(y_t_ref[...].T, act_ref[...], z_ref[...], dskip_ref[...], normw_ref[...])
    y_ref[...] = _out_proj(_out_proj_attn(ma_ref[...], wout_ref), mix_s, wout_ref, npost_ref[...], x_ref[...])


def _finish_sample(y_t, act, zs, mix_a, x2, w):
    n = x2.shape[0]
    return pl.pallas_call(
        _finish_sample_kernel,
        grid=(1,),
        in_specs=[_full(y_t.shape), pl.BlockSpec((n, D_SSM), lambda i: (0, 0)), _full(zs.shape),
                  _full((1, D_SSM)), _full((1, D_SSM)), _full(mix_a.shape), _full(x2.shape),
                  _full((D_MIX, D_MODEL)), _full((1, D_MODEL))],
        out_specs=_full((n, D_MODEL)),
        out_shape=jax.ShapeDtypeStruct((n, D_MODEL), F32),
        compiler_params=pltpu.CompilerParams(dimension_semantics=("arbitrary",),
                                             vmem_limit_bytes=VMEM_LIMIT_BYTES),
        name="finish_sample",
    )(y_t, act, zs, w['d_skip_w'], w['ssm_norm'], mix_a, x2, w['w_out'], w['norm_post'])


ORIG_KR = Q_RANK + KV_RANK
ORIG_ZA = ORIG_KR + QK_ROPE
ORIG_DT = ORIG_ZA + D_ATTN + D_SSM + CONV_DIM
PREP_COLS = 128


def _prep_w_in_kernel(wt_ref, o_ref):
    def put(tile, src_rows):
        o_ref[:, tile * LANES:(tile + 1) * LANES] = src_rows.T.astype(BF16)

    for t in range(COL_ZA // LANES):
        put(t, wt_ref[t * LANES:(t + 1) * LANES, :])
    for t in range((COL_MISC - COL_ZA) // LANES):
        put(COL_ZA // LANES + t, wt_ref[ORIG_ZA + t * LANES:ORIG_ZA + (t + 1) * LANES, :])
    pad = jnp.zeros((LANES - QK_ROPE - SSD_HEADS, wt_ref.shape[1]), F32)
    put(COL_MISC // LANES, jnp.concatenate([wt_ref[ORIG_KR:ORIG_ZA, :], wt_ref[ORIG_DT:ORIG_DT + SSD_HEADS, :], pad],
                                           axis=0))


def _prep_w_in(w_in):
    k, n = w_in.shape
    assert n == ORIG_DT + SSD_HEADS and k % PREP_COLS == 0
    return pl.pallas_call(
        _prep_w_in_kernel,
        grid=(k // PREP_COLS,),
        in_specs=[pl.BlockSpec((n, PREP_COLS), lambda i: (0, i))],
        out_specs=pl.BlockSpec((PREP_COLS, D_IN_PAD), lambda i: (i, 0)),
        out_shape=jax.ShapeDtypeStruct((k, D_IN_PAD), BF16),
        compiler_params=pltpu.CompilerParams(dimension_semantics=("arbitrary",),
                                             vmem_limit_bytes=VMEM_LIMIT_BYTES),
        name="prep_w_in",
    )(w_in.T)


def _rope_tables(pos):
    inv_freq = ROPE_THETA ** (-jnp.arange(ROPE_HALF, dtype=F32) / ROPE_HALF)
    ang = pos.astype(F32)[:, None] * inv_freq[None, :]
    cos, sin = jnp.cos(ang), jnp.sin(ang)
    reps = LANES // QK_ROPE
    return jnp.tile(cos, (1, 2 * reps)), jnp.tile(jnp.concatenate([-sin, sin], axis=1), (1, reps))


def _prep_weights(lw):
    row = lambda v: v.reshape(1, -1).astype(F32)
    w_qb = lw['w_q_b'].reshape(Q_RANK, MLA_HEADS, QK_NOPE + QK_ROPE)
    lane_pad = lambda v: jnp.pad(v.reshape(1, -1).astype(F32), ((0, 0), (DT_LANE0, LANES - DT_LANE0 - SSD_HEADS)))
    head_of_col = jnp.arange(D_SSM) // SSD_HEADDIM
    return {
        'norm_pre': row(lw['norm_pre']),
        'w_in': _prep_w_in(lw['w_in']),
        'q_a_norm': row(lw['q_a_norm']),
        'w_qb': jnp.concatenate([w_qb[:, :, :QK_NOPE].reshape(Q_RANK, -1),
                                 w_qb[:, :, QK_NOPE:].reshape(Q_RANK, -1)], axis=1).astype(BF16),
        'kv_a_norm': row(lw['kv_a_norm']),
        'w_uk2t': lw['w_uk'].reshape(KV_RANK, MLA_HEADS * QK_NOPE).T.astype(BF16),
        'w_uv2': lw['w_uv'].reshape(KV_RANK, MLA_HEADS * V_HEAD).astype(BF16),
        'w_ukt': jnp.transpose(lw['w_uk'], (1, 2, 0)).astype(BF16),
        'w_uvh': jnp.transpose(lw['w_uv'], (1, 0, 2)).astype(BF16),
        'conv_w': lw['conv_w'].astype(F32),
        'conv_b': row(lw['conv_b']),
        'dt_bias_t': lane_pad(lw['dt_bias']),
        'a_log_t': lane_pad(lw['a_log']),
        'expand': sum((jnp.arange(LANES)[:, None] == DT_LANE0 + k * SSD_HEADS + head_of_col[None, :])
                      for k in range(SPLIT_PIECES)).astype(BF16),
        'd_skip_w': jnp.repeat(lw['d_skip'].astype(F32), SSD_HEADDIM).reshape(1, D_SSM),
        'ssm_norm': row(lw['ssm_norm']),
        'w_out': lw['w_out'].astype(BF16),
        'norm_post': row(lw['norm_post']),
    }


def _prompt_layer(x, w):
    b, s, _ = x.shape
    x2 = x.reshape(b * s, D_MODEL)
    cos, sin = _rope_tables(jnp.arange(s, dtype=jnp.int32))
    q, kt, v, ckv, kr, za, mix_s, state, tail = _proj_prompt(x2, cos, sin, w)
    r3 = lambda a: a.reshape(b, s, a.shape[-1])
    mix_a = _attn_prompt(r3(q), kt, r3(v), r3(za))
    y = _merge(mix_a.reshape(b * s, D_ATTN), mix_s, x2, w)
    h = state.reshape(b, SSD_GROUPS, D_STATE, HEADS_PER_GROUP, SSD_HEADDIM)
    h = jnp.transpose(h, (0, 1, 3, 4, 2)).reshape(b, SSD_HEADS, SSD_HEADDIM, D_STATE)
    return (y.reshape(b, s, D_MODEL), r3(ckv), jnp.swapaxes(kr, 1, 2), tail[:, CONV_PAD - (CONV_W - 1):, :], h)


def _sample_layer(layer, x, cache_c, cache_r, conv_prev, h0, page_table, w):
    b, s, _ = x.shape
    n = b * s
    past = page_table.shape[1] * PAGE_SIZE
    x2 = x.reshape(n, D_MODEL)
    pos = past + jnp.arange(s, dtype=jnp.int32)
    cos, sin = _rope_tables(jnp.tile(pos, b))
    qlat, qrope, ckv, kr, za, zs, xbc, misc = _proj_sample(x2, cos, sin, w)
    olat = _attn_sample(layer, page_table, qlat.reshape(n, MLA_HEADS, KV_RANK), qrope.reshape(n, MLA_HEADS, QK_ROPE),
                        ckv.reshape(n, 1, KV_RANK), kr.reshape(n, 1, QK_ROPE), cache_c,
                        jnp.swapaxes(cache_r, 2, 3))
    mix_a, conv_new, act, xdt_t, dec_t = _post_sample(
        olat.reshape(n, MLA_HEADS * KV_RANK), za, xbc, misc, jnp.swapaxes(conv_prev, 0, 1), w)
    h, y_t = _state_sample(h0.astype(F32).reshape(n, D_SSM, D_STATE), xdt_t, dec_t, act)
    y = _finish_sample(y_t, act, zs, mix_a, x2, w)
    return (y.reshape(b, s, D_MODEL), ckv.reshape(b, s, KV_RANK), kr.reshape(b, s, QK_ROPE),
            jnp.swapaxes(conv_new, 0, 1), h.reshape(b, SSD_HEADS, SSD_HEADDIM, D_STATE))


def kernel(x_prompt, x_sample, cache_ckv, cache_krope, state_conv, state_ssm, page_table, norm_pre, w_in,
           q_a_norm, w_q_b, kv_a_norm, w_uk, w_uv, conv_w, conv_b, dt_bias, a_log, d_skip, ssm_norm, w_out,
           norm_post):
    assert x_sample.shape[1] == 1, "the sample path handles one new token per sequence"
    depth = w_in.shape[0]
    y_prompt, y_sample = x_prompt, x_sample
    outs = [[] for _ in range(8)]
    for l in range(depth):
        w = _prep_weights({'norm_pre': norm_pre[l], 'w_in': w_in[l], 'q_a_norm': q_a_norm[l],
                           'w_q_b': w_q_b[l], 'kv_a_norm': kv_a_norm[l], 'w_uk': w_uk[l], 'w_uv': w_uv[l],
                           'conv_w': conv_w[l], 'conv_b': conv_b[l], 'dt_bias': dt_bias[l], 'a_log': a_log[l],
                           'd_skip': d_skip[l], 'ssm_norm': ssm_norm[l], 'w_out': w_out[l],
                           'norm_post': norm_post[l]})
        y_prompt, c1, k1, v1, h1 = _prompt_layer(y_prompt, w)
        y_sample, c2, k2, v2, h2 = _sample_layer(l, y_sample, cache_ckv, cache_krope, state_conv[l],
                                                 state_ssm[l], page_table, w)
        for lst, val in zip(outs, (c1, k1, v1, h1, c2, k2, v2, h2)):
            lst.append(val)
    return (y_prompt, y_sample) + tuple(jnp.stack(o) for o in outs)
```

```python
import functools
import math

import jax
import jax.numpy as jnp
from jax import lax
from jax.experimental import pallas as pl
from jax.experimental.pallas import tpu as pltpu

F32 = jnp.float32
BF16 = jnp.bfloat16

D_MODEL = 1024
PAGE_SIZE = 128
D_MIX = 2 * D_MODEL
D_ATTN = D_MIX // 2
D_SSM = D_MIX - D_ATTN
MLA_HEADS = 8
QK_NOPE = 128
QK_ROPE = 64
ROPE_HALF = QK_ROPE // 2
V_HEAD = D_ATTN // MLA_HEADS
Q_RANK = 384
KV_RANK = 256
ROPE_THETA = 10000.0
SOFTMAX_SCALE = (QK_NOPE + QK_ROPE) ** -0.5
SSD_HEADDIM = 64
SSD_HEADS = D_SSM // SSD_HEADDIM
SSD_GROUPS = 2
HEADS_PER_GROUP = SSD_HEADS // SSD_GROUPS
GROUP_WIDTH = D_SSM // SSD_GROUPS
D_STATE = 128
CONV_W = 4
CONV_DIM = D_SSM + 2 * SSD_GROUPS * D_STATE
CHUNK = 128
EPS = 1e-6
SPLIT_PIECES = 3
NEG_BIG = -1e30

LANES = 128
SUBLANES = 8
VMEM_LIMIT_BYTES = 56 * 1024 * 1024

COL_Q = 0
COL_C = COL_Q + Q_RANK
COL_ZA = COL_C + KV_RANK
COL_ZS = COL_ZA + D_ATTN
COL_XBC = COL_ZS + D_SSM
COL_MISC = COL_XBC + CONV_DIM
D_IN_PAD = COL_MISC + LANES
DT_LANE0 = QK_ROPE
QK_PAD = 2 * LANES

PROJ_ROWS = 512
ATTN_TQ = 256
ATTN_HEADS = 4
SSD_STEP_CHUNKS = 4
MERGE_ROWS = 1024
CONV_PAD = SUBLANES
DEC_STATE_SEQS = 8
DEC_KV_CHUNK = 2048
DEC_ATTN_SEQS = 2

_NT = (((1,), (1,)), ((), ()))


def _rms(x, w):
    return x * lax.rsqrt(jnp.mean(x * x, axis=-1, keepdims=True) + EPS) * w


def _silu(x):
    return x / (1.0 + jnp.exp(-x))


def _softplus(x):
    return jnp.maximum(x, 0.0) + jnp.log(1.0 + jnp.exp(-jnp.abs(x)))


def _dot(a, b):
    return jnp.dot(a, b, preferred_element_type=F32)


def _rope_tile(x, cos, sin_signed):
    lane = lax.broadcasted_iota(jnp.int32, x.shape, 1)
    first_half = (lane % QK_ROPE) < ROPE_HALF
    partner = jnp.where(first_half,
                        pltpu.roll(x, LANES - ROPE_HALF, 1),
                        pltpu.roll(x, ROPE_HALF, 1))
    return x * cos + partner * sin_signed


def _front(x_ref, npre_ref, win_ref):
    h = _rms(x_ref[...], npre_ref[...]).astype(BF16)

    def seg(lo, hi):
        return _dot(h, win_ref[:, lo:hi])

    return seg


def _proj_prompt_kernel(seq_tiles, x_ref, cos_ref, sin_ref, npre_ref, win_ref, qan_ref, wqb_ref, kvn_ref,
                        wukt_ref, wuv_ref, cw_ref, cb_ref, dtb_ref, alog_ref, expand_ref, dskip_ref, normw_ref,
                        q_ref, kt_ref, v_ref, ckv_ref, kr_ref, za_ref, mixs_ref, state_ref, tail_ref, xp_ref):
    rows = x_ref.shape[0]
    first = pl.program_id(0) % seq_tiles == 0

    @pl.when(first)
    def _():
        xp_ref[0:CONV_PAD, :] = jnp.zeros((CONV_PAD, CONV_DIM), F32)
        state_ref[...] = jnp.zeros(state_ref.shape, F32)

    seg = _front(x_ref, npre_ref, win_ref)
    xp_ref[CONV_PAD:CONV_PAD + rows, :] = seg(COL_XBC, COL_MISC)
    xp = xp_ref[...]
    acc = cb_ref[...]
    for tap in range(CONV_W):
        back = CONV_W - 1 - tap
        shifted = xp if back == 0 else pltpu.roll(xp, back, 0)
        acc = acc + shifted[CONV_PAD:CONV_PAD + rows, :] * cw_ref[tap:tap + 1, :]
    act_all = _silu(acc)
    tail_ref[...] = xp_ref[rows:rows + CONV_PAD, :]
    xp_ref[CONV_PAD - 3:CONV_PAD, :] = xp_ref[CONV_PAD + rows - 3:CONV_PAD + rows, :]

    misc = seg(COL_MISC, D_IN_PAD)
    cos = cos_ref[...]
    sin = sin_ref[...]
    lane = lax.broadcasted_iota(jnp.int32, misc.shape, 1)
    low = lane < QK_ROPE

    kr_full = _rope_tile(misc, cos, sin)
    kr_t = jnp.where(low, kr_full, 0.0).T
    kr_ref[...] = kr_t[:QK_ROPE, :]
    kr_lo_t = kr_t.astype(BF16)
    kr_hi_t = jnp.where(low, 0.0, pltpu.roll(kr_full, QK_ROPE, 1)).T.astype(BF16)

    c_raw = seg(COL_C, COL_ZA)
    q_a = seg(COL_Q, COL_C)
    za_ref[...] = seg(COL_ZA, COL_ZS)
    ckv = _rms(c_raw, kvn_ref[...])
    ckv_ref[...] = ckv
    cb = ckv.astype(BF16)
    knope_t = lax.dot_general(wukt_ref[...], cb, _NT, preferred_element_type=F32)
    v_ref[...] = _dot(cb, wuv_ref[...]).astype(BF16)

    qn = _rms(q_a, qan_ref[...]).astype(BF16)
    q = _dot(qn, wqb_ref[...])
    nope_w = MLA_HEADS * QK_NOPE
    for hh in range(MLA_HEADS):
        pair = hh // 2
        r = _rope_tile(q[:, nope_w + pair * LANES: nope_w + (pair + 1) * LANES], cos, sin)
        own = low if hh % 2 == 0 else jnp.logical_not(low)
        base = hh * QK_PAD
        q_ref[:, base:base + LANES] = (q[:, hh * QK_NOPE:(hh + 1) * QK_NOPE] * SOFTMAX_SCALE).astype(BF16)
        q_ref[:, base + LANES:base + QK_PAD] = (jnp.where(own, r, 0.0) * SOFTMAX_SCALE).astype(BF16)
        kt_ref[base:base + LANES, :] = knope_t[hh * QK_NOPE:(hh + 1) * QK_NOPE, :].astype(BF16)
        kt_ref[base + LANES:base + QK_PAD, :] = kr_lo_t if hh % 2 == 0 else kr_hi_t

    zs = seg(COL_ZS, COL_XBC)
    for k in range(rows // CHUNK):
        r0 = k * CHUNK
        act = act_all[r0:r0 + CHUNK, :]
        y = _ssd_chunk(act, misc[r0:r0 + CHUNK, :], dtb_ref, alog_ref, expand_ref, state_ref)
        mixs_ref[r0:r0 + CHUNK, :] = _ssd_finish(y, act[:, :D_SSM], zs[r0:r0 + CHUNK, :], dskip_ref[...],
                                                 normw_ref[...])


def _proj_sample_kernel(x_ref, cos_ref, sin_ref, npre_ref, win_ref, qan_ref, wqb_ref, kvn_ref, wukt_ref,
                        qlat_ref, qrope_ref, ckv_ref, kr_ref, za_ref, zs_ref, xbc_ref, misc_ref):
    seg = _front(x_ref, npre_ref, win_ref)
    za_ref[...] = seg(COL_ZA, COL_ZS)
    zs_ref[...] = seg(COL_ZS, COL_XBC)
    xbc_ref[...] = seg(COL_XBC, COL_MISC)
    misc = seg(COL_MISC, D_IN_PAD)
    misc_ref[...] = misc
    cos = cos_ref[...]
    sin = sin_ref[...]
    kr_ref[...] = _rope_tile(misc, cos, sin)[:, :QK_ROPE]
    ckv_ref[...] = _rms(seg(COL_C, COL_ZA), kvn_ref[...])

    qn = _rms(seg(COL_Q, COL_C), qan_ref[...]).astype(BF16)
    q = _dot(qn, wqb_ref[...])
    nope_w = MLA_HEADS * QK_NOPE
    for pair in range(MLA_HEADS // 2):
        lo = nope_w + pair * LANES
        qrope_ref[:, pair * LANES:(pair + 1) * LANES] = _rope_tile(q[:, lo:lo + LANES], cos, sin) * SOFTMAX_SCALE
    for hh in range(MLA_HEADS):
        qh = q[:, hh * QK_NOPE:(hh + 1) * QK_NOPE].astype(BF16)
        qlat_ref[:, hh * KV_RANK:(hh + 1) * KV_RANK] = _dot(qh, wukt_ref[hh]) * SOFTMAX_SCALE


def _full(shape):
    return pl.BlockSpec(shape, lambda *_: (0,) * len(shape))


def _proj_prompt(x2, cos, sin, w):
    n = x2.shape[0]
    tm = PROJ_ROWS
    seq_tiles = cos.shape[0] // tm
    assert tm == SSD_STEP_CHUNKS * CHUNK
    batch = n // cos.shape[0]
    rows = lambda width: pl.BlockSpec((tm, width), lambda i: (i, 0))
    tab = pl.BlockSpec((tm, LANES), lambda i: (i % seq_tiles, 0))
    out_shape = (
        jax.ShapeDtypeStruct((n, MLA_HEADS * QK_PAD), BF16),
        jax.ShapeDtypeStruct((MLA_HEADS * QK_PAD, n), BF16),
        jax.ShapeDtypeStruct((n, D_ATTN), BF16),
        jax.ShapeDtypeStruct((n, KV_RANK), F32),
        jax.ShapeDtypeStruct((batch, QK_ROPE, cos.shape[0]), F32),
        jax.ShapeDtypeStruct((n, D_ATTN), F32),
        jax.ShapeDtypeStruct((n, D_SSM), BF16),
        jax.ShapeDtypeStruct((batch, SSD_GROUPS, D_STATE, GROUP_WIDTH), F32),
        jax.ShapeDtypeStruct((batch, CONV_PAD, CONV_DIM), F32),
    )
    per_seq = lambda shape: pl.BlockSpec((None,) + shape[1:], lambda i: (i // seq_tiles,) + (0,) * (len(shape) - 1))
    out_specs = [rows(s.shape[1]) for s in out_shape[:7]]
    out_specs[1] = pl.BlockSpec((MLA_HEADS * QK_PAD, tm), lambda i: (0, i))
    out_specs[4] = pl.BlockSpec((None, QK_ROPE, tm), lambda i: (i // seq_tiles, 0, i % seq_tiles))
    out_specs += [per_seq(out_shape[7].shape), per_seq(out_shape[8].shape)]
    return pl.pallas_call(
        functools.partial(_proj_prompt_kernel, seq_tiles),
        grid=(n // tm,),
        in_specs=[rows(D_MODEL), tab, tab, _full((1, D_MODEL)), _full((D_MODEL, D_IN_PAD)),
                  _full((1, Q_RANK)), _full(w['w_qb'].shape), _full((1, KV_RANK)),
                  _full(w['w_uk2t'].shape), _full(w['w_uv2'].shape),
                  _full((CONV_W, CONV_DIM)), _full((1, CONV_DIM)), _full((1, LANES)), _full((1, LANES)),
                  _full((LANES, D_SSM)), _full((1, D_SSM)), _full((1, D_SSM))],
        out_specs=tuple(out_specs),
        out_shape=out_shape,
        scratch_shapes=[pltpu.VMEM((CONV_PAD + tm, CONV_DIM), F32)],
        compiler_params=pltpu.CompilerParams(dimension_semantics=("arbitrary",),
                                             vmem_limit_bytes=VMEM_LIMIT_BYTES),
        name="proj_prompt",
    )(x2, cos, sin, w['norm_pre'], w['w_in'], w['q_a_norm'], w['w_qb'], w['kv_a_norm'],
      w['w_uk2t'], w['w_uv2'], w['conv_w'], w['conv_b'], w['dt_bias_t'], w['a_log_t'], w['expand'],
      w['d_skip_w'], w['ssm_norm'])


def _proj_sample(x2, cos, sin, w):
    n = x2.shape[0]
    rows = lambda width: pl.BlockSpec((n, width), lambda i: (0, 0))
    out_shape = (
        jax.ShapeDtypeStruct((n, MLA_HEADS * KV_RANK), F32),
        jax.ShapeDtypeStruct((n, MLA_HEADS * QK_ROPE), F32),
        jax.ShapeDtypeStruct((n, KV_RANK), F32),
        jax.ShapeDtypeStruct((n, QK_ROPE), F32),
        jax.ShapeDtypeStruct((n, D_ATTN), F32),
        jax.ShapeDtypeStruct((n, D_SSM), F32),
        jax.ShapeDtypeStruct((n, CONV_DIM), F32),
        jax.ShapeDtypeStruct((n, LANES), F32),
    )
    return pl.pallas_call(
        _proj_sample_kernel,
        grid=(1,),
        in_specs=[rows(D_MODEL), rows(LANES), rows(LANES), _full((1, D_MODEL)), _full((D_MODEL, D_IN_PAD)),
                  _full((1, Q_RANK)), _full(w['w_qb'].shape), _full((1, KV_RANK)),
                  _full(w['w_ukt'].shape)],
        out_specs=tuple(rows(s.shape[1]) for s in out_shape),
        out_shape=out_shape,
        compiler_params=pltpu.CompilerParams(dimension_semantics=("arbitrary",),
                                             vmem_limit_bytes=VMEM_LIMIT_BYTES),
        name="proj_sample",
    )(x2, cos, sin, w['norm_pre'], w['w_in'], w['q_a_norm'], w['w_qb'], w['kv_a_norm'], w['w_ukt'])


def _attn_prompt_kernel(q_ref, kt_ref, v_ref, z_ref, o_ref):
    seq = q_ref.shape[0]
    tq = ATTN_TQ
    row = lax.broadcasted_iota(jnp.int32, (tq, tq), 0)
    col = lax.broadcasted_iota(jnp.int32, (tq, tq), 1)
    causal = col <= row
    def score_phase(qi):
        lim = (qi + 1) * tq
        return [_dot(q_ref[qi * tq:lim, hd * QK_PAD:(hd + 1) * QK_PAD], kt_ref[hd * QK_PAD:(hd + 1) * QK_PAD, 0:lim])
                for hd in range(ATTN_HEADS)]

    def softmax_phase(qi, scores):
        lim = (qi + 1) * tq
        probs = []
        for s in scores:
            diag = jnp.where(causal, s[:, lim - tq:], NEG_BIG)
            s = diag if qi == 0 else jnp.concatenate([s[:, :lim - tq], diag], axis=1)
            m = jnp.max(s, axis=-1, keepdims=True)
            probs.append(jnp.exp(s - m).astype(BF16))
        return probs

    def value_phase(qi, probs):
        lim = (qi + 1) * tq
        ones_col = (lax.broadcasted_iota(jnp.int32, (lim, LANES), 1) == 0).astype(BF16)
        for hd, p in enumerate(probs):
            hv = slice(hd * V_HEAD, (hd + 1) * V_HEAD)
            ol = _dot(p, jnp.concatenate([v_ref[0:lim, hv], ones_col], axis=1))
            o = ol[:, :V_HEAD] / jnp.sum(ol[:, V_HEAD:], axis=-1, keepdims=True)
            o_ref[qi * tq:lim, hv] = (o * _silu(z_ref[qi * tq:lim, hv])).astype(BF16)

    for qi in reversed(range(seq // tq)):
        value_phase(qi, softmax_phase(qi, score_phase(qi)))


def _attn_prompt(q3, kt, v3, z3):
    b, s, _ = q3.shape
    nh = ATTN_HEADS
    hv = pl.BlockSpec((None, s, nh * V_HEAD), lambda i, j: (i, 0, j))
    return pl.pallas_call(
        _attn_prompt_kernel,
        grid=(b, MLA_HEADS // nh),
        in_specs=[pl.BlockSpec((None, s, nh * QK_PAD), lambda i, j: (i, 0, j)),
                  pl.BlockSpec((nh * QK_PAD, s), lambda i, j: (j, i)), hv, hv],
        out_specs=hv,
        out_shape=jax.ShapeDtypeStruct((b, s, D_ATTN), BF16),
        compiler_params=pltpu.CompilerParams(dimension_semantics=("arbitrary", "arbitrary"),
                                             vmem_limit_bytes=VMEM_LIMIT_BYTES),
        name="attn_prompt",
    )(q3, kt, v3, z3)


def _dt_lanes(shape):
    lane = lax.broadcasted_iota(jnp.int32, shape, 1)
    return jnp.logical_and(lane >= DT_LANE0, lane < DT_LANE0 + SSD_HEADS)


def _dt_and_da(misc, dtb, alog):
    dt = jnp.where(_dt_lanes(misc.shape), _softplus(misc + dtb), 0.0)
    return dt, dt * (-jnp.exp(alog))


def _split_bf16(x):
    pieces = []
    for _ in range(SPLIT_PIECES):
        piece = x.astype(BF16).astype(F32)
        pieces.append(piece)
        x = x - piece
    return pieces


def _cumsum_rows(lower_b, x):
    return sum(_dot(lower_b, piece.astype(BF16)) for piece in _split_bf16(x))


def _head_expand(x, expand_ref):
    pieces = _split_bf16(jnp.where(_dt_lanes(x.shape), x, 0.0))
    packed = pieces[0]
    for k in range(1, SPLIT_PIECES):
        packed = packed + pltpu.roll(pieces[k], k * SSD_HEADS, 1)
    return _dot(packed.astype(BF16), expand_ref[...])


def _ssd_finish(y, xs, z, dskip, normw):
    y = y + dskip * xs
    gated = y * _silu(z)
    outs = []
    for g in range(SSD_GROUPS):
        sl = slice(g * GROUP_WIDTH, (g + 1) * GROUP_WIDTH)
        outs.append(_rms(gated[:, sl], normw[:, sl]))
    return jnp.concatenate(outs, axis=1).astype(BF16)


def _out_proj_attn(mix_a, wout_ref):
    return _dot(mix_a, wout_ref[0:D_ATTN, :])


def _out_proj(o_attn, mix_s, wout_ref, npost, x):
    o = o_attn + _dot(mix_s, wout_ref[D_ATTN:D_MIX, :])
    return x + _rms(o, npost)


def _ssd_chunk(act, misc, dtb_ref, alog_ref, expand_ref, state_ref):
    xs = act[:, :D_SSM]
    dt, da = _dt_and_da(misc, dtb_ref[...], alog_ref[...])
    row = lax.broadcasted_iota(jnp.int32, (CHUNK, CHUNK), 0)
    col = lax.broadcasted_iota(jnp.int32, (CHUNK, CHUNK), 1)
    lower = row >= col
    a_cum = _cumsum_rows(lower.astype(BF16), da)
    a_cum_t = a_cum.T
    a_last = a_cum[CHUNK - 1:CHUNK, :]
    xdt = xs * _head_expand(dt, expand_ref)
    decay_out = _head_expand(jnp.exp(a_cum), expand_ref)
    state_decay = decay_out[CHUNK - 1:CHUNK, :]
    xdt_end = (xdt * _head_expand(jnp.exp(a_last - a_cum), expand_ref)).astype(BF16)
    xdt_b = xdt.astype(BF16)
    lane_w = lax.broadcasted_iota(jnp.int32, (CHUNK, LANES), 1)
    first_head = lane_w < SSD_HEADDIM

    ys = []
    for g in range(SSD_GROUPS):
        bm = act[:, D_SSM + g * D_STATE:D_SSM + (g + 1) * D_STATE]
        cm = act[:, D_SSM + (SSD_GROUPS + g) * D_STATE:D_SSM + (SSD_GROUPS + g + 1) * D_STATE]
        bm_b = bm.astype(BF16)
        cm_b = cm.astype(BF16)
        cb = lax.dot_general(cm_b, bm_b, _NT, preferred_element_type=F32)
        gsl = slice(g * GROUP_WIDTH, (g + 1) * GROUP_WIDTH)
        state = state_ref[g]
        y_off = _dot(cm_b, state.astype(BF16)) * decay_out[:, gsl]
        y_diag = []
        for pair in range(HEADS_PER_GROUP // 2):
            halves = []
            x_pair = xdt_b[:, g * GROUP_WIDTH + pair * LANES: g * GROUP_WIDTH + (pair + 1) * LANES]
            for k in range(2):
                lane_h = DT_LANE0 + g * HEADS_PER_GROUP + 2 * pair + k
                seg = a_cum[:, lane_h:lane_h + 1] - a_cum_t[lane_h:lane_h + 1, :]
                decay = jnp.exp(jnp.where(lower, seg, NEG_BIG))
                halves.append(_dot((cb * decay).astype(BF16), x_pair))
            y_diag.append(jnp.where(first_head, halves[0], halves[1]))
        ys.append(jnp.concatenate(y_diag, axis=1) + y_off)
        state_ref[g] = state * state_decay[:, gsl] + _dot(bm.T.astype(BF16), xdt_end[:, gsl])
    return jnp.concatenate(ys, axis=1)


def _merge_kernel(ma_ref, ms_ref, x_ref, wout_ref, npost_ref, y_ref):
    y_ref[...] = _out_proj(_out_proj_attn(ma_ref[...], wout_ref), ms_ref[...], wout_ref, npost_ref[...], x_ref[...])


def _merge(mix_a, mix_s, x2, w):
    n = x2.shape[0]
    tm = MERGE_ROWS
    rows = lambda width: pl.BlockSpec((tm, width), lambda i: (i, 0))
    return pl.pallas_call(
        _merge_kernel,
        grid=(n // tm,),
        in_specs=[rows(D_ATTN), rows(D_SSM), rows(D_MODEL), _full((D_MIX, D_MODEL)), _full((1, D_MODEL))],
        out_specs=rows(D_MODEL),
        out_shape=jax.ShapeDtypeStruct((n, D_MODEL), F32),
        compiler_params=pltpu.CompilerParams(dimension_semantics=("arbitrary",),
                                             vmem_limit_bytes=VMEM_LIMIT_BYTES),
        name="merge",
    )(mix_a, mix_s, x2, w['w_out'], w['norm_post'])


def _attn_sample_kernel(layer, pt_ref, qlat_ref, qrope_ref, cnew_ref, rnew_ref, cache_c_ref, cache_rt_ref,
                        o_ref, cbuf, rbuf, sem):
    i = pl.program_id(0)
    n = pl.num_programs(0)
    n_pages = pt_ref.shape[1]
    seqs = qlat_ref.shape[0]
    past = n_pages * PAGE_SIZE
    slot = i % 2

    def page_copies(step, slot_):
        copies = []
        for j in range(seqs):
            buf = slot_ * seqs + j
            for p in range(n_pages):
                page = pt_ref[step * seqs + j, p]
                dst = pl.ds(p * PAGE_SIZE, PAGE_SIZE)
                copies.append(pltpu.make_async_copy(cache_c_ref.at[layer, page], cbuf.at[buf, dst],
                                                    sem.at[0, slot_]))
                copies.append(pltpu.make_async_copy(cache_rt_ref.at[layer, page], rbuf.at[buf, :, dst],
                                                    sem.at[1, slot_]))
        return copies

    @pl.when(i == 0)
    def _():
        for cp in page_copies(0, 0):
            cp.start()

    @pl.when(i + 1 < n)
    def _():
        for cp in page_copies(i + 1, 1 - slot):
            cp.start()

    for cp in page_copies(i, slot):
        cp.wait()

    q_pos = past
    chunks = [pl.ds(c * DEC_KV_CHUNK, DEC_KV_CHUNK) for c in range(past // DEC_KV_CHUNK)]
    scores = []
    for j in range(seqs):
        buf = slot * seqs + j
        qlat = qlat_ref[j]
        qrope = qrope_ref[j]
        scores.append(jnp.concatenate(
            [lax.dot_general(qlat, cbuf[buf, keys, :], _NT, preferred_element_type=F32)
             + _dot(qrope, rbuf[buf, :, keys]) for keys in chunks], axis=1))
    probs = []
    for j in range(seqs):
        cnew = cnew_ref[j]
        s_new = (jnp.sum(qlat_ref[j] * cnew, axis=-1, keepdims=True)
                 + jnp.sum(qrope_ref[j] * rnew_ref[j], axis=-1, keepdims=True))
        k_pos = lax.broadcasted_iota(jnp.int32, scores[j].shape, 1)
        s = jnp.where(k_pos <= q_pos, scores[j], NEG_BIG)
        m = jnp.maximum(jnp.max(s, axis=-1, keepdims=True), s_new)
        p = jnp.exp(s - m)
        p_new = jnp.exp(s_new - m)
        probs.append((p, p_new, jnp.sum(p, axis=-1, keepdims=True) + p_new))
    for j in range(seqs):
        buf = slot * seqs + j
        p, p_new, l = probs[j]
        acc = p_new * cnew_ref[j]
        for c, keys in enumerate(chunks):
            acc = acc + _dot(p[:, c * DEC_KV_CHUNK:(c + 1) * DEC_KV_CHUNK], cbuf[buf, keys, :])
        o_ref[j] = acc / l


def _attn_sample(layer, page_table, qlat3, qrope3, cnew3, rnew3, cache_c, cache_rt):
    b, n_pages = page_table.shape
    past = n_pages * PAGE_SIZE
    seqs = DEC_ATTN_SEQS
    assert past % DEC_KV_CHUNK == 0 and b % seqs == 0
    per_step = lambda d1, d2: pl.BlockSpec((seqs, d1, d2), lambda i, pt: (i, 0, 0))
    grid_spec = pltpu.PrefetchScalarGridSpec(
        num_scalar_prefetch=1,
        grid=(b // seqs,),
        in_specs=[per_step(MLA_HEADS, KV_RANK), per_step(MLA_HEADS, QK_ROPE), per_step(1, KV_RANK),
                  per_step(1, QK_ROPE), pl.BlockSpec(memory_space=pl.ANY), pl.BlockSpec(memory_space=pl.ANY)],
        out_specs=per_step(MLA_HEADS, KV_RANK),
        scratch_shapes=[pltpu.VMEM((2 * seqs, past, KV_RANK), F32), pltpu.VMEM((2 * seqs, QK_ROPE, past), F32),
                        pltpu.SemaphoreType.DMA((2, 2))],
    )
    return pl.pallas_call(
        functools.partial(_attn_sample_kernel, layer),
        grid_spec=grid_spec,
        out_shape=jax.ShapeDtypeStruct((b, MLA_HEADS, KV_RANK), F32),
        compiler_params=pltpu.CompilerParams(dimension_semantics=("arbitrary",),
                                             vmem_limit_bytes=VMEM_LIMIT_BYTES),
        name="attn_sample",
    )(page_table, qlat3, qrope3, cnew3, rnew3, cache_c, cache_rt)


def _post_sample_kernel(olat_ref, wuv_ref, za_ref, xbc_ref, misc_ref, cprev_ref, cw_ref, cb_ref,
                        dtb_ref, alog_ref, expand_ref,
                        mixa_ref, cnew_ref, act_ref, xdt_t_ref, dec_t_ref):
    for hh in range(MLA_HEADS):
        o = _dot(olat_ref[:, hh * KV_RANK:(hh + 1) * KV_RANK].astype(BF16), wuv_ref[hh])
        vs = slice(hh * V_HEAD, (hh + 1) * V_HEAD)
        mixa_ref[:, vs] = (o * _silu(za_ref[:, vs])).astype(BF16)

    xb = xbc_ref[...]
    acc = cb_ref[...]
    for k in range(CONV_W - 1):
        acc = acc + cprev_ref[k] * cw_ref[k:k + 1, :]
    acc = acc + xb * cw_ref[CONV_W - 1:CONV_W, :]
    act = _silu(acc)
    act_ref[...] = act
    for k in range(CONV_W - 2):
        cnew_ref[k] = cprev_ref[k + 1]
    cnew_ref[CONV_W - 2] = xb

    dt, da = _dt_and_da(misc_ref[...], dtb_ref[...], alog_ref[...])
    xdt = act[:, :D_SSM] * _head_expand(dt, expand_ref)
    decay = _head_expand(jnp.exp(da), expand_ref)
    xdt_t_ref[...] = xdt.T
    dec_t_ref[...] = decay.T


def _post_sample(olat2, za, xbc, misc, cprev3, w):
    n = olat2.shape[0]
    full = lambda a: _full(a.shape)
    args = (olat2, w['w_uvh'], za, xbc, misc, cprev3, w['conv_w'], w['conv_b'], w['dt_bias_t'], w['a_log_t'],
            w['expand'])
    out_shape = (
        jax.ShapeDtypeStruct((n, D_ATTN), BF16),
        jax.ShapeDtypeStruct((CONV_W - 1, n, CONV_DIM), F32),
        jax.ShapeDtypeStruct((n, CONV_DIM), F32),
        jax.ShapeDtypeStruct((D_SSM, n), F32),
        jax.ShapeDtypeStruct((D_SSM, n), F32),
    )
    return pl.pallas_call(
        _post_sample_kernel,
        grid=(1,),
        in_specs=[full(a) for a in args],
        out_specs=tuple(_full(s.shape) for s in out_shape),
        out_shape=out_shape,
        compiler_params=pltpu.CompilerParams(dimension_semantics=("arbitrary",),
                                             vmem_limit_bytes=VMEM_LIMIT_BYTES),
        name="post_sample",
    )(*args)


def _state_sample_kernel(h0_ref, xdt_t_ref, dec_t_ref, bm_ref, cm_ref, h_ref, y_t_ref):
    t = pl.program_id(0)
    tb = h0_ref.shape[0]
    rows, n_seq = xdt_t_ref.shape

    @pl.when(t == 0)
    def _():
        y_t_ref[...] = jnp.zeros(y_t_ref.shape, F32)

    lane = lax.broadcasted_iota(jnp.int32, (rows, n_seq), 1)
    xdt_t = xdt_t_ref[...]
    dec_t = dec_t_ref[...]
    for j in range(tb):
        own = lane == t * tb + j
        x_col = jnp.sum(jnp.where(own, xdt_t, 0.0), axis=1, keepdims=True)
        d_col = jnp.sum(jnp.where(own, dec_t, 0.0), axis=1, keepdims=True)
        b_rows = jnp.concatenate(
            [jnp.broadcast_to(bm_ref[j:j + 1, g * D_STATE:(g + 1) * D_STATE], (GROUP_WIDTH, D_STATE))
             for g in range(SSD_GROUPS)], axis=0)
        c_rows = jnp.concatenate(
            [jnp.broadcast_to(cm_ref[j:j + 1, g * D_STATE:(g + 1) * D_STATE], (GROUP_WIDTH, D_STATE))
             for g in range(SSD_GROUPS)], axis=0)
        h = d_col * h0_ref[j] + x_col * b_rows
        h_ref[j] = h
        y_col = jnp.sum(h * c_rows, axis=1, keepdims=True)
        y_t_ref[...] = jnp.where(own, y_col, y_t_ref[...])


def _state_sample(h0, xdt_t, dec_t, act):
    n = h0.shape[0]
    tb = DEC_STATE_SEQS
    bc_w = SSD_GROUPS * D_STATE
    st = pl.BlockSpec((tb, D_SSM, D_STATE), lambda i: (i, 0, 0))
    return pl.pallas_call(
        _state_sample_kernel,
        grid=(n // tb,),
        in_specs=[st, _full((D_SSM, n)), _full((D_SSM, n)),
                  pl.BlockSpec((tb, bc_w), lambda i: (i, D_SSM // bc_w)),
                  pl.BlockSpec((tb, bc_w), lambda i: (i, D_SSM // bc_w + 1))],
        out_specs=(st, _full((D_SSM, n))),
        out_shape=(jax.ShapeDtypeStruct(h0.shape, F32), jax.ShapeDtypeStruct((D_SSM, n), F32)),
        compiler_params=pltpu.CompilerParams(dimension_semantics=("arbitrary",),
                                             vmem_limit_bytes=VMEM_LIMIT_BYTES),
        name="state_sample",
    )(h0, xdt_t, dec_t, act, act)


def _finish_sample_kernel(y_t_ref, act_ref, z_ref, dskip_ref, normw_ref, ma_ref, x_ref, wout_ref, npost_ref,
                          y_ref):
    mix_s = _ssd_finish(y_t_ref[...].T, act_ref[...], z_ref[...], dskip_ref[...], normw_ref[...])
    y_ref[...] = _out_proj(_out_proj_attn(ma_ref[...], wout_ref), mix_s, wout_ref, npost_ref[...], x_ref[...])


def _finish_sample(y_t, act, zs, mix_a, x2, w):
    n = x2.shape[0]
    return pl.pallas_call(
        _finish_sample_kernel,
        grid=(1,),
        in_specs=[_full(y_t.shape), pl.BlockSpec((n, D_SSM), lambda i: (0, 0)), _full(zs.shape),
                  _full((1, D_SSM)), _full((1, D_SSM)), _full(mix_a.shape), _full(x2.shape),
                  _full((D_MIX, D_MODEL)), _full((1, D_MODEL))],
        out_specs=_full((n, D_MODEL)),
        out_shape=jax.ShapeDtypeStruct((n, D_MODEL), F32),
        compiler_params=pltpu.CompilerParams(dimension_semantics=("arbitrary",),
                                             vmem_limit_bytes=VMEM_LIMIT_BYTES),
        name="finish_sample",
    )(y_t, act, zs, w['d_skip_w'], w['ssm_norm'], mix_a, x2, w['w_out'], w['norm_post'])


ORIG_KR = Q_RANK + KV_RANK
ORIG_ZA = ORIG_KR + QK_ROPE
ORIG_DT = ORIG_ZA + D_ATTN + D_SSM + CONV_DIM
PREP_COLS = 128


def _prep_w_in_kernel(wt_ref, o_ref):
    def put(tile, src_rows):
        o_ref[:, tile * LANES:(tile + 1) * LANES] = src_rows.T.astype(BF16)

    for t in range(COL_ZA // LANES):
        put(t, wt_ref[t * LANES:(t + 1) * LANES, :])
    for t in range((COL_MISC - COL_ZA) // LANES):
        put(COL_ZA // LANES + t, wt_ref[ORIG_ZA + t * LANES:ORIG_ZA + (t + 1) * LANES, :])
    pad = jnp.zeros((LANES - QK_ROPE - SSD_HEADS, wt_ref.shape[1]), F32)
    put(COL_MISC // LANES, jnp.concatenate([wt_ref[ORIG_KR:ORIG_ZA, :], wt_ref[ORIG_DT:ORIG_DT + SSD_HEADS, :], pad],
                                           axis=0))


def _prep_w_in(w_in):
    k, n = w_in.shape
    assert n == ORIG_DT + SSD_HEADS and k % PREP_COLS == 0
    return pl.pallas_call(
        _prep_w_in_kernel,
        grid=(k // PREP_COLS,),
        in_specs=[pl.BlockSpec((n, PREP_COLS), lambda i: (0, i))],
        out_specs=pl.BlockSpec((PREP_COLS, D_IN_PAD), lambda i: (i, 0)),
        out_shape=jax.ShapeDtypeStruct((k, D_IN_PAD), BF16),
        compiler_params=pltpu.CompilerParams(dimension_semantics=("arbitrary",),
                                             vmem_limit_bytes=VMEM_LIMIT_BYTES),
        name="prep_w_in",
    )(w_in.T)


def _rope_tables(pos):
    inv_freq = ROPE_THETA ** (-jnp.arange(ROPE_HALF, dtype=F32) / ROPE_HALF)
    ang = pos.astype(F32)[:, None] * inv_freq[None, :]
    cos, sin = jnp.cos(ang), jnp.sin(ang)
    reps = LANES // QK_ROPE
    return jnp.tile(cos, (1, 2 * reps)), jnp.tile(jnp.concatenate([-sin, sin], axis=1), (1, reps))


def _prep_weights(lw):
    row = lambda v: v.reshape(1, -1).astype(F32)
    w_qb = lw['w_q_b'].reshape(Q_RANK, MLA_HEADS, QK_NOPE + QK_ROPE)
    lane_pad = lambda v: jnp.pad(v.reshape(1, -1).astype(F32), ((0, 0), (DT_LANE0, LANES - DT_LANE0 - SSD_HEADS)))
    head_of_col = jnp.arange(D_SSM) // SSD_HEADDIM
    return {
        'norm_pre': row(lw['norm_pre']),
        'w_in': _prep_w_in(lw['w_in']),
        'q_a_norm': row(lw['q_a_norm']),
        'w_qb': jnp.concatenate([w_qb[:, :, :QK_NOPE].reshape(Q_RANK, -1),
                                 w_qb[:, :, QK_NOPE:].reshape(Q_RANK, -1)], axis=1).astype(BF16),
        'kv_a_norm': row(lw['kv_a_norm']),
        'w_uk2t': lw['w_uk'].reshape(KV_RANK, MLA_HEADS * QK_NOPE).T.astype(BF16),
        'w_uv2': lw['w_uv'].reshape(KV_RANK, MLA_HEADS * V_HEAD).astype(BF16),
        'w_ukt': jnp.transpose(lw['w_uk'], (1, 2, 0)).astype(BF16),
        'w_uvh': jnp.transpose(lw['w_uv'], (1, 0, 2)).astype(BF16),
        'conv_w': lw['conv_w'].astype(F32),
        'conv_b': row(lw['conv_b']),
        'dt_bias_t': lane_pad(lw['dt_bias']),
        'a_log_t': lane_pad(lw['a_log']),
        'expand': sum((jnp.arange(LANES)[:, None] == DT_LANE0 + k * SSD_HEADS + head_of_col[None, :])
                      for k in range(SPLIT_PIECES)).astype(BF16),
        'd_skip_w': jnp.repeat(lw['d_skip'].astype(F32), SSD_HEADDIM).reshape(1, D_SSM),
        'ssm_norm': row(lw['ssm_norm']),
        'w_out': lw['w_out'].astype(BF16),
        'norm_post': row(lw['norm_post']),
    }


def _prompt_layer(x, w):
    b, s, _ = x.shape
    x2 = x.reshape(b * s, D_MODEL)
    cos, sin = _rope_tables(jnp.arange(s, dtype=jnp.int32))
    q, kt, v, ckv, kr, za, mix_s, state, tail = _proj_prompt(x2, cos, sin, w)
    r3 = lambda a: a.reshape(b, s, a.shape[-1])
    mix_a = _attn_prompt(r3(q), kt, r3(v), r3(za))
    y = _merge(mix_a.reshape(b * s, D_ATTN), mix_s, x2, w)
    h = state.reshape(b, SSD_GROUPS, D_STATE, HEADS_PER_GROUP, SSD_HEADDIM)
    h = jnp.transpose(h, (0, 1, 3, 4, 2)).reshape(b, SSD_HEADS, SSD_HEADDIM, D_STATE)
    return (y.reshape(b, s, D_MODEL), r3(ckv), jnp.swapaxes(kr, 1, 2), tail[:, CONV_PAD - (CONV_W - 1):, :], h)


def _sample_layer(layer, x, cache_c, cache_r, conv_prev, h0, page_table, w):
    b, s, _ = x.shape
    n = b * s
    past = page_table.shape[1] * PAGE_SIZE
    x2 = x.reshape(n, D_MODEL)
    pos = past + jnp.arange(s, dtype=jnp.int32)
    cos, sin = _rope_tables(jnp.tile(pos, b))
    qlat, qrope, ckv, kr, za, zs, xbc, misc = _proj_sample(x2, cos, sin, w)
    olat = _attn_sample(layer, page_table, qlat.reshape(n, MLA_HEADS, KV_RANK), qrope.reshape(n, MLA_HEADS, QK_ROPE),
                        ckv.reshape(n, 1, KV_RANK), kr.reshape(n, 1, QK_ROPE), cache_c,
                        jnp.swapaxes(cache_r, 2, 3))
    mix_a, conv_new, act, xdt_t, dec_t = _post_sample(
        olat.reshape(n, MLA_HEADS * KV_RANK), za, xbc, misc, jnp.swapaxes(conv_prev, 0, 1), w)
    h, y_t = _state_sample(h0.astype(F32).reshape(n, D_SSM, D_STATE), xdt_t, dec_t, act)
    y = _finish_sample(y_t, act, zs, mix_a, x2, w)
    return (y.reshape(b, s, D_MODEL), ckv.reshape(b, s, KV_RANK), kr.reshape(b, s, QK_ROPE),
            jnp.swapaxes(conv_new, 0, 1), h.reshape(b, SSD_HEADS, SSD_HEADDIM, D_STATE))


def kernel(x_prompt, x_sample, cache_ckv, cache_krope, state_conv, state_ssm, page_table, norm_pre, w_in,
           q_a_norm, w_q_b, kv_a_norm, w_uk, w_uv, conv_w, conv_b, dt_bias, a_log, d_skip, ssm_norm, w_out,
           norm_post):
    assert x_sample.shape[1] == 1, "the sample path handles one new token per sequence"
    depth = w_in.shape[0]
    y_prompt, y_sample = x_prompt, x_sample
    outs = [[] for _ in range(8)]
    for l in range(depth):
        w = _prep_weights({'norm_pre': norm_pre[l], 'w_in': w_in[l], 'q_a_norm': q_a_norm[l],
                           'w_q_b': w_q_b[l], 'kv_a_norm': kv_a_norm[l], 'w_uk': w_uk[l], 'w_uv': w_uv[l],
                           'conv_w': conv_w[l], 'conv_b': conv_b[l], 'dt_bias': dt_bias[l], 'a_log': a_log[l],
                           'd_skip': d_skip[l], 'ssm_norm': ssm_norm[l], 'w_out': w_out[l],
                           'norm_post': norm_post[l]})
        y_prompt, c1, k1, v1, h1 = _prompt_layer(y_prompt, w)
        y_sample, c2, k2, v2, h2 = _sample_layer(l, y_sample, cache_ckv, cache_krope, state_conv[l],
                                                 state_ssm[l], page_table, w)
        for lst, val in zip(outs, (c1, k1, v1, h1, c2, k2, v2, h2)):
            lst.append(val)
    return (y_prompt, y_sample) + tuple(jnp.stack(o) for o in outs)
```

```python
import functools
import math

import jax
import jax.numpy as jnp
from jax import lax
from jax.experimental import pallas as pl
from jax.experimental.pallas import tpu as pltpu

F32 = jnp.float32
BF16 = jnp.bfloat16

D_MODEL = 1024
PAGE_SIZE = 128
D_MIX = 2 * D_MODEL
D_ATTN = D_MIX // 2
D_SSM = D_MIX - D_ATTN
MLA_HEADS = 8
QK_NOPE = 128
QK_ROPE = 64
ROPE_HALF = QK_ROPE // 2
V_HEAD = D_ATTN // MLA_HEADS
Q_RANK = 384
KV_RANK = 256
ROPE_THETA = 10000.0
SOFTMAX_SCALE = (QK_NOPE + QK_ROPE) ** -0.5
SSD_HEADDIM = 64
SSD_HEADS = D_SSM // SSD_HEADDIM
SSD_GROUPS = 2
HEADS_PER_GROUP = SSD_HEADS // SSD_GROUPS
GROUP_WIDTH = D_SSM // SSD_GROUPS
D_STATE = 128
CONV_W = 4
CONV_DIM = D_SSM + 2 * SSD_GROUPS * D_STATE
CHUNK = 128
EPS = 1e-6
SPLIT_PIECES = 3
NEG_BIG = -1e30

LANES = 128
SUBLANES = 8
VMEM_LIMIT_BYTES = 56 * 1024 * 1024

COL_Q = 0
COL_C = COL_Q + Q_RANK
COL_ZA = COL_C + KV_RANK
COL_ZS = COL_ZA + D_ATTN
COL_XBC = COL_ZS + D_SSM
COL_MISC = COL_XBC + CONV_DIM
D_IN_PAD = COL_MISC + LANES
DT_LANE0 = QK_ROPE
QK_PAD = 2 * LANES

PROJ_ROWS = 512
ATTN_TQ = 256
ATTN_HEADS = 4
SSD_STEP_CHUNKS = 4
MERGE_ROWS = 1024
CONV_PAD = SUBLANES
DEC_KV_CHUNK = 2048
DEC_ATTN_SEQS = 2

_NT = (((1,), (1,)), ((), ()))


def _rms(x, w):
    return x * lax.rsqrt(jnp.mean(x * x, axis=-1, keepdims=True) + EPS) * w


def _silu(x):
    return x / (1.0 + jnp.exp(-x))


def _softplus(x):
    return jnp.maximum(x, 0.0) + jnp.log(1.0 + jnp.exp(-jnp.abs(x)))


def _dot(a, b):
    return jnp.dot(a, b, preferred_element_type=F32)


def _rope_tile(x, cos, sin_signed):
    lane = lax.broadcasted_iota(jnp.int32, x.shape, 1)
    first_half = (lane % QK_ROPE) < ROPE_HALF
    partner = jnp.where(first_half,
                        pltpu.roll(x, LANES - ROPE_HALF, 1),
                        pltpu.roll(x, ROPE_HALF, 1))
    return x * cos + partner * sin_signed


def _front(x_ref, npre_ref, win_ref):
    h = _rms(x_ref[...], npre_ref[...]).astype(BF16)

    def seg(lo, hi):
        return _dot(h, win_ref[:, lo:hi])

    return seg


def _proj_prompt_kernel(seq_tiles, x_ref, cos_ref, sin_ref, npre_ref, win_ref, qan_ref, wqb_ref, kvn_ref,
                        wukt_ref, wuv_ref, cw_ref, cb_ref, dtb_ref, alog_ref, expand_ref, dskip_ref, normw_ref,
                        q_ref, kt_ref, v_ref, ckv_ref, kr_ref, za_ref, mixs_ref, state_ref, tail_ref, xp_ref):
    rows = x_ref.shape[0]
    first = pl.program_id(0) % seq_tiles == 0

    @pl.when(first)
    def _():
        xp_ref[0:CONV_PAD, :] = jnp.zeros((CONV_PAD, CONV_DIM), F32)
        state_ref[...] = jnp.zeros(state_ref.shape, F32)

    seg = _front(x_ref, npre_ref, win_ref)
    xp_ref[CONV_PAD:CONV_PAD + rows, :] = seg(COL_XBC, COL_MISC)
    xp = xp_ref[...]
    acc = cb_ref[...]
    for tap in range(CONV_W):
        back = CONV_W - 1 - tap
        shifted = xp if back == 0 else pltpu.roll(xp, back, 0)
        acc = acc + shifted[CONV_PAD:CONV_PAD + rows, :] * cw_ref[tap:tap + 1, :]
    act_all = _silu(acc)
    tail_ref[...] = xp_ref[rows:rows + CONV_PAD, :]
    xp_ref[CONV_PAD - 3:CONV_PAD, :] = xp_ref[CONV_PAD + rows - 3:CONV_PAD + rows, :]

    misc = seg(COL_MISC, D_IN_PAD)
    cos = cos_ref[...]
    sin = sin_ref[...]
    lane = lax.broadcasted_iota(jnp.int32, misc.shape, 1)
    low = lane < QK_ROPE

    kr_full = _rope_tile(misc, cos, sin)
    kr_t = jnp.where(low, kr_full, 0.0).T
    kr_ref[...] = kr_t[:QK_ROPE, :]
    kr_lo_t = kr_t.astype(BF16)
    kr_hi_t = jnp.where(low, 0.0, pltpu.roll(kr_full, QK_ROPE, 1)).T.astype(BF16)

    c_raw = seg(COL_C, COL_ZA)
    q_a = seg(COL_Q, COL_C)
    za_ref[...] = seg(COL_ZA, COL_ZS)
    ckv = _rms(c_raw, kvn_ref[...])
    ckv_ref[...] = ckv
    cb = ckv.astype(BF16)
    knope_t = lax.dot_general(wukt_ref[...], cb, _NT, preferred_element_type=F32)
    v_ref[...] = _dot(cb, wuv_ref[...]).astype(BF16)

    qn = _rms(q_a, qan_ref[...]).astype(BF16)
    q = _dot(qn, wqb_ref[...])
    nope_w = MLA_HEADS * QK_NOPE
    for hh in range(MLA_HEADS):
        pair = hh // 2
        r = _rope_tile(q[:, nope_w + pair * LANES: nope_w + (pair + 1) * LANES], cos, sin)
        own = low if hh % 2 == 0 else jnp.logical_not(low)
        base = hh * QK_PAD
        q_ref[:, base:base + LANES] = (q[:, hh * QK_NOPE:(hh + 1) * QK_NOPE] * SOFTMAX_SCALE).astype(BF16)
        q_ref[:, base + LANES:base + QK_PAD] = (jnp.where(own, r, 0.0) * SOFTMAX_SCALE).astype(BF16)
        kt_ref[base:base + LANES, :] = knope_t[hh * QK_NOPE:(hh + 1) * QK_NOPE, :].astype(BF16)
        kt_ref[base + LANES:base + QK_PAD, :] = kr_lo_t if hh % 2 == 0 else kr_hi_t

    zs = seg(COL_ZS, COL_XBC)
    for k in range(rows // CHUNK):
        r0 = k * CHUNK
        act = act_all[r0:r0 + CHUNK, :]
        y = _ssd_chunk(act, misc[r0:r0 + CHUNK, :], dtb_ref, alog_ref, expand_ref, state_ref)
        mixs_ref[r0:r0 + CHUNK, :] = _ssd_finish(y, act[:, :D_SSM], zs[r0:r0 + CHUNK, :], dskip_ref[...],
                                                 normw_ref[...])


def _proj_sample_kernel(x_ref, cos_ref, sin_ref, npre_ref, win_ref, qan_ref, wqb_ref, kvn_ref, wukt_ref,
                        qlat_ref, qrope_ref, ckv_ref, kr_ref, za_ref, zs_ref, xbc_ref, misc_ref):
    seg = _front(x_ref, npre_ref, win_ref)
    za_ref[...] = seg(COL_ZA, COL_ZS)
    zs_ref[...] = seg(COL_ZS, COL_XBC)
    xbc_ref[...] = seg(COL_XBC, COL_MISC)
    misc = seg(COL_MISC, D_IN_PAD)
    misc_ref[...] = misc
    cos = cos_ref[...]
    sin = sin_ref[...]
    kr_ref[...] = _rope_tile(misc, cos, sin)[:, :QK_ROPE]
    ckv_ref[...] = _rms(seg(COL_C, COL_ZA), kvn_ref[...])

    qn = _rms(seg(COL_Q, COL_C), qan_ref[...]).astype(BF16)
    q = _dot(qn, wqb_ref[...])
    nope_w = MLA_HEADS * QK_NOPE
    for pair in range(MLA_HEADS // 2):
        lo = nope_w + pair * LANES
        qrope_ref[:, pair * LANES:(pair + 1) * LANES] = _rope_tile(q[:, lo:lo + LANES], cos, sin) * SOFTMAX_SCALE
    for hh in range(MLA_HEADS):
        qh = q[:, hh * QK_NOPE:(hh + 1) * QK_NOPE].astype(BF16)
        qlat_ref[:, hh * KV_RANK:(hh + 1) * KV_RANK] = _dot(qh, wukt_ref[hh]) * SOFTMAX_SCALE


def _full(shape):
    return pl.BlockSpec(shape, lambda *_: (0,) * len(shape))


def _proj_prompt(x2, cos, sin, w):
    n = x2.shape[0]
    tm = PROJ_ROWS
    seq_tiles = cos.shape[0] // tm
    assert tm == SSD_STEP_CHUNKS * CHUNK
    batch = n // cos.shape[0]
    rows = lambda width: pl.BlockSpec((tm, width), lambda i: (i, 0))
    tab = pl.BlockSpec((tm, LANES), lambda i: (i % seq_tiles, 0))
    out_shape = (
        jax.ShapeDtypeStruct((n, MLA_HEADS * QK_PAD), BF16),
        jax.ShapeDtypeStruct((MLA_HEADS * QK_PAD, n), BF16),
        jax.ShapeDtypeStruct((n, D_ATTN), BF16),
        jax.ShapeDtypeStruct((n, KV_RANK), F32),
        jax.ShapeDtypeStruct((batch, QK_ROPE, cos.shape[0]), F32),
        jax.ShapeDtypeStruct((n, D_ATTN), F32),
        jax.ShapeDtypeStruct((n, D_SSM), BF16),
        jax.ShapeDtypeStruct((batch, SSD_GROUPS, D_STATE, GROUP_WIDTH), F32),
        jax.ShapeDtypeStruct((batch, CONV_PAD, CONV_DIM), F32),
    )
    per_seq = lambda shape: pl.BlockSpec((None,) + shape[1:], lambda i: (i // seq_tiles,) + (0,) * (len(shape) - 1))
    out_specs = [rows(s.shape[1]) for s in out_shape[:7]]
    out_specs[1] = pl.BlockSpec((MLA_HEADS * QK_PAD, tm), lambda i: (0, i))
    out_specs[4] = pl.BlockSpec((None, QK_ROPE, tm), lambda i: (i // seq_tiles, 0, i % seq_tiles))
    out_specs += [per_seq(out_shape[7].shape), per_seq(out_shape[8].shape)]
    return pl.pallas_call(
        functools.partial(_proj_prompt_kernel, seq_tiles),
        grid=(n // tm,),
        in_specs=[rows(D_MODEL), tab, tab, _full((1, D_MODEL)), _full((D_MODEL, D_IN_PAD)),
                  _full((1, Q_RANK)), _full(w['w_qb'].shape), _full((1, KV_RANK)),
                  _full(w['w_uk2t'].shape), _full(w['w_uv2'].shape),
                  _full((CONV_W, CONV_DIM)), _full((1, CONV_DIM)), _full((1, LANES)), _full((1, LANES)),
                  _full((LANES, D_SSM)), _full((1, D_SSM)), _full((1, D_SSM))],
        out_specs=tuple(out_specs),
        out_shape=out_shape,
        scratch_shapes=[pltpu.VMEM((CONV_PAD + tm, CONV_DIM), F32)],
        compiler_params=pltpu.CompilerParams(dimension_semantics=("arbitrary",),
                                             vmem_limit_bytes=VMEM_LIMIT_BYTES),
        name="proj_prompt",
    )(x2, cos, sin, w['norm_pre'], w['w_in'], w['q_a_norm'], w['w_qb'], w['kv_a_norm'],
      w['w_uk2t'], w['w_uv2'], w['conv_w'], w['conv_b'], w['dt_bias_t'], w['a_log_t'], w['expand'],
      w['d_skip_w'], w['ssm_norm'])


def _proj_sample(x2, cos, sin, w):
    n = x2.shape[0]
    rows = lambda width: pl.BlockSpec((n, width), lambda i: (0, 0))
    out_shape = (
        jax.ShapeDtypeStruct((n, MLA_HEADS * KV_RANK), F32),
        jax.ShapeDtypeStruct((n, MLA_HEADS * QK_ROPE), F32),
        jax.ShapeDtypeStruct((n, KV_RANK), F32),
        jax.ShapeDtypeStruct((n, QK_ROPE), F32),
        jax.ShapeDtypeStruct((n, D_ATTN), F32),
        jax.ShapeDtypeStruct((n, D_SSM), F32),
        jax.ShapeDtypeStruct((n, CONV_DIM), F32),
        jax.ShapeDtypeStruct((n, LANES), F32),
    )
    return pl.pallas_call(
        _proj_sample_kernel,
        grid=(1,),
        in_specs=[rows(D_MODEL), rows(LANES), rows(LANES), _full((1, D_MODEL)), _full((D_MODEL, D_IN_PAD)),
                  _full((1, Q_RANK)), _full(w['w_qb'].shape), _full((1, KV_RANK)),
                  _full(w['w_ukt'].shape)],
        out_specs=tuple(rows(s.shape[1]) for s in out_shape),
        out_shape=out_shape,
        compiler_params=pltpu.CompilerParams(dimension_semantics=("arbitrary",),
                                             vmem_limit_bytes=VMEM_LIMIT_BYTES),
        name="proj_sample",
    )(x2, cos, sin, w['norm_pre'], w['w_in'], w['q_a_norm'], w['w_qb'], w['kv_a_norm'], w['w_ukt'])


def _attn_prompt_kernel(q_ref, kt_ref, v_ref, z_ref, o_ref):
    seq = q_ref.shape[0]
    tq = ATTN_TQ
    row = lax.broadcasted_iota(jnp.int32, (tq, tq), 0)
    col = lax.broadcasted_iota(jnp.int32, (tq, tq), 1)
    causal = col <= row
    def score_phase(qi):
        lim = (qi + 1) * tq
        return [_dot(q_ref[qi * tq:lim, hd * QK_PAD:(hd + 1) * QK_PAD], kt_ref[hd * QK_PAD:(hd + 1) * QK_PAD, 0:lim])
                for hd in range(ATTN_HEADS)]

    def softmax_phase(qi, scores):
        lim = (qi + 1) * tq
        probs = []
        for s in scores:
            diag = jnp.where(causal, s[:, lim - tq:], NEG_BIG)
            s = diag if qi == 0 else jnp.concatenate([s[:, :lim - tq], diag], axis=1)
            m = jnp.max(s, axis=-1, keepdims=True)
            probs.append(jnp.exp(s - m).astype(BF16))
        return probs

    def value_phase(qi, probs):
        lim = (qi + 1) * tq
        ones_col = (lax.broadcasted_iota(jnp.int32, (lim, LANES), 1) == 0).astype(BF16)
        for hd, p in enumerate(probs):
            hv = slice(hd * V_HEAD, (hd + 1) * V_HEAD)
            ol = _dot(p, jnp.concatenate([v_ref[0:lim, hv], ones_col], axis=1))
            o = ol[:, :V_HEAD] / jnp.sum(ol[:, V_HEAD:], axis=-1, keepdims=True)
            o_ref[qi * tq:lim, hv] = (o * _silu(z_ref[qi * tq:lim, hv])).astype(BF16)

    for qi in reversed(range(seq // tq)):
        value_phase(qi, softmax_phase(qi, score_phase(qi)))


def _attn_prompt(q3, kt, v3, z3):
    b, s, _ = q3.shape
    nh = ATTN_HEADS
    hv = pl.BlockSpec((None, s, nh * V_HEAD), lambda i, j: (i, 0, j))
    return pl.pallas_call(
        _attn_prompt_kernel,
        grid=(b, MLA_HEADS // nh),
        in_specs=[pl.BlockSpec((None, s, nh * QK_PAD), lambda i, j: (i, 0, j)),
                  pl.BlockSpec((nh * QK_PAD, s), lambda i, j: (j, i)), hv, hv],
        out_specs=hv,
        out_shape=jax.ShapeDtypeStruct((b, s, D_ATTN), BF16),
        compiler_params=pltpu.CompilerParams(dimension_semantics=("arbitrary", "arbitrary"),
                                             vmem_limit_bytes=VMEM_LIMIT_BYTES),
        name="attn_prompt",
    )(q3, kt, v3, z3)


def _dt_lanes(shape):
    lane = lax.broadcasted_iota(jnp.int32, shape, 1)
    return jnp.logical_and(lane >= DT_LANE0, lane < DT_LANE0 + SSD_HEADS)


def _dt_and_da(misc, dtb, alog):
    dt = jnp.where(_dt_lanes(misc.shape), _softplus(misc + dtb), 0.0)
    return dt, dt * (-jnp.exp(alog))


def _split_bf16(x):
    pieces = []
    for _ in range(SPLIT_PIECES):
        piece = x.astype(BF16).astype(F32)
        pieces.append(piece)
        x = x - piece
    return pieces


def _cumsum_rows(lower_b, x):
    return sum(_dot(lower_b, piece.astype(BF16)) for piece in _split_bf16(x))


def _head_expand(x, expand_ref):
    pieces = _split_bf16(jnp.where(_dt_lanes(x.shape), x, 0.0))
    packed = pieces[0]
    for k in range(1, SPLIT_PIECES):
        packed = packed + pltpu.roll(pieces[k], k * SSD_HEADS, 1)
    return _dot(packed.astype(BF16), expand_ref[...])


def _ssd_finish(y, xs, z, dskip, normw):
    y = y + dskip * xs
    gated = y * _silu(z)
    outs = []
    for g in range(SSD_GROUPS):
        sl = slice(g * GROUP_WIDTH, (g + 1) * GROUP_WIDTH)
        outs.append(_rms(gated[:, sl], normw[:, sl]))
    return jnp.concatenate(outs, axis=1).astype(BF16)


def _out_proj_attn(mix_a, wout_ref):
    return _dot(mix_a, wout_ref[0:D_ATTN, :])


def _out_proj(o_attn, mix_s, wout_ref, npost, x):
    o = o_attn + _dot(mix_s, wout_ref[D_ATTN:D_MIX, :])
    return x + _rms(o, npost)


def _ssd_chunk(act, misc, dtb_ref, alog_ref, expand_ref, state_ref):
    xs = act[:, :D_SSM]
    dt, da = _dt_and_da(misc, dtb_ref[...], alog_ref[...])
    row = lax.broadcasted_iota(jnp.int32, (CHUNK, CHUNK), 0)
    col = lax.broadcasted_iota(jnp.int32, (CHUNK, CHUNK), 1)
    lower = row >= col
    a_cum = _cumsum_rows(lower.astype(BF16), da)
    a_cum_t = a_cum.T
    a_last = a_cum[CHUNK - 1:CHUNK, :]
    xdt = xs * _head_expand(dt, expand_ref)
    decay_out = _head_expand(jnp.exp(a_cum), expand_ref)
    state_decay = decay_out[CHUNK - 1:CHUNK, :]
    xdt_end = (xdt * _head_expand(jnp.exp(a_last - a_cum), expand_ref)).astype(BF16)
    xdt_b = xdt.astype(BF16)
    lane_w = lax.broadcasted_iota(jnp.int32, (CHUNK, LANES), 1)
    first_head = lane_w < SSD_HEADDIM

    ys = []
    for g in range(SSD_GROUPS):
        bm = act[:, D_SSM + g * D_STATE:D_SSM + (g + 1) * D_STATE]
        cm = act[:, D_SSM + (SSD_GROUPS + g) * D_STATE:D_SSM + (SSD_GROUPS + g + 1) * D_STATE]
        bm_b = bm.astype(BF16)
        cm_b = cm.astype(BF16)
        cb = lax.dot_general(cm_b, bm_b, _NT, preferred_element_type=F32)
        gsl = slice(g * GROUP_WIDTH, (g + 1) * GROUP_WIDTH)
        state = state_ref[g]
        y_off = _dot(cm_b, state.astype(BF16)) * decay_out[:, gsl]
        y_diag = []
        for pair in range(HEADS_PER_GROUP // 2):
            halves = []
            x_pair = xdt_b[:, g * GROUP_WIDTH + pair * LANES: g * GROUP_WIDTH + (pair + 1) * LANES]
            for k in range(2):
                lane_h = DT_LANE0 + g * HEADS_PER_GROUP + 2 * pair + k
                seg = a_cum[:, lane_h:lane_h + 1] - a_cum_t[lane_h:lane_h + 1, :]
                decay = jnp.exp(jnp.where(lower, seg, NEG_BIG))
                halves.append(_dot((cb * decay).astype(BF16), x_pair))
            y_diag.append(jnp.where(first_head, halves[0], halves[1]))
        ys.append(jnp.concatenate(y_diag, axis=1) + y_off)
        state_ref[g] = state * state_decay[:, gsl] + _dot(bm.T.astype(BF16), xdt_end[:, gsl])
    return jnp.concatenate(ys, axis=1)


def _merge_kernel(ma_ref, ms_ref, x_ref, wout_ref, npost_ref, y_ref):
    y_ref[...] = _out_proj(_out_proj_attn(ma_ref[...], wout_ref), ms_ref[...], wout_ref, npost_ref[...], x_ref[...])


def _merge(mix_a, mix_s, x2, w):
    n = x2.shape[0]
    tm = MERGE_ROWS
    rows = lambda width: pl.BlockSpec((tm, width), lambda i: (i, 0))
    return pl.pallas_call(
        _merge_kernel,
        grid=(n // tm,),
        in_specs=[rows(D_ATTN), rows(D_SSM), rows(D_MODEL), _full((D_MIX, D_MODEL)), _full((1, D_MODEL))],
        out_specs=rows(D_MODEL),
        out_shape=jax.ShapeDtypeStruct((n, D_MODEL), F32),
        compiler_params=pltpu.CompilerParams(dimension_semantics=("arbitrary",),
                                             vmem_limit_bytes=VMEM_LIMIT_BYTES),
        name="merge",
    )(mix_a, mix_s, x2, w['w_out'], w['norm_post'])


def _attn_sample_kernel(layer, pt_ref, qlat_ref, qrope_ref, cnew_ref, rnew_ref, h0_ref, xdt_t_ref, dec_t_ref,
                        act_ref, cache_c_ref, cache_rt_ref, o_ref, h_ref, y_t_ref, cbuf, rbuf, sem):
    i = pl.program_id(0)
    n = pl.num_programs(0)
    n_pages = pt_ref.shape[1]
    seqs = qlat_ref.shape[0]
    past = n_pages * PAGE_SIZE
    slot = i % 2

    def page_copies(step, slot_):
        copies = []
        for j in range(seqs):
            buf = slot_ * seqs + j
            for p in range(n_pages):
                page = pt_ref[step * seqs + j, p]
                dst = pl.ds(p * PAGE_SIZE, PAGE_SIZE)
                copies.append(pltpu.make_async_copy(cache_c_ref.at[layer, page], cbuf.at[buf, dst],
                                                    sem.at[0, slot_]))
                copies.append(pltpu.make_async_copy(cache_rt_ref.at[layer, page], rbuf.at[buf, :, dst],
                                                    sem.at[1, slot_]))
        return copies

    @pl.when(i == 0)
    def _():
        for cp in page_copies(0, 0):
            cp.start()

    @pl.when(i + 1 < n)
    def _():
        for cp in page_copies(i + 1, 1 - slot):
            cp.start()

    for cp in page_copies(i, slot):
        cp.wait()

    q_pos = past
    chunks = [pl.ds(c * DEC_KV_CHUNK, DEC_KV_CHUNK) for c in range(past // DEC_KV_CHUNK)]
    scores = []
    for j in range(seqs):
        buf = slot * seqs + j
        qlat = qlat_ref[j]
        qrope = qrope_ref[j]
        scores.append(jnp.concatenate(
            [lax.dot_general(qlat, cbuf[buf, keys, :], _NT, preferred_element_type=F32)
             + _dot(qrope, rbuf[buf, :, keys]) for keys in chunks], axis=1))
    probs = []
    for j in range(seqs):
        cnew = cnew_ref[j]
        s_new = (jnp.sum(qlat_ref[j] * cnew, axis=-1, keepdims=True)
                 + jnp.sum(qrope_ref[j] * rnew_ref[j], axis=-1, keepdims=True))
        k_pos = lax.broadcasted_iota(jnp.int32, scores[j].shape, 1)
        s = jnp.where(k_pos <= q_pos, scores[j], NEG_BIG)
        m = jnp.maximum(jnp.max(s, axis=-1, keepdims=True), s_new)
        p = jnp.exp(s - m)
        p_new = jnp.exp(s_new - m)
        probs.append((p, p_new, jnp.sum(p, axis=-1, keepdims=True) + p_new))
    for j in range(seqs):
        buf = slot * seqs + j
        p, p_new, l = probs[j]
        acc = p_new * cnew_ref[j]
        for c, keys in enumerate(chunks):
            acc = acc + _dot(p[:, c * DEC_KV_CHUNK:(c + 1) * DEC_KV_CHUNK], cbuf[buf, keys, :])
        o_ref[j] = acc / l

    @pl.when(i == 0)
    def _():
        y_t_ref[...] = jnp.zeros(y_t_ref.shape, F32)

    rows, n_seq = xdt_t_ref.shape
    lane = lax.broadcasted_iota(jnp.int32, (rows, n_seq), 1)
    xdt_t = xdt_t_ref[...]
    dec_t = dec_t_ref[...]
    for j in range(seqs):
        seq = i * seqs + j
        own = lane == seq
        x_col = jnp.sum(jnp.where(own, xdt_t, 0.0), axis=1, keepdims=True)
        d_col = jnp.sum(jnp.where(own, dec_t, 0.0), axis=1, keepdims=True)
        bc = act_ref[pl.ds(seq, 1), :]
        b_rows = jnp.concatenate(
            [jnp.broadcast_to(bc[:, D_SSM + g * D_STATE:D_SSM + (g + 1) * D_STATE], (GROUP_WIDTH, D_STATE))
             for g in range(SSD_GROUPS)], axis=0)
        c_rows = jnp.concatenate(
            [jnp.broadcast_to(bc[:, D_SSM + (SSD_GROUPS + g) * D_STATE:D_SSM + (SSD_GROUPS + g + 1) * D_STATE],
                              (GROUP_WIDTH, D_STATE)) for g in range(SSD_GROUPS)], axis=0)
        h = d_col * h0_ref[j] + x_col * b_rows
        h_ref[j] = h
        y_col = jnp.sum(h * c_rows, axis=1, keepdims=True)
        y_t_ref[...] = jnp.where(own, y_col, y_t_ref[...])


def _attn_sample(layer, page_table, qlat3, qrope3, cnew3, rnew3, h0, xdt_t, dec_t, act, cache_c, cache_rt):
    b, n_pages = page_table.shape
    past = n_pages * PAGE_SIZE
    seqs = DEC_ATTN_SEQS
    assert past % DEC_KV_CHUNK == 0 and b % seqs == 0
    per_step = lambda d1, d2: pl.BlockSpec((seqs, d1, d2), lambda i, pt: (i, 0, 0))
    whole = lambda a: pl.BlockSpec(a.shape, lambda i, pt: (0,) * a.ndim)
    grid_spec = pltpu.PrefetchScalarGridSpec(
        num_scalar_prefetch=1,
        grid=(b // seqs,),
        in_specs=[per_step(MLA_HEADS, KV_RANK), per_step(MLA_HEADS, QK_ROPE), per_step(1, KV_RANK),
                  per_step(1, QK_ROPE), per_step(D_SSM, D_STATE), whole(xdt_t), whole(dec_t), whole(act),
                  pl.BlockSpec(memory_space=pl.ANY), pl.BlockSpec(memory_space=pl.ANY)],
        out_specs=(per_step(MLA_HEADS, KV_RANK), per_step(D_SSM, D_STATE), whole(xdt_t)),
        scratch_shapes=[pltpu.VMEM((2 * seqs, past, KV_RANK), F32), pltpu.VMEM((2 * seqs, QK_ROPE, past), F32),
                        pltpu.SemaphoreType.DMA((2, 2))],
    )
    return pl.pallas_call(
        functools.partial(_attn_sample_kernel, layer),
        grid_spec=grid_spec,
        out_shape=(jax.ShapeDtypeStruct((b, MLA_HEADS, KV_RANK), F32), jax.ShapeDtypeStruct(h0.shape, F32),
                   jax.ShapeDtypeStruct(xdt_t.shape, F32)),
        compiler_params=pltpu.CompilerParams(dimension_semantics=("arbitrary",),
                                             vmem_limit_bytes=VMEM_LIMIT_BYTES),
        name="attn_sample",
    )(page_table, qlat3, qrope3, cnew3, rnew3, h0, xdt_t, dec_t, act, cache_c, cache_rt)


def _post_sample_kernel(xbc_ref, misc_ref, cprev_ref, cw_ref, cb_ref, dtb_ref, alog_ref, expand_ref,
                        cnew_ref, act_ref, xdt_t_ref, dec_t_ref):
    xb = xbc_ref[...]
    acc = cb_ref[...]
    for k in range(CONV_W - 1):
        acc = acc + cprev_ref[k] * cw_ref[k:k + 1, :]
    acc = acc + xb * cw_ref[CONV_W - 1:CONV_W, :]
    act = _silu(acc)
    act_ref[...] = act
    for k in range(CONV_W - 2):
        cnew_ref[k] = cprev_ref[k + 1]
    cnew_ref[CONV_W - 2] = xb

    dt, da = _dt_and_da(misc_ref[...], dtb_ref[...], alog_ref[...])
    xdt = act[:, :D_SSM] * _head_expand(dt, expand_ref)
    decay = _head_expand(jnp.exp(da), expand_ref)
    xdt_t_ref[...] = xdt.T
    dec_t_ref[...] = decay.T


def _post_sample(xbc, misc, cprev3, w):
    n = xbc.shape[0]
    full = lambda a: _full(a.shape)
    args = (xbc, misc, cprev3, w['conv_w'], w['conv_b'], w['dt_bias_t'], w['a_log_t'], w['expand'])
    out_shape = (
        jax.ShapeDtypeStruct((CONV_W - 1, n, CONV_DIM), F32),
        jax.ShapeDtypeStruct((n, CONV_DIM), F32),
        jax.ShapeDtypeStruct((D_SSM, n), F32),
        jax.ShapeDtypeStruct((D_SSM, n), F32),
    )
    return pl.pallas_call(
        _post_sample_kernel,
        grid=(1,),
        in_specs=[full(a) for a in args],
        out_specs=tuple(_full(s.shape) for s in out_shape),
        out_shape=out_shape,
        compiler_params=pltpu.CompilerParams(dimension_semantics=("arbitrary",),
                                             vmem_limit_bytes=VMEM_LIMIT_BYTES),
        name="post_sample",
    )(*args)


def _finish_sample_kernel(y_t_ref, act_ref, z_ref, dskip_ref, normw_ref, olat_ref, wuv_ref, za_ref, x_ref,
                          wout_ref, npost_ref, y_ref):
    mix_a = []
    for hh in range(MLA_HEADS):
        o = _dot(olat_ref[:, hh * KV_RANK:(hh + 1) * KV_RANK].astype(BF16), wuv_ref[hh])
        mix_a.append((o * _silu(za_ref[:, hh * V_HEAD:(hh + 1) * V_HEAD])).astype(BF16))
    mix_s = _ssd_finish(y_t_ref[...].T, act_ref[...], z_ref[...], dskip_ref[...], normw_ref[...])
    y_ref[...] = _out_proj(_out_proj_attn(jnp.concatenate(mix_a, axis=1), wout_ref), mix_s, wout_ref,
                           npost_ref[...], x_ref[...])


def _finish_sample(y_t, act, zs, olat2, za, x2, w):
    n = x2.shape[0]
    return pl.pallas_call(
        _finish_sample_kernel,
        grid=(1,),
        in_specs=[_full(y_t.shape), pl.BlockSpec((n, D_SSM), lambda i: (0, 0)), _full(zs.shape),
                  _full((1, D_SSM)), _full((1, D_SSM)), _full(olat2.shape), _full(w['w_uvh'].shape),
                  _full(za.shape), _full(x2.shape), _full((D_MIX, D_MODEL)), _full((1, D_MODEL))],
        out_specs=_full((n, D_MODEL)),
        out_shape=jax.ShapeDtypeStruct((n, D_MODEL), F32),
        compiler_params=pltpu.CompilerParams(dimension_semantics=("arbitrary",),
                                             vmem_limit_bytes=VMEM_LIMIT_BYTES),
        name="finish_sample",
    )(y_t, act, zs, w['d_skip_w'], w['ssm_norm'], olat2, w['w_uvh'], za, x2, w['w_out'], w['norm_post'])


ORIG_KR = Q_RANK + KV_RANK
ORIG_ZA = ORIG_KR + QK_ROPE
ORIG_DT = ORIG_ZA + D_ATTN + D_SSM + CONV_DIM
PREP_COLS = 128


def _prep_w_in_kernel(wt_ref, o_ref):
    def put(tile, src_rows):
        o_ref[:, tile * LANES:(tile + 1) * LANES] = src_rows.T.astype(BF16)

    for t in range(COL_ZA // LANES):
        put(t, wt_ref[t * LANES:(t + 1) * LANES, :])
    for t in range((COL_MISC - COL_ZA) // LANES):
        put(COL_ZA // LANES + t, wt_ref[ORIG_ZA + t * LANES:ORIG_ZA + (t + 1) * LANES, :])
    pad = jnp.zeros((LANES - QK_ROPE - SSD_HEADS, wt_ref.shape[1]), F32)
    put(COL_MISC // LANES, jnp.concatenate([wt_ref[ORIG_KR:ORIG_ZA, :], wt_ref[ORIG_DT:ORIG_DT + SSD_HEADS, :], pad],
                                           axis=0))


def _prep_w_in(w_in):
    k, n = w_in.shape
    assert n == ORIG_DT + SSD_HEADS and k % PREP_COLS == 0
    return pl.pallas_call(
        _prep_w_in_kernel,
        grid=(k // PREP_COLS,),
        in_specs=[pl.BlockSpec((n, PREP_COLS), lambda i: (0, i))],
        out_specs=pl.BlockSpec((PREP_COLS, D_IN_PAD), lambda i: (i, 0)),
        out_shape=jax.ShapeDtypeStruct((k, D_IN_PAD), BF16),
        compiler_params=pltpu.CompilerParams(dimension_semantics=("arbitrary",),
                                             vmem_limit_bytes=VMEM_LIMIT_BYTES),
        name="prep_w_in",
    )(w_in.T)


def _rope_tables(pos):
    inv_freq = ROPE_THETA ** (-jnp.arange(ROPE_HALF, dtype=F32) / ROPE_HALF)
    ang = pos.astype(F32)[:, None] * inv_freq[None, :]
    cos, sin = jnp.cos(ang), jnp.sin(ang)
    reps = LANES // QK_ROPE
    return jnp.tile(cos, (1, 2 * reps)), jnp.tile(jnp.concatenate([-sin, sin], axis=1), (1, reps))


def _prep_weights(lw):
    row = lambda v: v.reshape(1, -1).astype(F32)
    w_qb = lw['w_q_b'].reshape(Q_RANK, MLA_HEADS, QK_NOPE + QK_ROPE)
    lane_pad = lambda v: jnp.pad(v.reshape(1, -1).astype(F32), ((0, 0), (DT_LANE0, LANES - DT_LANE0 - SSD_HEADS)))
    head_of_col = jnp.arange(D_SSM) // SSD_HEADDIM
    return {
        'norm_pre': row(lw['norm_pre']),
        'w_in': _prep_w_in(lw['w_in']),
        'q_a_norm': row(lw['q_a_norm']),
        'w_qb': jnp.concatenate([w_qb[:, :, :QK_NOPE].reshape(Q_RANK, -1),
                                 w_qb[:, :, QK_NOPE:].reshape(Q_RANK, -1)], axis=1).astype(BF16),
        'kv_a_norm': row(lw['kv_a_norm']),
        'w_uk2t': lw['w_uk'].reshape(KV_RANK, MLA_HEADS * QK_NOPE).T.astype(BF16),
        'w_uv2': lw['w_uv'].reshape(KV_RANK, MLA_HEADS * V_HEAD).astype(BF16),
        'w_ukt': jnp.transpose(lw['w_uk'], (1, 2, 0)).astype(BF16),
        'w_uvh': jnp.transpose(lw['w_uv'], (1, 0, 2)).astype(BF16),
        'conv_w': lw['conv_w'].astype(F32),
        'conv_b': row(lw['conv_b']),
        'dt_bias_t': lane_pad(lw['dt_bias']),
        'a_log_t': lane_pad(lw['a_log']),
        'expand': sum((jnp.arange(LANES)[:, None] == DT_LANE0 + k * SSD_HEADS + head_of_col[None, :])
                      for k in range(SPLIT_PIECES)).astype(BF16),
        'd_skip_w': jnp.repeat(lw['d_skip'].astype(F32), SSD_HEADDIM).reshape(1, D_SSM),
        'ssm_norm': row(lw['ssm_norm']),
        'w_out': lw['w_out'].astype(BF16),
        'norm_post': row(lw['norm_post']),
    }


def _prompt_layer(x, w):
    b, s, _ = x.shape
    x2 = x.reshape(b * s, D_MODEL)
    cos, sin = _rope_tables(jnp.arange(s, dtype=jnp.int32))
    q, kt, v, ckv, kr, za, mix_s, state, tail = _proj_prompt(x2, cos, sin, w)
    r3 = lambda a: a.reshape(b, s, a.shape[-1])
    mix_a = _attn_prompt(r3(q), kt, r3(v), r3(za))
    y = _merge(mix_a.reshape(b * s, D_ATTN), mix_s, x2, w)
    h = state.reshape(b, SSD_GROUPS, D_STATE, HEADS_PER_GROUP, SSD_HEADDIM)
    h = jnp.transpose(h, (0, 1, 3, 4, 2)).reshape(b, SSD_HEADS, SSD_HEADDIM, D_STATE)
    return (y.reshape(b, s, D_MODEL), r3(ckv), jnp.swapaxes(kr, 1, 2), tail[:, CONV_PAD - (CONV_W - 1):, :], h)


def _sample_layer(layer, x, cache_c, cache_r, conv_prev, h0, page_table, w):
    b, s, _ = x.shape
    n = b * s
    past = page_table.shape[1] * PAGE_SIZE
    x2 = x.reshape(n, D_MODEL)
    pos = past + jnp.arange(s, dtype=jnp.int32)
    cos, sin = _rope_tables(jnp.tile(pos, b))
    qlat, qrope, ckv, kr, za, zs, xbc, misc = _proj_sample(x2, cos, sin, w)
    conv_new, act, xdt_t, dec_t = _post_sample(xbc, misc, jnp.swapaxes(conv_prev, 0, 1), w)
    olat, h, y_t = _attn_sample(
        layer, page_table, qlat.reshape(n, MLA_HEADS, KV_RANK), qrope.reshape(n, MLA_HEADS, QK_ROPE),
        ckv.reshape(n, 1, KV_RANK), kr.reshape(n, 1, QK_ROPE), h0.astype(F32).reshape(n, D_SSM, D_STATE),
        xdt_t, dec_t, act, cache_c, jnp.swapaxes(cache_r, 2, 3))
    y = _finish_sample(y_t, act, zs, olat.reshape(n, MLA_HEADS * KV_RANK), za, x2, w)
    return (y.reshape(b, s, D_MODEL), ckv.reshape(b, s, KV_RANK), kr.reshape(b, s, QK_ROPE),
            jnp.swapaxes(conv_new, 0, 1), h.reshape(b, SSD_HEADS, SSD_HEADDIM, D_STATE))


def kernel(x_prompt, x_sample, cache_ckv, cache_krope, state_conv, state_ssm, page_table, norm_pre, w_in,
           q_a_norm, w_q_b, kv_a_norm, w_uk, w_uv, conv_w, conv_b, dt_bias, a_log, d_skip, ssm_norm, w_out,
           norm_post):
    assert x_sample.shape[1] == 1, "the sample path handles one new token per sequence"
    depth = w_in.shape[0]
    y_prompt, y_sample = x_prompt, x_sample
    outs = [[] for _ in range(8)]
    for l in range(depth):
        w = _prep_weights({'norm_pre': norm_pre[l], 'w_in': w_in[l], 'q_a_norm': q_a_norm[l],
                           'w_q_b': w_q_b[l], 'kv_a_norm': kv_a_norm[l], 'w_uk': w_uk[l], 'w_uv': w_uv[l],
                           'conv_w': conv_w[l], 'conv_b': conv_b[l], 'dt_bias': dt_bias[l], 'a_log': a_log[l],
                           'd_skip': d_skip[l], 'ssm_norm': ssm_norm[l], 'w_out': w_out[l],
                           'norm_post': norm_post[l]})
        y_prompt, c1, k1, v1, h1 = _prompt_layer(y_prompt, w)
        y_sample, c2, k2, v2, h2 = _sample_layer(l, y_sample, cache_ckv, cache_krope, state_conv[l],
                                                 state_ssm[l], page_table, w)
        for lst, val in zip(outs, (c1, k1, v1, h1, c2, k2, v2, h2)):
            lst.append(val)
    return (y_prompt, y_sample) + tuple(jnp.stack(o) for o in outs)
```

```python
import functools
import math

import jax
import jax.numpy as jnp
from jax import lax
from jax.experimental import pallas as pl
from jax.experimental.pallas import tpu as pltpu

F32 = jnp.float32
BF16 = jnp.bfloat16

D_MODEL = 1024
PAGE_SIZE = 128
D_MIX = 2 * D_MODEL
D_ATTN = D_MIX // 2
D_SSM = D_MIX - D_ATTN
MLA_HEADS = 8
QK_NOPE = 128
QK_ROPE = 64
ROPE_HALF = QK_ROPE // 2
V_HEAD = D_ATTN // MLA_HEADS
Q_RANK = 384
KV_RANK = 256
ROPE_THETA = 10000.0
SOFTMAX_SCALE = (QK_NOPE + QK_ROPE) ** -0.5
SSD_HEADDIM = 64
SSD_HEADS = D_SSM // SSD_HEADDIM
SSD_GROUPS = 2
HEADS_PER_GROUP = SSD_HEADS // SSD_GROUPS
GROUP_WIDTH = D_SSM // SSD_GROUPS
D_STATE = 128
CONV_W = 4
CONV_DIM = D_SSM + 2 * SSD_GROUPS * D_STATE
CHUNK = 128
EPS = 1e-6
SPLIT_PIECES = 3
NEG_BIG = -1e30

LANES = 128
SUBLANES = 8
VMEM_LIMIT_BYTES = 56 * 1024 * 1024

COL_Q = 0
COL_C = COL_Q + Q_RANK
COL_ZA = COL_C + KV_RANK
COL_ZS = COL_ZA + D_ATTN
COL_XBC = COL_ZS + D_SSM
COL_MISC = COL_XBC + CONV_DIM
D_IN_PAD = COL_MISC + LANES
DT_LANE0 = QK_ROPE
QK_PAD = 2 * LANES

PROJ_ROWS = 512
ATTN_TQ = 256
ATTN_HEADS = 4
SSD_STEP_CHUNKS = 4
MERGE_ROWS = 1024
MERGE_SUB_ROWS = 256
CONV_PAD = SUBLANES
DEC_KV_CHUNK = 2048
DEC_ATTN_SEQS = 2

_NT = (((1,), (1,)), ((), ()))


def _rms(x, w):
    return x * lax.rsqrt(jnp.mean(x * x, axis=-1, keepdims=True) + EPS) * w


def _silu(x):
    return x / (1.0 + jnp.exp(-x))


def _softplus(x):
    return jnp.maximum(x, 0.0) + jnp.log(1.0 + jnp.exp(-jnp.abs(x)))


def _dot(a, b):
    return jnp.dot(a, b, preferred_element_type=F32)


def _rope_tile(x, cos, sin_signed):
    lane = lax.broadcasted_iota(jnp.int32, x.shape, 1)
    first_half = (lane % QK_ROPE) < ROPE_HALF
    partner = jnp.where(first_half,
                        pltpu.roll(x, LANES - ROPE_HALF, 1),
                        pltpu.roll(x, ROPE_HALF, 1))
    return x * cos + partner * sin_signed


def _front(x_ref, npre_ref, win_ref):
    h = _rms(x_ref[...], npre_ref[...]).astype(BF16)

    def seg(lo, hi):
        return _dot(h, win_ref[:, lo:hi])

    return seg


def _proj_prompt_kernel(seq_tiles, x_ref, cos_ref, sin_ref, npre_ref, win_ref, qan_ref, wqb_ref, kvn_ref,
                        wukt_ref, wuv_ref, cw_ref, cb_ref, dtb_ref, alog_ref, expand_ref, dskip_ref, normw_ref,
                        q_ref, kt_ref, v_ref, ckv_ref, kr_ref, za_ref, mixs_ref, state_ref, tail_ref, xp_ref):
    rows = x_ref.shape[0]
    first = pl.program_id(0) % seq_tiles == 0

    @pl.when(first)
    def _():
        xp_ref[0:CONV_PAD, :] = jnp.zeros((CONV_PAD, CONV_DIM), F32)
        state_ref[...] = jnp.zeros(state_ref.shape, F32)

    seg = _front(x_ref, npre_ref, win_ref)
    xp_ref[CONV_PAD:CONV_PAD + rows, :] = seg(COL_XBC, COL_MISC)
    xp = xp_ref[...]
    acc = cb_ref[...]
    for tap in range(CONV_W):
        back = CONV_W - 1 - tap
        shifted = xp if back == 0 else pltpu.roll(xp, back, 0)
        acc = acc + shifted[CONV_PAD:CONV_PAD + rows, :] * cw_ref[tap:tap + 1, :]
    act_all = _silu(acc)
    tail_ref[...] = xp_ref[rows:rows + CONV_PAD, :]
    xp_ref[CONV_PAD - 3:CONV_PAD, :] = xp_ref[CONV_PAD + rows - 3:CONV_PAD + rows, :]

    misc = seg(COL_MISC, D_IN_PAD)
    cos = cos_ref[...]
    sin = sin_ref[...]
    lane = lax.broadcasted_iota(jnp.int32, misc.shape, 1)
    low = lane < QK_ROPE

    kr_full = _rope_tile(misc, cos, sin)
    kr_t = jnp.where(low, kr_full, 0.0).T
    kr_ref[...] = kr_t[:QK_ROPE, :]
    kr_lo_t = kr_t.astype(BF16)
    kr_hi_t = jnp.where(low, 0.0, pltpu.roll(kr_full, QK_ROPE, 1)).T.astype(BF16)

    c_raw = seg(COL_C, COL_ZA)
    q_a = seg(COL_Q, COL_C)
    za_ref[...] = seg(COL_ZA, COL_ZS)
    ckv = _rms(c_raw, kvn_ref[...])
    ckv_ref[...] = ckv
    cb = ckv.astype(BF16)
    knope_t = lax.dot_general(wukt_ref[...], cb, _NT, preferred_element_type=F32)
    v_ref[...] = _dot(cb, wuv_ref[...]).astype(BF16)

    qn = _rms(q_a, qan_ref[...]).astype(BF16)
    q = _dot(qn, wqb_ref[...])
    nope_w = MLA_HEADS * QK_NOPE
    for hh in range(MLA_HEADS):
        pair = hh // 2
        r = _rope_tile(q[:, nope_w + pair * LANES: nope_w + (pair + 1) * LANES], cos, sin)
        own = low if hh % 2 == 0 else jnp.logical_not(low)
        base = hh * QK_PAD
        q_ref[:, base:base + LANES] = (q[:, hh * QK_NOPE:(hh + 1) * QK_NOPE] * SOFTMAX_SCALE).astype(BF16)
        q_ref[:, base + LANES:base + QK_PAD] = (jnp.where(own, r, 0.0) * SOFTMAX_SCALE).astype(BF16)
        kt_ref[base:base + LANES, :] = knope_t[hh * QK_NOPE:(hh + 1) * QK_NOPE, :].astype(BF16)
        kt_ref[base + LANES:base + QK_PAD, :] = kr_lo_t if hh % 2 == 0 else kr_hi_t

    zs = seg(COL_ZS, COL_XBC)
    for k in range(rows // CHUNK):
        r0 = k * CHUNK
        act = act_all[r0:r0 + CHUNK, :]
        y = _ssd_chunk(act, misc[r0:r0 + CHUNK, :], dtb_ref, alog_ref, expand_ref, state_ref)
        mixs_ref[r0:r0 + CHUNK, :] = _ssd_finish(y, act[:, :D_SSM], zs[r0:r0 + CHUNK, :], dskip_ref[...],
                                                 normw_ref[...])


def _proj_sample_kernel(x_ref, cos_ref, sin_ref, npre_ref, win_ref, qan_ref, wqb_ref, kvn_ref, wukt_ref,
                        qlat_ref, qrope_ref, ckv_ref, kr_ref, za_ref, zs_ref, xbc_ref, misc_ref):
    seg = _front(x_ref, npre_ref, win_ref)
    za_ref[...] = seg(COL_ZA, COL_ZS)
    zs_ref[...] = seg(COL_ZS, COL_XBC)
    xbc_ref[...] = seg(COL_XBC, COL_MISC)
    misc = seg(COL_MISC, D_IN_PAD)
    misc_ref[...] = misc
    cos = cos_ref[...]
    sin = sin_ref[...]
    kr_ref[...] = _rope_tile(misc, cos, sin)[:, :QK_ROPE]
    ckv_ref[...] = _rms(seg(COL_C, COL_ZA), kvn_ref[...])

    qn = _rms(seg(COL_Q, COL_C), qan_ref[...]).astype(BF16)
    q = _dot(qn, wqb_ref[...])
    nope_w = MLA_HEADS * QK_NOPE
    for pair in range(MLA_HEADS // 2):
        lo = nope_w + pair * LANES
        qrope_ref[:, pair * LANES:(pair + 1) * LANES] = _rope_tile(q[:, lo:lo + LANES], cos, sin) * SOFTMAX_SCALE
    for hh in range(MLA_HEADS):
        qh = q[:, hh * QK_NOPE:(hh + 1) * QK_NOPE].astype(BF16)
        qlat_ref[:, hh * KV_RANK:(hh + 1) * KV_RANK] = _dot(qh, wukt_ref[hh]) * SOFTMAX_SCALE


def _full(shape):
    return pl.BlockSpec(shape, lambda *_: (0,) * len(shape))


def _proj_prompt(x2, cos, sin, w):
    n = x2.shape[0]
    tm = PROJ_ROWS
    seq_tiles = cos.shape[0] // tm
    assert tm == SSD_STEP_CHUNKS * CHUNK
    batch = n // cos.shape[0]
    rows = lambda width: pl.BlockSpec((tm, width), lambda i: (i, 0))
    tab = pl.BlockSpec((tm, LANES), lambda i: (i % seq_tiles, 0))
    out_shape = (
        jax.ShapeDtypeStruct((n, MLA_HEADS * QK_PAD), BF16),
        jax.ShapeDtypeStruct((MLA_HEADS * QK_PAD, n), BF16),
        jax.ShapeDtypeStruct((n, D_ATTN), BF16),
        jax.ShapeDtypeStruct((n, KV_RANK), F32),
        jax.ShapeDtypeStruct((batch, QK_ROPE, cos.shape[0]), F32),
        jax.ShapeDtypeStruct((n, D_ATTN), F32),
        jax.ShapeDtypeStruct((n, D_SSM), BF16),
        jax.ShapeDtypeStruct((batch, SSD_GROUPS, D_STATE, GROUP_WIDTH), F32),
        jax.ShapeDtypeStruct((batch, CONV_PAD, CONV_DIM), F32),
    )
    per_seq = lambda shape: pl.BlockSpec((None,) + shape[1:], lambda i: (i // seq_tiles,) + (0,) * (len(shape) - 1))
    out_specs = [rows(s.shape[1]) for s in out_shape[:7]]
    out_specs[1] = pl.BlockSpec((MLA_HEADS * QK_PAD, tm), lambda i: (0, i))
    out_specs[4] = pl.BlockSpec((None, QK_ROPE, tm), lambda i: (i // seq_tiles, 0, i % seq_tiles))
    out_specs += [per_seq(out_shape[7].shape), per_seq(out_shape[8].shape)]
    return pl.pallas_call(
        functools.partial(_proj_prompt_kernel, seq_tiles),
        grid=(n // tm,),
        in_specs=[rows(D_MODEL), tab, tab, _full((1, D_MODEL)), _full((D_MODEL, D_IN_PAD)),
                  _full((1, Q_RANK)), _full(w['w_qb'].shape), _full((1, KV_RANK)),
                  _full(w['w_uk2t'].shape), _full(w['w_uv2'].shape),
                  _full((CONV_W, CONV_DIM)), _full((1, CONV_DIM)), _full((1, LANES)), _full((1, LANES)),
                  _full((LANES, D_SSM)), _full((1, D_SSM)), _full((1, D_SSM))],
        out_specs=tuple(out_specs),
        out_shape=out_shape,
        scratch_shapes=[pltpu.VMEM((CONV_PAD + tm, CONV_DIM), F32)],
        compiler_params=pltpu.CompilerParams(dimension_semantics=("arbitrary",),
                                             vmem_limit_bytes=VMEM_LIMIT_BYTES),
        name="proj_prompt",
    )(x2, cos, sin, w['norm_pre'], w['w_in'], w['q_a_norm'], w['w_qb'], w['kv_a_norm'],
      w['w_uk2t'], w['w_uv2'], w['conv_w'], w['conv_b'], w['dt_bias_t'], w['a_log_t'], w['expand'],
      w['d_skip_w'], w['ssm_norm'])


def _proj_sample(x2, cos, sin, w):
    n = x2.shape[0]
    rows = lambda width: pl.BlockSpec((n, width), lambda i: (0, 0))
    out_shape = (
        jax.ShapeDtypeStruct((n, MLA_HEADS * KV_RANK), F32),
        jax.ShapeDtypeStruct((n, MLA_HEADS * QK_ROPE), F32),
        jax.ShapeDtypeStruct((n, KV_RANK), F32),
        jax.ShapeDtypeStruct((n, QK_ROPE), F32),
        jax.ShapeDtypeStruct((n, D_ATTN), F32),
        jax.ShapeDtypeStruct((n, D_SSM), F32),
        jax.ShapeDtypeStruct((n, CONV_DIM), F32),
        jax.ShapeDtypeStruct((n, LANES), F32),
    )
    return pl.pallas_call(
        _proj_sample_kernel,
        grid=(1,),
        in_specs=[rows(D_MODEL), rows(LANES), rows(LANES), _full((1, D_MODEL)), _full((D_MODEL, D_IN_PAD)),
                  _full((1, Q_RANK)), _full(w['w_qb'].shape), _full((1, KV_RANK)),
                  _full(w['w_ukt'].shape)],
        out_specs=tuple(rows(s.shape[1]) for s in out_shape),
        out_shape=out_shape,
        compiler_params=pltpu.CompilerParams(dimension_semantics=("arbitrary",),
                                             vmem_limit_bytes=VMEM_LIMIT_BYTES),
        name="proj_sample",
    )(x2, cos, sin, w['norm_pre'], w['w_in'], w['q_a_norm'], w['w_qb'], w['kv_a_norm'], w['w_ukt'])


def _attn_prompt_kernel(q_ref, kt_ref, v_ref, z_ref, o_ref):
    seq = q_ref.shape[0]
    tq = ATTN_TQ
    row = lax.broadcasted_iota(jnp.int32, (tq, tq), 0)
    col = lax.broadcasted_iota(jnp.int32, (tq, tq), 1)
    causal = col <= row
    def score_phase(qi):
        lim = (qi + 1) * tq
        return [_dot(q_ref[qi * tq:lim, hd * QK_PAD:(hd + 1) * QK_PAD], kt_ref[hd * QK_PAD:(hd + 1) * QK_PAD, 0:lim])
                for hd in range(ATTN_HEADS)]

    def softmax_phase(qi, scores):
        lim = (qi + 1) * tq
        probs = []
        for s in scores:
            diag = jnp.where(causal, s[:, lim - tq:], NEG_BIG)
            s = diag if qi == 0 else jnp.concatenate([s[:, :lim - tq], diag], axis=1)
            m = jnp.max(s, axis=-1, keepdims=True)
            probs.append(jnp.exp(s - m).astype(BF16))
        return probs

    def value_phase(qi, probs):
        lim = (qi + 1) * tq
        ones_col = (lax.broadcasted_iota(jnp.int32, (lim, LANES), 1) == 0).astype(BF16)
        for hd, p in enumerate(probs):
            hv = slice(hd * V_HEAD, (hd + 1) * V_HEAD)
            ol = _dot(p, jnp.concatenate([v_ref[0:lim, hv], ones_col], axis=1))
            o = ol[:, :V_HEAD] / jnp.sum(ol[:, V_HEAD:], axis=-1, keepdims=True)
            o_ref[qi * tq:lim, hv] = (o * _silu(z_ref[qi * tq:lim, hv])).astype(BF16)

    for qi in reversed(range(seq // tq)):
        value_phase(qi, softmax_phase(qi, score_phase(qi)))


def _attn_prompt(q3, kt, v3, z3):
    b, s, _ = q3.shape
    nh = ATTN_HEADS
    hv = pl.BlockSpec((None, s, nh * V_HEAD), lambda i, j: (i, 0, j))
    return pl.pallas_call(
        _attn_prompt_kernel,
        grid=(b, MLA_HEADS // nh),
        in_specs=[pl.BlockSpec((None, s, nh * QK_PAD), lambda i, j: (i, 0, j)),
                  pl.BlockSpec((nh * QK_PAD, s), lambda i, j: (j, i)), hv, hv],
        out_specs=hv,
        out_shape=jax.ShapeDtypeStruct((b, s, D_ATTN), BF16),
        compiler_params=pltpu.CompilerParams(dimension_semantics=("arbitrary", "arbitrary"),
                                             vmem_limit_bytes=VMEM_LIMIT_BYTES),
        name="attn_prompt",
    )(q3, kt, v3, z3)


def _dt_lanes(shape):
    lane = lax.broadcasted_iota(jnp.int32, shape, 1)
    return jnp.logical_and(lane >= DT_LANE0, lane < DT_LANE0 + SSD_HEADS)


def _dt_and_da(misc, dtb, alog):
    dt = jnp.where(_dt_lanes(misc.shape), _softplus(misc + dtb), 0.0)
    return dt, dt * (-jnp.exp(alog))


def _split_bf16(x):
    pieces = []
    for _ in range(SPLIT_PIECES):
        piece = x.astype(BF16).astype(F32)
        pieces.append(piece)
        x = x - piece
    return pieces


def _cumsum_rows(lower_b, x):
    return sum(_dot(lower_b, piece.astype(BF16)) for piece in _split_bf16(x))


def _head_expand(x, expand_ref):
    pieces = _split_bf16(jnp.where(_dt_lanes(x.shape), x, 0.0))
    packed = pieces[0]
    for k in range(1, SPLIT_PIECES):
        packed = packed + pltpu.roll(pieces[k], k * SSD_HEADS, 1)
    return _dot(packed.astype(BF16), expand_ref[...])


def _ssd_finish(y, xs, z, dskip, normw):
    y = y + dskip * xs
    gated = y * _silu(z)
    outs = []
    for g in range(SSD_GROUPS):
        sl = slice(g * GROUP_WIDTH, (g + 1) * GROUP_WIDTH)
        outs.append(_rms(gated[:, sl], normw[:, sl]))
    return jnp.concatenate(outs, axis=1).astype(BF16)


def _out_proj_attn(mix_a, wout_ref):
    return _dot(mix_a, wout_ref[0:D_ATTN, :])


def _out_proj(o_attn, mix_s, wout_ref, npost, x):
    o = o_attn + _dot(mix_s, wout_ref[D_ATTN:D_MIX, :])
    return x + _rms(o, npost)


def _ssd_chunk(act, misc, dtb_ref, alog_ref, expand_ref, state_ref):
    xs = act[:, :D_SSM]
    dt, da = _dt_and_da(misc, dtb_ref[...], alog_ref[...])
    row = lax.broadcasted_iota(jnp.int32, (CHUNK, CHUNK), 0)
    col = lax.broadcasted_iota(jnp.int32, (CHUNK, CHUNK), 1)
    lower = row >= col
    a_cum = _cumsum_rows(lower.astype(BF16), da)
    a_cum_t = a_cum.T
    a_last = a_cum[CHUNK - 1:CHUNK, :]
    xdt = xs * _head_expand(dt, expand_ref)
    decay_out = _head_expand(jnp.exp(a_cum), expand_ref)
    state_decay = decay_out[CHUNK - 1:CHUNK, :]
    xdt_end = (xdt * _head_expand(jnp.exp(a_last - a_cum), expand_ref)).astype(BF16)
    xdt_b = xdt.astype(BF16)
    lane_w = lax.broadcasted_iota(jnp.int32, (CHUNK, LANES), 1)
    first_head = lane_w < SSD_HEADDIM

    ys = []
    for g in range(SSD_GROUPS):
        bm = act[:, D_SSM + g * D_STATE:D_SSM + (g + 1) * D_STATE]
        cm = act[:, D_SSM + (SSD_GROUPS + g) * D_STATE:D_SSM + (SSD_GROUPS + g + 1) * D_STATE]
        bm_b = bm.astype(BF16)
        cm_b = cm.astype(BF16)
        cb = lax.dot_general(cm_b, bm_b, _NT, preferred_element_type=F32)
        gsl = slice(g * GROUP_WIDTH, (g + 1) * GROUP_WIDTH)
        state = state_ref[g]
        y_off = _dot(cm_b, state.astype(BF16)) * decay_out[:, gsl]
        y_diag = []
        for pair in range(HEADS_PER_GROUP // 2):
            halves = []
            x_pair = xdt_b[:, g * GROUP_WIDTH + pair * LANES: g * GROUP_WIDTH + (pair + 1) * LANES]
            for k in range(2):
                lane_h = DT_LANE0 + g * HEADS_PER_GROUP + 2 * pair + k
                seg = a_cum[:, lane_h:lane_h + 1] - a_cum_t[lane_h:lane_h + 1, :]
                decay = jnp.exp(jnp.where(lower, seg, NEG_BIG))
                halves.append(_dot((cb * decay).astype(BF16), x_pair))
            y_diag.append(jnp.where(first_head, halves[0], halves[1]))
        ys.append(jnp.concatenate(y_diag, axis=1) + y_off)
        state_ref[g] = state * state_decay[:, gsl] + _dot(bm.T.astype(BF16), xdt_end[:, gsl])
    return jnp.concatenate(ys, axis=1)


def _merge_kernel(ma_ref, ms_ref, x_ref, wout_ref, npost_ref, y_ref):
    for r0 in range(0, x_ref.shape[0], MERGE_SUB_ROWS):
        r = slice(r0, r0 + MERGE_SUB_ROWS)
        y_ref[r, :] = _out_proj(_out_proj_attn(ma_ref[r, :], wout_ref), ms_ref[r, :], wout_ref, npost_ref[...],
                                x_ref[r, :])


def _merge(mix_a, mix_s, x2, w):
    n = x2.shape[0]
    tm = MERGE_ROWS
    rows = lambda width: pl.BlockSpec((tm, width), lambda i: (i, 0))
    return pl.pallas_call(
        _merge_kernel,
        grid=(n // tm,),
        in_specs=[rows(D_ATTN), rows(D_SSM), rows(D_MODEL), _full((D_MIX, D_MODEL)), _full((1, D_MODEL))],
        out_specs=rows(D_MODEL),
        out_shape=jax.ShapeDtypeStruct((n, D_MODEL), F32),
        compiler_params=pltpu.CompilerParams(dimension_semantics=("arbitrary",),
                                             vmem_limit_bytes=VMEM_LIMIT_BYTES),
        name="merge",
    )(mix_a, mix_s, x2, w['w_out'], w['norm_post'])


def _attn_sample_kernel(layer, pt_ref, qlat_ref, qrope_ref, cnew_ref, rnew_ref, h0_ref, xdt_t_ref, dec_t_ref,
                        act_ref, cache_c_ref, cache_rt_ref, o_ref, h_ref, y_t_ref, cbuf, rbuf, sem):
    i = pl.program_id(0)
    n = pl.num_programs(0)
    n_pages = pt_ref.shape[1]
    seqs = qlat_ref.shape[0]
    past = n_pages * PAGE_SIZE
    slot = i % 2

    def page_copies(step, slot_):
        copies = []
        for j in range(seqs):
            buf = slot_ * seqs + j
            for p in range(n_pages):
                page = pt_ref[step * seqs + j, p]
                dst = pl.ds(p * PAGE_SIZE, PAGE_SIZE)
                copies.append(pltpu.make_async_copy(cache_c_ref.at[layer, page], cbuf.at[buf, dst],
                                                    sem.at[0, slot_]))
                copies.append(pltpu.make_async_copy(cache_rt_ref.at[layer, page], rbuf.at[buf, :, dst],
                                                    sem.at[1, slot_]))
        return copies

    @pl.when(i == 0)
    def _():
        for cp in page_copies(0, 0):
            cp.start()

    @pl.when(i + 1 < n)
    def _():
        for cp in page_copies(i + 1, 1 - slot):
            cp.start()

    for cp in page_copies(i, slot):
        cp.wait()

    q_pos = past
    chunks = [pl.ds(c * DEC_KV_CHUNK, DEC_KV_CHUNK) for c in range(past // DEC_KV_CHUNK)]
    scores = []
    for j in range(seqs):
        buf = slot * seqs + j
        qlat = qlat_ref[j]
        qrope = qrope_ref[j]
        scores.append(jnp.concatenate(
            [lax.dot_general(qlat, cbuf[buf, keys, :], _NT, preferred_element_type=F32)
             + _dot(qrope, rbuf[buf, :, keys]) for keys in chunks], axis=1))
    probs = []
    for j in range(seqs):
        cnew = cnew_ref[j]
        s_new = (jnp.sum(qlat_ref[j] * cnew, axis=-1, keepdims=True)
                 + jnp.sum(qrope_ref[j] * rnew_ref[j], axis=-1, keepdims=True))
        k_pos = lax.broadcasted_iota(jnp.int32, scores[j].shape, 1)
        s = jnp.where(k_pos <= q_pos, scores[j], NEG_BIG)
        m = jnp.maximum(jnp.max(s, axis=-1, keepdims=True), s_new)
        p = jnp.exp(s - m)
        p_new = jnp.exp(s_new - m)
        probs.append((p, p_new, jnp.sum(p, axis=-1, keepdims=True) + p_new))
    for j in range(seqs):
        buf = slot * seqs + j
        p, p_new, l = probs[j]
        acc = p_new * cnew_ref[j]
        for c, keys in enumerate(chunks):
            acc = acc + _dot(p[:, c * DEC_KV_CHUNK:(c + 1) * DEC_KV_CHUNK], cbuf[buf, keys, :])
        o_ref[j] = acc / l

    @pl.when(i == 0)
    def _():
        y_t_ref[...] = jnp.zeros(y_t_ref.shape, F32)

    rows, n_seq = xdt_t_ref.shape
    lane = lax.broadcasted_iota(jnp.int32, (rows, n_seq), 1)
    xdt_t = xdt_t_ref[...]
    dec_t = dec_t_ref[...]
    for j in range(seqs):
        seq = i * seqs + j
        own = lane == seq
        x_col = jnp.sum(jnp.where(own, xdt_t, 0.0), axis=1, keepdims=True)
        d_col = jnp.sum(jnp.where(own, dec_t, 0.0), axis=1, keepdims=True)
        bc = act_ref[pl.ds(seq, 1), :]
        b_rows = jnp.concatenate(
            [jnp.broadcast_to(bc[:, D_SSM + g * D_STATE:D_SSM + (g + 1) * D_STATE], (GROUP_WIDTH, D_STATE))
             for g in range(SSD_GROUPS)], axis=0)
        c_rows = jnp.concatenate(
            [jnp.broadcast_to(bc[:, D_SSM + (SSD_GROUPS + g) * D_STATE:D_SSM + (SSD_GROUPS + g + 1) * D_STATE],
                              (GROUP_WIDTH, D_STATE)) for g in range(SSD_GROUPS)], axis=0)
        h = d_col * h0_ref[j] + x_col * b_rows
        h_ref[j] = h
        y_col = jnp.sum(h * c_rows, axis=1, keepdims=True)
        y_t_ref[...] = jnp.where(own, y_col, y_t_ref[...])


def _attn_sample(layer, page_table, qlat3, qrope3, cnew3, rnew3, h0, xdt_t, dec_t, act, cache_c, cache_rt):
    b, n_pages = page_table.shape
    past = n_pages * PAGE_SIZE
    seqs = DEC_ATTN_SEQS
    assert past % DEC_KV_CHUNK == 0 and b % seqs == 0
    per_step = lambda d1, d2: pl.BlockSpec((seqs, d1, d2), lambda i, pt: (i, 0, 0))
    whole = lambda a: pl.BlockSpec(a.shape, lambda i, pt: (0,) * a.ndim)
    grid_spec = pltpu.PrefetchScalarGridSpec(
        num_scalar_prefetch=1,
        grid=(b // seqs,),
        in_specs=[per_step(MLA_HEADS, KV_RANK), per_step(MLA_HEADS, QK_ROPE), per_step(1, KV_RANK),
                  per_step(1, QK_ROPE), per_step(D_SSM, D_STATE), whole(xdt_t), whole(dec_t), whole(act),
                  pl.BlockSpec(memory_space=pl.ANY), pl.BlockSpec(memory_space=pl.ANY)],
        out_specs=(per_step(MLA_HEADS, KV_RANK), per_step(D_SSM, D_STATE), whole(xdt_t)),
        scratch_shapes=[pltpu.VMEM((2 * seqs, past, KV_RANK), F32), pltpu.VMEM((2 * seqs, QK_ROPE, past), F32),
                        pltpu.SemaphoreType.DMA((2, 2))],
    )
    return pl.pallas_call(
        functools.partial(_attn_sample_kernel, layer),
        grid_spec=grid_spec,
        out_shape=(jax.ShapeDtypeStruct((b, MLA_HEADS, KV_RANK), F32), jax.ShapeDtypeStruct(h0.shape, F32),
                   jax.ShapeDtypeStruct(xdt_t.shape, F32)),
        compiler_params=pltpu.CompilerParams(dimension_semantics=("arbitrary",),
                                             vmem_limit_bytes=VMEM_LIMIT_BYTES),
        name="attn_sample",
    )(page_table, qlat3, qrope3, cnew3, rnew3, h0, xdt_t, dec_t, act, cache_c, cache_rt)


def _post_sample_kernel(xbc_ref, misc_ref, cprev_ref, cw_ref, cb_ref, dtb_ref, alog_ref, expand_ref,
                        cnew_ref, act_ref, xdt_t_ref, dec_t_ref):
    xb = xbc_ref[...]
    acc = cb_ref[...]
    for k in range(CONV_W - 1):
        acc = acc + cprev_ref[k] * cw_ref[k:k + 1, :]
    acc = acc + xb * cw_ref[CONV_W - 1:CONV_W, :]
    act = _silu(acc)
    act_ref[...] = act
    for k in range(CONV_W - 2):
        cnew_ref[k] = cprev_ref[k + 1]
    cnew_ref[CONV_W - 2] = xb

    dt, da = _dt_and_da(misc_ref[...], dtb_ref[...], alog_ref[...])
    xdt = act[:, :D_SSM] * _head_expand(dt, expand_ref)
    decay = _head_expand(jnp.exp(da), expand_ref)
    xdt_t_ref[...] = xdt.T
    dec_t_ref[...] = decay.T


def _post_sample(xbc, misc, cprev3, w):
    n = xbc.shape[0]
    full = lambda a: _full(a.shape)
    args = (xbc, misc, cprev3, w['conv_w'], w['conv_b'], w['dt_bias_t'], w['a_log_t'], w['expand'])
    out_shape = (
        jax.ShapeDtypeStruct((CONV_W - 1, n, CONV_DIM), F32),
        jax.ShapeDtypeStruct((n, CONV_DIM), F32),
        jax.ShapeDtypeStruct((D_SSM, n), F32),
        jax.ShapeDtypeStruct((D_SSM, n), F32),
    )
    return pl.pallas_call(
        _post_sample_kernel,
        grid=(1,),
        in_specs=[full(a) for a in args],
        out_specs=tuple(_full(s.shape) for s in out_shape),
        out_shape=out_shape,
        compiler_params=pltpu.CompilerParams(dimension_semantics=("arbitrary",),
                                             vmem_limit_bytes=VMEM_LIMIT_BYTES),
        name="post_sample",
    )(*args)


def _finish_sample_kernel(y_t_ref, act_ref, z_ref, dskip_ref, normw_ref, olat_ref, wuv_ref, za_ref, x_ref,
                          wout_ref, npost_ref, y_ref):
    mix_a = []
    for hh in range(MLA_HEADS):
        o = _dot(olat_ref[:, hh * KV_RANK:(hh + 1) * KV_RANK].astype(BF16), wuv_ref[hh])
        mix_a.append((o * _silu(za_ref[:, hh * V_HEAD:(hh + 1) * V_HEAD])).astype(BF16))
    mix_s = _ssd_finish(y_t_ref[...].T, act_ref[...], z_ref[...], dskip_ref[...], normw_ref[...])
    y_ref[...] = _out_proj(_out_proj_attn(jnp.concatenate(mix_a, axis=1), wout_ref), mix_s, wout_ref,
                           npost_ref[...], x_ref[...])


def _finish_sample(y_t, act, zs, olat2, za, x2, w):
    n = x2.shape[0]
    return pl.pallas_call(
        _finish_sample_kernel,
        grid=(1,),
        in_specs=[_full(y_t.shape), pl.BlockSpec((n, D_SSM), lambda i: (0, 0)), _full(zs.shape),
                  _full((1, D_SSM)), _full((1, D_SSM)), _full(olat2.shape), _full(w['w_uvh'].shape),
                  _full(za.shape), _full(x2.shape), _full((D_MIX, D_MODEL)), _full((1, D_MODEL))],
        out_specs=_full((n, D_MODEL)),
        out_shape=jax.ShapeDtypeStruct((n, D_MODEL), F32),
        compiler_params=pltpu.CompilerParams(dimension_semantics=("arbitrary",),
                                             vmem_limit_bytes=VMEM_LIMIT_BYTES),
        name="finish_sample",
    )(y_t, act, zs, w['d_skip_w'], w['ssm_norm'], olat2, w['w_uvh'], za, x2, w['w_out'], w['norm_post'])


ORIG_KR = Q_RANK + KV_RANK
ORIG_ZA = ORIG_KR + QK_ROPE
ORIG_DT = ORIG_ZA + D_ATTN + D_SSM + CONV_DIM
PREP_COLS = 128


def _prep_w_in_kernel(wt_ref, o_ref):
    def put(tile, src_rows):
        o_ref[:, tile * LANES:(tile + 1) * LANES] = src_rows.T.astype(BF16)

    for t in range(COL_ZA // LANES):
        put(t, wt_ref[t * LANES:(t + 1) * LANES, :])
    for t in range((COL_MISC - COL_ZA) // LANES):
        put(COL_ZA // LANES + t, wt_ref[ORIG_ZA + t * LANES:ORIG_ZA + (t + 1) * LANES, :])
    pad = jnp.zeros((LANES - QK_ROPE - SSD_HEADS, wt_ref.shape[1]), F32)
    put(COL_MISC // LANES, jnp.concatenate([wt_ref[ORIG_KR:ORIG_ZA, :], wt_ref[ORIG_DT:ORIG_DT + SSD_HEADS, :], pad],
                                           axis=0))


def _prep_w_in(w_in):
    k, n = w_in.shape
    assert n == ORIG_DT + SSD_HEADS and k % PREP_COLS == 0
    return pl.pallas_call(
        _prep_w_in_kernel,
        grid=(k // PREP_COLS,),
        in_specs=[pl.BlockSpec((n, PREP_COLS), lambda i: (0, i))],
        out_specs=pl.BlockSpec((PREP_COLS, D_IN_PAD), lambda i: (i, 0)),
        out_shape=jax.ShapeDtypeStruct((k, D_IN_PAD), BF16),
        compiler_params=pltpu.CompilerParams(dimension_semantics=("arbitrary",),
                                             vmem_limit_bytes=VMEM_LIMIT_BYTES),
        name="prep_w_in",
    )(w_in.T)


def _rope_tables(pos):
    inv_freq = ROPE_THETA ** (-jnp.arange(ROPE_HALF, dtype=F32) / ROPE_HALF)
    ang = pos.astype(F32)[:, None] * inv_freq[None, :]
    cos, sin = jnp.cos(ang), jnp.sin(ang)
    reps = LANES // QK_ROPE
    return jnp.tile(cos, (1, 2 * reps)), jnp.tile(jnp.concatenate([-sin, sin], axis=1), (1, reps))


def _prep_weights(lw):
    row = lambda v: v.reshape(1, -1).astype(F32)
    w_qb = lw['w_q_b'].reshape(Q_RANK, MLA_HEADS, QK_NOPE + QK_ROPE)
    lane_pad = lambda v: jnp.pad(v.reshape(1, -1).astype(F32), ((0, 0), (DT_LANE0, LANES - DT_LANE0 - SSD_HEADS)))
    head_of_col = jnp.arange(D_SSM) // SSD_HEADDIM
    return {
        'norm_pre': row(lw['norm_pre']),
        'w_in': _prep_w_in(lw['w_in']),
        'q_a_norm': row(lw['q_a_norm']),
        'w_qb': jnp.concatenate([w_qb[:, :, :QK_NOPE].reshape(Q_RANK, -1),
                                 w_qb[:, :, QK_NOPE:].reshape(Q_RANK, -1)], axis=1).astype(BF16),
        'kv_a_norm': row(lw['kv_a_norm']),
        'w_uk2t': lw['w_uk'].reshape(KV_RANK, MLA_HEADS * QK_NOPE).T.astype(BF16),
        'w_uv2': lw['w_uv'].reshape(KV_RANK, MLA_HEADS * V_HEAD).astype(BF16),
        'w_ukt': jnp.transpose(lw['w_uk'], (1, 2, 0)).astype(BF16),
        'w_uvh': jnp.transpose(lw['w_uv'], (1, 0, 2)).astype(BF16),
        'conv_w': lw['conv_w'].astype(F32),
        'conv_b': row(lw['conv_b']),
        'dt_bias_t': lane_pad(lw['dt_bias']),
        'a_log_t': lane_pad(lw['a_log']),
        'expand': sum((jnp.arange(LANES)[:, None] == DT_LANE0 + k * SSD_HEADS + head_of_col[None, :])
                      for k in range(SPLIT_PIECES)).astype(BF16),
        'd_skip_w': jnp.repeat(lw['d_skip'].astype(F32), SSD_HEADDIM).reshape(1, D_SSM),
        'ssm_norm': row(lw['ssm_norm']),
        'w_out': lw['w_out'].astype(BF16),
        'norm_post': row(lw['norm_post']),
    }


def _prompt_layer(x, w):
    b, s, _ = x.shape
    x2 = x.reshape(b * s, D_MODEL)
    cos, sin = _rope_tables(jnp.arange(s, dtype=jnp.int32))
    q, kt, v, ckv, kr, za, mix_s, state, tail = _proj_prompt(x2, cos, sin, w)
    r3 = lambda a: a.reshape(b, s, a.shape[-1])
    mix_a = _attn_prompt(r3(q), kt, r3(v), r3(za))
    y = _merge(mix_a.reshape(b * s, D_ATTN), mix_s, x2, w)
    h = state.reshape(b, SSD_GROUPS, D_STATE, HEADS_PER_GROUP, SSD_HEADDIM)
    h = jnp.transpose(h, (0, 1, 3, 4, 2)).reshape(b, SSD_HEADS, SSD_HEADDIM, D_STATE)
    return (y.reshape(b, s, D_MODEL), r3(ckv), jnp.swapaxes(kr, 1, 2), tail[:, CONV_PAD - (CONV_W - 1):, :], h)


def _sample_layer(layer, x, cache_c, cache_r, conv_prev, h0, page_table, w):
    b, s, _ = x.shape
    n = b * s
    past = page_table.shape[1] * PAGE_SIZE
    x2 = x.reshape(n, D_MODEL)
    pos = past + jnp.arange(s, dtype=jnp.int32)
    cos, sin = _rope_tables(jnp.tile(pos, b))
    qlat, qrope, ckv, kr, za, zs, xbc, misc = _proj_sample(x2, cos, sin, w)
    conv_new, act, xdt_t, dec_t = _post_sample(xbc, misc, jnp.swapaxes(conv_prev, 0, 1), w)
    olat, h, y_t = _attn_sample(
        layer, page_table, qlat.reshape(n, MLA_HEADS, KV_RANK), qrope.reshape(n, MLA_HEADS, QK_ROPE),
        ckv.reshape(n, 1, KV_RANK), kr.reshape(n, 1, QK_ROPE), h0.astype(F32).reshape(n, D_SSM, D_STATE),
        xdt_t, dec_t, act, cache_c, jnp.swapaxes(cache_r, 2, 3))
    y = _finish_sample(y_t, act, zs, olat.reshape(n, MLA_HEADS * KV_RANK), za, x2, w)
    return (y.reshape(b, s, D_MODEL), ckv.reshape(b, s, KV_RANK), kr.reshape(b, s, QK_ROPE),
            jnp.swapaxes(conv_new, 0, 1), h.reshape(b, SSD_HEADS, SSD_HEADDIM, D_STATE))


def kernel(x_prompt, x_sample, cache_ckv, cache_krope, state_conv, state_ssm, page_table, norm_pre, w_in,
           q_a_norm, w_q_b, kv_a_norm, w_uk, w_uv, conv_w, conv_b, dt_bias, a_log, d_skip, ssm_norm, w_out,
           norm_post):
    assert x_sample.shape[1] == 1, "the sample path handles one new token per sequence"
    depth = w_in.shape[0]
    y_prompt, y_sample = x_prompt, x_sample
    outs = [[] for _ in range(8)]
    for l in range(depth):
        w = _prep_weights({'norm_pre': norm_pre[l], 'w_in': w_in[l], 'q_a_norm': q_a_norm[l],
                           'w_q_b': w_q_b[l], 'kv_a_norm': kv_a_norm[l], 'w_uk': w_uk[l], 'w_uv': w_uv[l],
                           'conv_w': conv_w[l], 'conv_b': conv_b[l], 'dt_bias': dt_bias[l], 'a_log': a_log[l],
                           'd_skip': d_skip[l], 'ssm_norm': ssm_norm[l], 'w_out': w_out[l],
                           'norm_post': norm_post[l]})
        y_prompt, c1, k1, v1, h1 = _prompt_layer(y_prompt, w)
        y_sample, c2, k2, v2, h2 = _sample_layer(l, y_sample, cache_ckv, cache_krope, state_conv[l],
                                                 state_ssm[l], page_table, w)
        for lst, val in zip(outs, (c1, k1, v1, h1, c2, k2, v2, h2)):
            lst.append(val)
    return (y_prompt, y_sample) + tuple(jnp.stack(o) for o in outs)
```
